```python
import jax, jax.numpy as jnp
from jax import lax
import numpy as np

D_MODEL = 1024
BATCH = 8
SEQ = 2048
DEPTH = 2
DEC_BATCH = 128
DEC_SEQ = 1
PAST_LEN = 16384
PAGE_SIZE = 128

MIX = 2 * D_MODEL
RET_HEADS = 8
RET_WIDTH = MIX // 2
RET_DK = RET_WIDTH // RET_HEADS
RET_DV = RET_WIDTH // RET_HEADS
GM_HEADS = 8
GM_WIDTH = MIX - RET_WIDTH
GM_DIM = GM_WIDTH // GM_HEADS
CHUNK = 128
ROPE_BASE = 10000.0
EPS = 1e-6
SPLITS = [RET_WIDTH, 2 * RET_WIDTH, 3 * RET_WIDTH, 4 * RET_WIDTH,
          4 * RET_WIDTH + GM_WIDTH, 4 * RET_WIDTH + 2 * GM_WIDTH]
IN_COLS = 4 * RET_WIDTH + 3 * GM_WIDTH

kernel_name = "hybrid_retention_gmlp_decoder_step"


def _rmsnorm(x, g):
    xf = x.astype(jnp.float32)
    y = xf * lax.rsqrt(jnp.mean(xf * xf, axis=-1, keepdims=True) + EPS) * g.astype(jnp.float32)
    return y.astype(x.dtype)


def _layernorm(x, g, b):
    xf = x.astype(jnp.float32)
    mu = jnp.mean(xf, axis=-1, keepdims=True)
    var = jnp.mean(jnp.square(xf - mu), axis=-1, keepdims=True)
    y = (xf - mu) * lax.rsqrt(var + EPS) * g.astype(jnp.float32) + b.astype(jnp.float32)
    return y.astype(x.dtype)


def _rope(x, pos):
    d = x.shape[-1]
    inv = ROPE_BASE ** (-jnp.arange(0, d, 2, dtype=jnp.float32) / d)
    ang = pos[:, None] * inv[None, :]
    c = jnp.cos(ang)[:, None, :]
    s = jnp.sin(ang)[:, None, :]
    xf = x.astype(jnp.float32)
    x1, x2 = xf[..., : d // 2], xf[..., d // 2:]
    return jnp.concatenate([x1 * c - x2 * s, x1 * s + x2 * c], axis=-1)


def _log_gamma():
    return jnp.log(1.0 - 2.0 ** (-5.0 - jnp.arange(RET_HEADS, dtype=jnp.float32)))


def _retention(q, k, v, S0):
    B, T, H, _ = q.shape
    c = CHUNK if T % CHUNK == 0 else T
    n = T // c
    lg = _log_gamma()
    idx = jnp.arange(c, dtype=jnp.float32)
    diff = idx[:, None] - idx[None, :]
    decay = jnp.where(diff >= 0, jnp.exp(lg[:, None, None] * jnp.maximum(diff, 0.0)), 0.0)
    q_dec = jnp.exp(lg[None, :] * (idx[:, None] + 1.0))
    k_dec = jnp.exp(lg[None, :] * (c - 1.0 - idx[:, None]))
    s_dec = jnp.exp(lg * c)

    def to_chunks(a):
        return a.astype(jnp.float32).reshape(B, n, c, H, -1).transpose(1, 0, 2, 3, 4)

    def body(S, blk):
        qc, kc, vc = blk
        sc = jnp.einsum('bihd,bjhd->bhij', qc, kc) * decay
        o = (jnp.einsum('bhij,bjhv->bihv', sc, vc)
             + jnp.einsum('bihd,bhdv->bihv', qc, S) * q_dec[None, :, :, None])
        S = S * s_dec[None, :, None, None] + jnp.einsum('bjhd,bjhv->bhdv', kc * k_dec[None, :, :, None], vc)
        return S, o

    S, o = lax.scan(body, S0.astype(jnp.float32), (to_chunks(q), to_chunks(k), to_chunks(v)))
    o = o.transpose(1, 0, 2, 3, 4).reshape(B, T, H, -1)
    return o, S


def _spatial_gate(v, ws, bs):
    B, T, GH, GD = v.shape
    c = CHUNK if T % CHUNK == 0 else T
    n = T // c
    w = jnp.tril(ws[:, :c, :c])
    b = bs[:, :c].T
    vc = v.reshape(B, n, c, GH, GD)
    s = jnp.einsum('hij,bnjhg->bnihg', w, vc) + b[None, None, :, :, None]
    return s.reshape(B, T, GH, GD)


def _layer(h, pos, S0, norm_g, w_in, w_out, ws, bs, ln_g, ln_b):
    B, T, _ = h.shape
    xn = _rmsnorm(h, norm_g)
    proj = jnp.einsum('btd,de->bte', xn, w_in)
    q, k, v, g_r, u, v_g, g_g = jnp.split(proj, SPLITS, axis=-1)
    q = _rope(q.reshape(B, T, RET_HEADS, RET_DK), pos)
    k = _rope(k.reshape(B, T, RET_HEADS, RET_DK), pos) * (RET_DK ** -0.5)
    v = v.reshape(B, T, RET_HEADS, RET_DV)
    o, S = _retention(q, k, v, S0)
    o = o * lax.rsqrt(jnp.mean(o * o, axis=-1, keepdims=True) + EPS)
    o = o.reshape(B, T, RET_WIDTH).astype(h.dtype) * jax.nn.silu(g_r)
    vn = _layernorm(v_g, ln_g, ln_b).reshape(B, T, GM_HEADS, GM_DIM)
    s = _spatial_gate(vn, ws, bs).reshape(B, T, GM_WIDTH)
    m = u * s.astype(h.dtype) * jax.nn.silu(g_g)
    y = jnp.einsum('bte,ed->btd', jnp.concatenate([o, m], axis=-1), w_out)
    return h + y, S.astype(S0.dtype), vn


def setup_inputs(seed: int = 0) -> dict:
    key = jax.random.key(seed)
    ks = jax.random.split(key, 12)
    f32 = jnp.float32
    return {
        "x_prompt": jax.random.normal(ks[0], (BATCH, SEQ, D_MODEL), f32),
        "x_sample": jax.random.normal(ks[1], (DEC_BATCH, DEC_SEQ, D_MODEL), f32),
        "state_ret": 0.5 * jax.random.normal(ks[2], (DEPTH, DEC_BATCH, RET_HEADS, RET_DK, RET_DV), f32),
        "norm_g": 1.0 + 0.02 * jax.random.normal(ks[3], (DEPTH, D_MODEL), f32),
        "w_in": jax.random.normal(ks[4], (DEPTH, D_MODEL, IN_COLS), f32) * D_MODEL ** -0.5,
        "w_out": jax.random.normal(ks[5], (DEPTH, MIX, D_MODEL), f32) * MIX ** -0.5,
        "gm_ws": jax.random.normal(ks[6], (DEPTH, GM_HEADS, CHUNK, CHUNK), f32) * CHUNK ** -0.5,
        "gm_b": 1.0 + 0.02 * jax.random.normal(ks[7], (DEPTH, GM_HEADS, CHUNK), f32),
        "gm_ln_g": 1.0 + 0.02 * jax.random.normal(ks[8], (DEPTH, GM_WIDTH), f32),
        "gm_ln_b": 0.02 * jax.random.normal(ks[9], (DEPTH, GM_WIDTH), f32),
        "final_g": 1.0 + 0.02 * jax.random.normal(ks[10], (D_MODEL,), f32),
    }


def reference(x_prompt, x_sample, state_ret, norm_g, w_in, w_out, gm_ws, gm_b, gm_ln_g, gm_ln_b, final_g):
    pos_p = jnp.arange(SEQ, dtype=jnp.float32)
    pos_s = PAST_LEN + jnp.arange(DEC_SEQ, dtype=jnp.float32)
    s0_prompt = jnp.zeros((BATCH, RET_HEADS, RET_DK, RET_DV), state_ret.dtype)
    h_p, h_s = x_prompt, x_sample
    sp_list, ss_list, vs_list = [], [], []
    for l in range(DEPTH):
        h_p, S_p, _ = _layer(h_p, pos_p, s0_prompt, norm_g[l], w_in[l], w_out[l],
                             gm_ws[l], gm_b[l], gm_ln_g[l], gm_ln_b[l])
        h_s, S_s, v_s = _layer(h_s, pos_s, state_ret[l], norm_g[l], w_in[l], w_out[l],
                               gm_ws[l], gm_b[l], gm_ln_g[l], gm_ln_b[l])
        sp_list.append(S_p)
        ss_list.append(S_s)
        vs_list.append(v_s)
    y_prompt = _rmsnorm(h_p, final_g)
    y_sample = _rmsnorm(h_s, final_g)
    state_ret_prompt = jnp.stack(sp_list)
    state_ret_sample = jnp.stack(ss_list)
    state_gm_v_sample = jnp.stack(vs_list)
    return (y_prompt, y_sample, state_ret_prompt, state_ret_sample, state_gm_v_sample)
```

```python
import functools

import jax
import jax.numpy as jnp
from jax import lax
from jax.experimental import pallas as pl
from jax.experimental.pallas import tpu as pltpu

F32 = jnp.float32
BF16 = jnp.bfloat16

D_MODEL = 1024
HEADS = 8
HD = 128
WIDTH = HEADS * HD
N_SEG = 7
MIX = 2 * WIDTH
CHUNK = 128
PAST_LEN = 16384
ROPE_BASE = 10000.0
EPS = 1e-6
SEG_Q, SEG_K, SEG_V, SEG_GR, SEG_U, SEG_VG, SEG_GG = range(N_SEG)

GAMMA = tuple(1.0 - 2.0 ** (-5.0 - h) for h in range(HEADS))
CHUNK_DECAY = tuple(g ** CHUNK for g in GAMMA)

ROWS_PER_STEP = 8
VMEM_LIMIT_BYTES = 56 * 1024 * 1024


def _silu(x):
    return x * (1.0 / (1.0 + jnp.exp(-x)))


def _rms_scale(x):
    return x * lax.rsqrt(jnp.mean(x * x, axis=-1, keepdims=True) + EPS)


def _layernorm(x, g, b):
    mu = jnp.mean(x, axis=-1, keepdims=True)
    xc = x - mu
    var = jnp.mean(xc * xc, axis=-1, keepdims=True)
    return xc * lax.rsqrt(var + EPS) * g + b


def _rope(x, cos, sin):
    return x * cos + pltpu.roll(x, HD // 2, 1) * sin


def _head(h):
    return slice(h * HD, (h + 1) * HD)


def _dot(a, b):
    return jnp.dot(a, b, preferred_element_type=F32)


def _prompt_kernel(x_ref, ng_ref, win_ref, wout_ref, cq_ref, sq_ref, ck_ref, sk_ref,
                   qdec_ref, kdec_ref, decay_ref, ws_ref, gb_ref, lng_ref, lnb_ref, fg_ref,
                   y_ref, s_ref,
                   xb, qb, qdb, kb, kdb, vb, gr, ug, vnb, om, wtril, *, final_norm):
    t = pl.program_id(1)

    @pl.when(t == 0)
    def _start_of_row():
        s_ref[...] = jnp.zeros(s_ref.shape, F32)
        row = lax.broadcasted_iota(jnp.int32, (CHUNK, CHUNK), 0)
        col = lax.broadcasted_iota(jnp.int32, (CHUNK, CHUNK), 1)
        for h in range(HEADS):
            wtril[h] = jnp.where(row >= col, ws_ref[h], 0.0).astype(BF16)

    def proj(seg):
        return _dot(xb[...], win_ref[:, seg * WIDTH:(seg + 1) * WIDTH])

    xb[...] = (_rms_scale(x_ref[...]) * ng_ref[...]).astype(BF16)

    acc = proj(SEG_Q)
    cos, sin = cq_ref[...], sq_ref[...]
    for h in range(HEADS):
        r = _rope(acc[:, _head(h)], cos, sin)
        qb[:, _head(h)] = r.astype(BF16)
        qdb[:, _head(h)] = (r * qdec_ref[:, _head(h)]).astype(BF16)

    acc = proj(SEG_K)
    cos, sin = ck_ref[...], sk_ref[...]
    for h in range(HEADS):
        r = _rope(acc[:, _head(h)], cos, sin)
        kb[:, _head(h)] = r.astype(BF16)
        kdb[:, _head(h)] = (r * kdec_ref[:, _head(h)]).astype(BF16)

    vb[...] = proj(SEG_V).astype(BF16)
    gr[...] = _silu(proj(SEG_GR))
    ug[...] = proj(SEG_U)
    ug[...] = ug[...] * _silu(proj(SEG_GG))
    vnb[...] = _layernorm(proj(SEG_VG), lng_ref[...], lnb_ref[...]).astype(BF16)

    for h in range(HEADS):
        hs = _head(h)
        q_h, k_h, v_h = qb[:, hs], kb[:, hs], vb[:, hs]
        sc = lax.dot_general(q_h, k_h, (((1,), (1,)), ((), ())), preferred_element_type=F32)
        s_old = s_ref[h]
        lhs = jnp.concatenate([(sc * decay_ref[h]).astype(BF16), qdb[:, hs]], axis=1)
        rhs = jnp.concatenate([v_h, s_old.astype(BF16)], axis=0)
        o = _dot(lhs, rhs)
        kv = lax.dot_general(kdb[:, hs], v_h, (((0,), (0,)), ((), ())), preferred_element_type=F32)
        s_ref[h] = s_old * CHUNK_DECAY[h] + kv
        om[:, hs] = (_rms_scale(o) * gr[:, hs]).astype(BF16)
        s = _dot(wtril[h], vnb[:, hs]) + gb_ref[:, hs]
        om[:, WIDTH + h * HD:WIDTH + (h + 1) * HD] = (ug[:, hs] * s).astype(BF16)

    y = x_ref[...] + _dot(om[...], wout_ref[...])
    if final_norm:
        y = _rms_scale(y) * fg_ref[...]
    y_ref[...] = y


def _const_spec(shape):
    zeros = (0,) * len(shape)
    return pl.BlockSpec(shape, lambda *_: zeros)


def _resident_spec(shape):
    zeros = (0,) * len(shape)
    return pl.BlockSpec(shape, lambda *_: zeros, pipeline_mode=pl.Buffered(1))


def _prompt_layer(x, ng, win, wout, cq, sq, ck, sk, qdec, kdec, decay, ws, gb, lng, lnb, fg, *, final_norm):
    batch, seq, _ = x.shape
    n_chunks = seq // CHUNK
    row_spec = pl.BlockSpec((None, CHUNK, D_MODEL), lambda b, t: (b, t, 0))
    pos_spec = pl.BlockSpec((CHUNK, HD), lambda b, t: (t, 0))
    bf16_rows = lambda width: pltpu.VMEM((CHUNK, width), BF16)
    return pl.pallas_call(
        functools.partial(_prompt_kernel, final_norm=final_norm),
        grid=(batch, n_chunks),
        in_specs=[
            row_spec,
            _const_spec((1, D_MODEL)),
            _resident_spec((D_MODEL, N_SEG * WIDTH)),
            _resident_spec((MIX, D_MODEL)),
            pos_spec, pos_spec, pos_spec, pos_spec,
            _const_spec((CHUNK, WIDTH)), _const_spec((CHUNK, WIDTH)),
            _const_spec((HEADS, CHUNK, CHUNK)),
            _const_spec((HEADS, CHUNK, CHUNK)),
            _const_spec((CHUNK, WIDTH)),
            _const_spec((1, WIDTH)), _const_spec((1, WIDTH)),
            _const_spec((1, D_MODEL)),
        ],
        out_specs=[
            row_spec,
            pl.BlockSpec((None, HEADS, HD, HD), lambda b, t: (b, 0, 0, 0)),
        ],
        out_shape=[
            jax.ShapeDtypeStruct(x.shape, F32),
            jax.ShapeDtypeStruct((batch, HEADS, HD, HD), F32),
        ],
        scratch_shapes=[
            bf16_rows(D_MODEL),
            bf16_rows(WIDTH), bf16_rows(WIDTH),
            bf16_rows(WIDTH), bf16_rows(WIDTH),
            bf16_rows(WIDTH),
            pltpu.VMEM((CHUNK, WIDTH), F32),
            pltpu.VMEM((CHUNK, WIDTH), F32),
            bf16_rows(WIDTH),
            bf16_rows(MIX),
            pltpu.VMEM((HEADS, CHUNK, CHUNK), BF16),
        ],
        compiler_params=pltpu.CompilerParams(
            dimension_semantics=("arbitrary", "arbitrary"),
            vmem_limit_bytes=VMEM_LIMIT_BYTES,
        ),
        name="prompt_layer",
    )(x, ng, win, wout, cq, sq, ck, sk, qdec, kdec, decay, ws, gb, lng, lnb, fg)


def _sample_kernel(x_ref, s0_ref, ng_ref, win_ref, wout_ref, cq_ref, sq_ref, ck_ref, sk_ref,
                   w0_ref, b0_ref, lng_ref, lnb_ref, fg_ref,
                   y_ref, snew_ref, vn_ref,
                   hs_ref, xb, qf, qb, ktf, vmask, scv, gr, ug, oacc, om):
    layer = pl.program_id(0)
    g = pl.program_id(1)
    n_rows = hs_ref.shape[0]
    last_layer = pl.num_programs(0) - 1
    last_group = pl.num_programs(1) - 1

    @pl.when((layer == 0) & (g == 0))
    def _load_tokens():
        hs_ref[...] = x_ref[...]

    @pl.when(g == 0)
    def _project():
        def proj(seg):
            return _dot(xb[...], win_ref[:, seg * WIDTH:(seg + 1) * WIDTH])

        xb[...] = (_rms_scale(hs_ref[...]) * ng_ref[...]).astype(BF16)
        acc = proj(SEG_Q)
        cos, sin = cq_ref[...], sq_ref[...]
        for h in range(HEADS):
            r = _rope(acc[:, _head(h)], cos, sin)
            qf[:, _head(h)] = r
            qb[:, _head(h)] = r.astype(BF16)
        acc = proj(SEG_K)
        cos, sin = ck_ref[...], sk_ref[...]
        for h in range(HEADS):
            r = _rope(acc[:, _head(h)], cos, sin)
            ktf[h] = r.T
            qk = jnp.sum(qf[:, _head(h)] * r, axis=-1, keepdims=True)
            scv[:, _head(h)] = jnp.broadcast_to(qk, (n_rows, HD))
        acc = proj(SEG_V)
        row = lax.broadcasted_iota(jnp.int32, (n_rows, HD), 0)
        for h in range(HEADS):
            v_h = acc[:, _head(h)]
            scv[:, _head(h)] = scv[:, _head(h)] * v_h
            for j in range(ROWS_PER_STEP):
                vmask[h, :, j * HD:(j + 1) * HD] = jnp.where((row & (ROWS_PER_STEP - 1)) == j, v_h, 0.0).astype(BF16)
        gr[...] = _silu(proj(SEG_GR))
        ug[...] = proj(SEG_U)
        ug[...] = ug[...] * _silu(proj(SEG_GG))
        vn = _layernorm(proj(SEG_VG), lng_ref[...], lnb_ref[...])
        vn_ref[...] = vn
        om[:, WIDTH:] = (ug[...] * (vn * w0_ref[...] + b0_ref[...])).astype(BF16)
        oacc[...] = jnp.zeros(oacc.shape, F32)

    col = lax.broadcasted_iota(jnp.int32, (HD, n_rows), 1)
    in_group = lax.shift_right_logical(col, 3) == g
    pair = pl.multiple_of(lax.shift_right_logical(g, 1) * 16, 16)
    slot0 = (g & 1) * ROWS_PER_STEP
    row16 = lax.broadcasted_iota(jnp.int32, (16, HD), 0)
    for h in range(HEADS):
        hs = _head(h)
        kt_g = jnp.where(in_group, ktf[h], 0.0).astype(BF16)
        kv = _dot(kt_g, vmask[h])
        q16 = qb[pl.ds(pair, 16), hs]
        o16 = jnp.zeros((16, HD), F32)
        for j in range(ROWS_PER_STEP):
            s_old = s0_ref[j, h]
            snew_ref[j, h] = s_old * GAMMA[h] + kv[:, j * HD:(j + 1) * HD]
            r = _dot(q16, s_old.astype(BF16))
            o16 = o16 + jnp.where(row16 == slot0 + j, r, 0.0)
        oacc[pl.ds(pair, 16), hs] = oacc[pl.ds(pair, 16), hs] + o16

    @pl.when(g == last_group)
    def _merge():
        for h in range(HEADS):
            hs = _head(h)
            o = scv[:, hs] + GAMMA[h] * oacc[:, hs]
            om[:, hs] = (_rms_scale(o) * gr[:, hs]).astype(BF16)
        hs_ref[...] = hs_ref[...] + _dot(om[...], wout_ref[...])

    @pl.when((g == last_group) & (layer == last_layer))
    def _final():
        y_ref[...] = _rms_scale(hs_ref[...]) * fg_ref[...]


def _sample_layers(x, state, ng, win, wout, cq, sq, ck, sk, w0, b0, lng, lnb, fg):
    depth = win.shape[0]
    n_rows = x.shape[0]
    n_groups = n_rows // ROWS_PER_STEP
    per_layer_vec = pl.BlockSpec((None, 1, WIDTH), lambda l, g: (l, 0, 0))
    state_spec = pl.BlockSpec((None, ROWS_PER_STEP, HEADS, HD, HD), lambda l, g: (l, g, 0, 0, 0))
    rows = lambda width, dtype: pltpu.VMEM((n_rows, width), dtype)
    return pl.pallas_call(
        _sample_kernel,
        grid=(depth, n_groups),
        in_specs=[
            _const_spec((n_rows, D_MODEL)),
            state_spec,
            per_layer_vec,
            pl.BlockSpec((None, D_MODEL, N_SEG * WIDTH), lambda l, g: (l, 0, 0), pipeline_mode=pl.Buffered(1)),
            pl.BlockSpec((None, MIX, D_MODEL), lambda l, g: (l, 0, 0), pipeline_mode=pl.Buffered(1)),
            _const_spec((1, HD)), _const_spec((1, HD)), _const_spec((1, HD)), _const_spec((1, HD)),
            per_layer_vec, per_layer_vec, per_layer_vec, per_layer_vec,
            _const_spec((1, D_MODEL)),
        ],
        out_specs=[
            _const_spec((n_rows, D_MODEL)),
            state_spec,
            pl.BlockSpec((None, n_rows, WIDTH), lambda l, g: (l, 0, 0)),
        ],
        out_shape=[
            jax.ShapeDtypeStruct((n_rows, D_MODEL), F32),
            jax.ShapeDtypeStruct(state.shape, F32),
            jax.ShapeDtypeStruct((depth, n_rows, WIDTH), F32),
        ],
        scratch_shapes=[
            rows(D_MODEL, F32),
            rows(D_MODEL, BF16),
            rows(WIDTH, F32), rows(WIDTH, BF16),
            pltpu.VMEM((HEADS, HD, n_rows), F32),
            pltpu.VMEM((HEADS, n_rows, ROWS_PER_STEP * HD), BF16),
            rows(WIDTH, F32),
            rows(WIDTH, F32), rows(WIDTH, F32),
            rows(WIDTH, F32),
            rows(MIX, BF16),
        ],
        compiler_params=pltpu.CompilerParams(
            dimension_semantics=("arbitrary", "arbitrary"),
            vmem_limit_bytes=VMEM_LIMIT_BYTES,
        ),
        name="sample_layers",
    )(x, state, ng, win, wout, cq, sq, ck, sk, w0, b0, lng, lnb, fg)


def _rope_tables(pos, scale):
    inv = ROPE_BASE ** (-jnp.arange(0, HD, 2, dtype=F32) / HD)
    ang = pos[:, None] * inv[None, :]
    c, s = jnp.cos(ang), jnp.sin(ang)
    return jnp.concatenate([c, c], axis=-1) * scale, jnp.concatenate([-s, s], axis=-1) * scale


def _retention_tables():
    lg = jnp.log(1.0 - 2.0 ** (-5.0 - jnp.arange(HEADS, dtype=F32)))
    idx = jnp.arange(CHUNK, dtype=F32)
    diff = idx[:, None] - idx[None, :]
    decay = jnp.where(diff >= 0, jnp.exp(lg[:, None, None] * jnp.maximum(diff, 0.0)), 0.0)
    q_dec = jnp.exp(lg[None, :] * (idx[:, None] + 1.0))
    k_dec = jnp.exp(lg[None, :] * (CHUNK - 1.0 - idx[:, None]))
    per_lane = lambda a: jnp.repeat(a, HD, axis=1)
    return decay, per_lane(q_dec), per_lane(k_dec)


def kernel(x_prompt, x_sample, state_ret, norm_g, w_in, w_out, gm_ws, gm_b, gm_ln_g, gm_ln_b, final_g):
    depth = w_in.shape[0]
    batch, seq, _ = x_prompt.shape
    n_rows = x_sample.shape[0]
    assert x_sample.shape[1] == 1 and seq % CHUNK == 0 and n_rows % 16 == 0

    win_b = w_in.astype(BF16)
    wout_b = w_out.astype(BF16)
    k_scale = HD ** -0.5
    pos_p = jnp.arange(seq, dtype=F32)
    pos_s = PAST_LEN + jnp.arange(1, dtype=F32)
    cq_p, sq_p = _rope_tables(pos_p, 1.0)
    ck_p, sk_p = _rope_tables(pos_p, k_scale)
    cq_s, sq_s = _rope_tables(pos_s, 1.0)
    ck_s, sk_s = _rope_tables(pos_s, k_scale)
    decay, q_dec, k_dec = _retention_tables()
    row_vec = lambda a: a.reshape(depth, 1, -1)
    fg = final_g.reshape(1, D_MODEL)

    h_p = x_prompt
    states_p = []
    for l in range(depth):
        gb = jnp.repeat(gm_b[l].T, HD, axis=1)
        h_p, s_p = _prompt_layer(
            h_p, norm_g[l].reshape(1, D_MODEL), win_b[l], wout_b[l], cq_p, sq_p, ck_p, sk_p,
            q_dec, k_dec, decay, gm_ws[l], gb, gm_ln_g[l].reshape(1, WIDTH), gm_ln_b[l].reshape(1, WIDTH), fg,
            final_norm=(l == depth - 1))
        states_p.append(s_p)

    w0 = jnp.repeat(gm_ws[:, :, 0, 0], HD, axis=1).reshape(depth, 1, WIDTH)
    b0 = jnp.repeat(gm_b[:, :, 0], HD, axis=1).reshape(depth, 1, WIDTH)
    y_s, states_s, vn_s = _sample_layers(
        x_sample.reshape(n_rows, D_MODEL), state_ret, row_vec(norm_g), win_b, wout_b,
        cq_s, sq_s, ck_s, sk_s, w0, b0, row_vec(gm_ln_g), row_vec(gm_ln_b), fg)

    return (h_p, y_s.reshape(n_rows, 1, D_MODEL), jnp.stack(states_p), states_s,
            vn_s.reshape(depth, n_rows, 1, HEADS, HD))
```

```python
import functools

import jax
import jax.numpy as jnp
from jax import lax
from jax.experimental import pallas as pl
from jax.experimental.pallas import tpu as pltpu

F32 = jnp.float32
BF16 = jnp.bfloat16

D_MODEL = 1024
HEADS = 8
HD = 128
WIDTH = HEADS * HD
N_SEG = 7
MIX = 2 * WIDTH
CHUNK = 128
PAST_LEN = 16384
ROPE_BASE = 10000.0
EPS = 1e-6
SEG_Q, SEG_K, SEG_V, SEG_GR, SEG_U, SEG_VG, SEG_GG = range(N_SEG)

GAMMA = tuple(1.0 - 2.0 ** (-5.0 - h) for h in range(HEADS))
CHUNK_DECAY = tuple(g ** CHUNK for g in GAMMA)

ROWS_PER_STEP = 8
VMEM_LIMIT_BYTES = 56 * 1024 * 1024


def _silu(x):
    return x * (1.0 / (1.0 + jnp.exp(-x)))


def _rms_scale(x):
    return x * lax.rsqrt(jnp.mean(x * x, axis=-1, keepdims=True) + EPS)


def _layernorm(x, g, b):
    mu = jnp.mean(x, axis=-1, keepdims=True)
    xc = x - mu
    var = jnp.mean(xc * xc, axis=-1, keepdims=True)
    return xc * lax.rsqrt(var + EPS) * g + b


def _rope(x, cos, sin):
    return x * cos + pltpu.roll(x, HD // 2, 1) * sin


def _head(h):
    return slice(h * HD, (h + 1) * HD)


def _dot(a, b):
    return jnp.dot(a, b, preferred_element_type=F32)


def _prompt_kernel(x_ref, ng_ref, win_ref, wout_ref, cq_ref, sq_ref, ck_ref, sk_ref,
                   qdec_ref, kdec_ref, decay_ref, ws_ref, gb_ref, lng_ref, lnb_ref, fg_ref,
                   y_ref, s_ref,
                   xb, qb, qdb, kb, kdb, vb, gr, ug, vnb, om, wtril, sb, kvs, scb, *, final_norm):
    t = pl.program_id(1)

    @pl.when(t == 0)
    def _start_of_row():
        s_ref[...] = jnp.zeros(s_ref.shape, F32)
        row = lax.broadcasted_iota(jnp.int32, (CHUNK, CHUNK), 0)
        col = lax.broadcasted_iota(jnp.int32, (CHUNK, CHUNK), 1)
        for h in range(HEADS):
            wtril[h] = jnp.where(row >= col, ws_ref[h], 0.0).astype(BF16)

    def proj(seg):
        return _dot(xb[...], win_ref[:, seg * WIDTH:(seg + 1) * WIDTH])

    xb[...] = (_rms_scale(x_ref[...]) * ng_ref[...]).astype(BF16)

    acc = proj(SEG_Q)
    cos, sin = cq_ref[...], sq_ref[...]
    for h in range(HEADS):
        r = _rope(acc[:, _head(h)], cos, sin)
        qb[:, _head(h)] = r.astype(BF16)
        qdb[:, _head(h)] = (r * qdec_ref[:, _head(h)]).astype(BF16)

    acc = proj(SEG_K)
    cos, sin = ck_ref[...], sk_ref[...]
    for h in range(HEADS):
        r = _rope(acc[:, _head(h)], cos, sin)
        kb[:, _head(h)] = r.astype(BF16)
        kdb[:, _head(h)] = (r * kdec_ref[:, _head(h)]).astype(BF16)

    vb[...] = proj(SEG_V).astype(BF16)

    for h in range(HEADS):
        hs = _head(h)
        sc = lax.dot_general(qb[:, hs], kb[:, hs], (((1,), (1,)), ((), ())), preferred_element_type=F32)
        scb[h] = (sc * decay_ref[h]).astype(BF16)
        kvs[h] = lax.dot_general(kdb[:, hs], vb[:, hs], (((0,), (0,)), ((), ())), preferred_element_type=F32)
        sb[h] = s_ref[h].astype(BF16)

    gr[...] = _silu(proj(SEG_GR))
    vnb[...] = _layernorm(proj(SEG_VG), lng_ref[...], lnb_ref[...]).astype(BF16)

    for h in range(HEADS):
        hs = _head(h)
        lhs = jnp.concatenate([scb[h], qdb[:, hs]], axis=1)
        rhs = jnp.concatenate([vb[:, hs], sb[h]], axis=0)
        om[:, hs] = (_rms_scale(_dot(lhs, rhs)) * gr[:, hs]).astype(BF16)

    ug[...] = proj(SEG_U)
    ug[...] = ug[...] * _silu(proj(SEG_GG))

    for h in range(HEADS):
        hs = _head(h)
        s = _dot(wtril[h], vnb[:, hs]) + gb_ref[:, hs]
        om[:, WIDTH + h * HD:WIDTH + (h + 1) * HD] = (ug[:, hs] * s).astype(BF16)

    for h in range(HEADS):
        s_ref[h] = s_ref[h] * CHUNK_DECAY[h] + kvs[h]

    y = x_ref[...] + _dot(om[...], wout_ref[...])
    if final_norm:
        y = _rms_scale(y) * fg_ref[...]
    y_ref[...] = y


def _const_spec(shape):
    zeros = (0,) * len(shape)
    return pl.BlockSpec(shape, lambda *_: zeros)


def _resident_spec(shape):
    zeros = (0,) * len(shape)
    return pl.BlockSpec(shape, lambda *_: zeros, pipeline_mode=pl.Buffered(1))


def _prompt_layer(x, ng, win, wout, cq, sq, ck, sk, qdec, kdec, decay, ws, gb, lng, lnb, fg, *, final_norm):
    batch, seq, _ = x.shape
    n_chunks = seq // CHUNK
    row_spec = pl.BlockSpec((None, CHUNK, D_MODEL), lambda b, t: (b, t, 0))
    pos_spec = pl.BlockSpec((CHUNK, HD), lambda b, t: (t, 0))
    bf16_rows = lambda width: pltpu.VMEM((CHUNK, width), BF16)
    return pl.pallas_call(
        functools.partial(_prompt_kernel, final_norm=final_norm),
        grid=(batch, n_chunks),
        in_specs=[
            row_spec,
            _const_spec((1, D_MODEL)),
            _resident_spec((D_MODEL, N_SEG * WIDTH)),
            _resident_spec((MIX, D_MODEL)),
            pos_spec, pos_spec, pos_spec, pos_spec,
            _const_spec((CHUNK, WIDTH)), _const_spec((CHUNK, WIDTH)),
            _const_spec((HEADS, CHUNK, CHUNK)),
            _const_spec((HEADS, CHUNK, CHUNK)),
            _const_spec((CHUNK, WIDTH)),
            _const_spec((1, WIDTH)), _const_spec((1, WIDTH)),
            _const_spec((1, D_MODEL)),
        ],
        out_specs=[
            row_spec,
            pl.BlockSpec((None, HEADS, HD, HD), lambda b, t: (b, 0, 0, 0)),
        ],
        out_shape=[
            jax.ShapeDtypeStruct(x.shape, F32),
            jax.ShapeDtypeStruct((batch, HEADS, HD, HD), F32),
        ],
        scratch_shapes=[
            bf16_rows(D_MODEL),
            bf16_rows(WIDTH), bf16_rows(WIDTH),
            bf16_rows(WIDTH), bf16_rows(WIDTH),
            bf16_rows(WIDTH),
            pltpu.VMEM((CHUNK, WIDTH), F32),
            pltpu.VMEM((CHUNK, WIDTH), F32),
            bf16_rows(WIDTH),
            bf16_rows(MIX),
            pltpu.VMEM((HEADS, CHUNK, CHUNK), BF16),
            pltpu.VMEM((HEADS, HD, HD), BF16),
            pltpu.VMEM((HEADS, HD, HD), F32),
            pltpu.VMEM((HEADS, CHUNK, CHUNK), BF16),
        ],
        compiler_params=pltpu.CompilerParams(
            dimension_semantics=("arbitrary", "arbitrary"),
            vmem_limit_bytes=VMEM_LIMIT_BYTES,
        ),
        name="prompt_layer",
    )(x, ng, win, wout, cq, sq, ck, sk, qdec, kdec, decay, ws, gb, lng, lnb, fg)


def _sample_kernel(x_ref, s0_ref, ng_ref, win_ref, wout_ref, cq_ref, sq_ref, ck_ref, sk_ref,
                   w0_ref, b0_ref, lng_ref, lnb_ref, fg_ref,
                   y_ref, snew_ref, vn_ref,
                   hs_ref, xb, qf, qb, ktf, vmask, scv, gr, ug, oacc, om):
    layer = pl.program_id(0)
    g = pl.program_id(1)
    n_rows = hs_ref.shape[0]
    last_layer = pl.num_programs(0) - 1
    last_group = pl.num_programs(1) - 1

    @pl.when((layer == 0) & (g == 0))
    def _load_tokens():
        hs_ref[...] = x_ref[...]

    @pl.when(g == 0)
    def _project():
        def proj(seg):
            return _dot(xb[...], win_ref[:, seg * WIDTH:(seg + 1) * WIDTH])

        xb[...] = (_rms_scale(hs_ref[...]) * ng_ref[...]).astype(BF16)
        acc = proj(SEG_Q)
        cos, sin = cq_ref[...], sq_ref[...]
        for h in range(HEADS):
            r = _rope(acc[:, _head(h)], cos, sin)
            qf[:, _head(h)] = r
            qb[:, _head(h)] = r.astype(BF16)
        acc = proj(SEG_K)
        cos, sin = ck_ref[...], sk_ref[...]
        for h in range(HEADS):
            r = _rope(acc[:, _head(h)], cos, sin)
            ktf[h] = r.T
            qk = jnp.sum(qf[:, _head(h)] * r, axis=-1, keepdims=True)
            scv[:, _head(h)] = jnp.broadcast_to(qk, (n_rows, HD))
        acc = proj(SEG_V)
        row = lax.broadcasted_iota(jnp.int32, (n_rows, HD), 0)
        for h in range(HEADS):
            v_h = acc[:, _head(h)]
            scv[:, _head(h)] = scv[:, _head(h)] * v_h
            for j in range(ROWS_PER_STEP):
                vmask[h, :, j * HD:(j + 1) * HD] = jnp.where((row & (ROWS_PER_STEP - 1)) == j, v_h, 0.0).astype(BF16)
        gr[...] = _silu(proj(SEG_GR))
        ug[...] = proj(SEG_U)
        ug[...] = ug[...] * _silu(proj(SEG_GG))
        vn = _layernorm(proj(SEG_VG), lng_ref[...], lnb_ref[...])
        vn_ref[...] = vn
        om[:, WIDTH:] = (ug[...] * (vn * w0_ref[...] + b0_ref[...])).astype(BF16)
        oacc[...] = jnp.zeros(oacc.shape, F32)

    col = lax.broadcasted_iota(jnp.int32, (HD, n_rows), 1)
    in_group = lax.shift_right_logical(col, 3) == g
    pair = pl.multiple_of(lax.shift_right_logical(g, 1) * 16, 16)
    slot0 = (g & 1) * ROWS_PER_STEP
    row16 = lax.broadcasted_iota(jnp.int32, (16, HD), 0)
    for h in range(HEADS):
        hs = _head(h)
        kt_g = jnp.where(in_group, ktf[h], 0.0).astype(BF16)
        kv = _dot(kt_g, vmask[h])
        q16 = qb[pl.ds(pair, 16), hs]
        o16 = jnp.zeros((16, HD), F32)
        for j in range(ROWS_PER_STEP):
            s_old = s0_ref[j, h]
            snew_ref[j, h] = s_old * GAMMA[h] + kv[:, j * HD:(j + 1) * HD]
            r = _dot(q16, s_old.astype(BF16))
            o16 = o16 + jnp.where(row16 == slot0 + j, r, 0.0)
        oacc[pl.ds(pair, 16), hs] = oacc[pl.ds(pair, 16), hs] + o16

    @pl.when(g == last_group)
    def _merge():
        for h in range(HEADS):
            hs = _head(h)
            o = scv[:, hs] + GAMMA[h] * oacc[:, hs]
            om[:, hs] = (_rms_scale(o) * gr[:, hs]).astype(BF16)
        hs_ref[...] = hs_ref[...] + _dot(om[...], wout_ref[...])

    @pl.when((g == last_group) & (layer == last_layer))
    def _final():
        y_ref[...] = _rms_scale(hs_ref[...]) * fg_ref[...]


def _sample_layers(x, state, ng, win, wout, cq, sq, ck, sk, w0, b0, lng, lnb, fg):
    depth = win.shape[0]
    n_rows = x.shape[0]
    n_groups = n_rows // ROWS_PER_STEP
    per_layer_vec = pl.BlockSpec((None, 1, WIDTH), lambda l, g: (l, 0, 0))
    state_spec = pl.BlockSpec((None, ROWS_PER_STEP, HEADS, HD, HD), lambda l, g: (l, g, 0, 0, 0))
    rows = lambda width, dtype: pltpu.VMEM((n_rows, width), dtype)
    return pl.pallas_call(
        _sample_kernel,
        grid=(depth, n_groups),
        in_specs=[
            _const_spec((n_rows, D_MODEL)),
            state_spec,
            per_layer_vec,
            pl.BlockSpec((None, D_MODEL, N_SEG * WIDTH), lambda l, g: (l, 0, 0), pipeline_mode=pl.Buffered(1)),
            pl.BlockSpec((None, MIX, D_MODEL), lambda l, g: (l, 0, 0), pipeline_mode=pl.Buffered(1)),
            _const_spec((1, HD)), _const_spec((1, HD)), _const_spec((1, HD)), _const_spec((1, HD)),
            per_layer_vec, per_layer_vec, per_layer_vec, per_layer_vec,
            _const_spec((1, D_MODEL)),
        ],
        out_specs=[
            _const_spec((n_rows, D_MODEL)),
            state_spec,
            pl.BlockSpec((None, n_rows, WIDTH), lambda l, g: (l, 0, 0)),
        ],
        out_shape=[
            jax.ShapeDtypeStruct((n_rows, D_MODEL), F32),
            jax.ShapeDtypeStruct(state.shape, F32),
            jax.ShapeDtypeStruct((depth, n_rows, WIDTH), F32),
        ],
        scratch_shapes=[
            rows(D_MODEL, F32),
            rows(D_MODEL, BF16),
            rows(WIDTH, F32), rows(WIDTH, BF16),
            pltpu.VMEM((HEADS, HD, n_rows), F32),
            pltpu.VMEM((HEADS, n_rows, ROWS_PER_STEP * HD), BF16),
            rows(WIDTH, F32),
            rows(WIDTH, F32), rows(WIDTH, F32),
            rows(WIDTH, F32),
            rows(MIX, BF16),
        ],
        compiler_params=pltpu.CompilerParams(
            dimension_semantics=("arbitrary", "arbitrary"),
            vmem_limit_bytes=VMEM_LIMIT_BYTES,
        ),
        name="sample_layers",
    )(x, state, ng, win, wout, cq, sq, ck, sk, w0, b0, lng, lnb, fg)


def _rope_tables(pos, scale):
    inv = ROPE_BASE ** (-jnp.arange(0, HD, 2, dtype=F32) / HD)
    ang = pos[:, None] * inv[None, :]
    c, s = jnp.cos(ang), jnp.sin(ang)
    return jnp.concatenate([c, c], axis=-1) * scale, jnp.concatenate([-s, s], axis=-1) * scale


def _retention_tables():
    lg = jnp.log(1.0 - 2.0 ** (-5.0 - jnp.arange(HEADS, dtype=F32)))
    idx = jnp.arange(CHUNK, dtype=F32)
    diff = idx[:, None] - idx[None, :]
    decay = jnp.where(diff >= 0, jnp.exp(lg[:, None, None] * jnp.maximum(diff, 0.0)), 0.0)
    q_dec = jnp.exp(lg[None, :] * (idx[:, None] + 1.0))
    k_dec = jnp.exp(lg[None, :] * (CHUNK - 1.0 - idx[:, None]))
    per_lane = lambda a: jnp.repeat(a, HD, axis=1)
    return decay, per_lane(q_dec), per_lane(k_dec)


def kernel(x_prompt, x_sample, state_ret, norm_g, w_in, w_out, gm_ws, gm_b, gm_ln_g, gm_ln_b, final_g):
    depth = w_in.shape[0]
    batch, seq, _ = x_prompt.shape
    n_rows = x_sample.shape[0]
    assert x_sample.shape[1] == 1 and seq % CHUNK == 0 and n_rows % 16 == 0

    win_b = w_in.astype(BF16)
    wout_b = w_out.astype(BF16)
    k_scale = HD ** -0.5
    pos_p = jnp.arange(seq, dtype=F32)
    pos_s = PAST_LEN + jnp.arange(1, dtype=F32)
    cq_p, sq_p = _rope_tables(pos_p, 1.0)
    ck_p, sk_p = _rope_tables(pos_p, k_scale)
    cq_s, sq_s = _rope_tables(pos_s, 1.0)
    ck_s, sk_s = _rope_tables(pos_s, k_scale)
    decay, q_dec, k_dec = _retention_tables()
    row_vec = lambda a: a.reshape(depth, 1, -1)
    fg = final_g.reshape(1, D_MODEL)

    h_p = x_prompt
    states_p = []
    for l in range(depth):
        gb = jnp.repeat(gm_b[l].T, HD, axis=1)
        h_p, s_p = _prompt_layer(
            h_p, norm_g[l].reshape(1, D_MODEL), win_b[l], wout_b[l], cq_p, sq_p, ck_p, sk_p,
            q_dec, k_dec, decay, gm_ws[l], gb, gm_ln_g[l].reshape(1, WIDTH), gm_ln_b[l].reshape(1, WIDTH), fg,
            final_norm=(l == depth - 1))
        states_p.append(s_p)

    w0 = jnp.repeat(gm_ws[:, :, 0, 0], HD, axis=1).reshape(depth, 1, WIDTH)
    b0 = jnp.repeat(gm_b[:, :, 0], HD, axis=1).reshape(depth, 1, WIDTH)
    y_s, states_s, vn_s = _sample_layers(
        x_sample.reshape(n_rows, D_MODEL), state_ret, row_vec(norm_g), win_b, wout_b,
        cq_s, sq_s, ck_s, sk_s, w0, b0, row_vec(gm_ln_g), row_vec(gm_ln_b), fg)

    return (h_p, y_s.reshape(n_rows, 1, D_MODEL), jnp.stack(states_p), states_s,
            vn_s.reshape(depth, n_rows, 1, HEADS, HD))
```

```python
import functools

import jax
import jax.numpy as jnp
from jax import lax
from jax.experimental import pallas as pl
from jax.experimental.pallas import tpu as pltpu

F32 = jnp.float32
BF16 = jnp.bfloat16

D_MODEL = 1024
HEADS = 8
HD = 128
WIDTH = HEADS * HD
N_SEG = 7
MIX = 2 * WIDTH
CHUNK = 128
PAST_LEN = 16384
ROPE_BASE = 10000.0
EPS = 1e-6
SEG_Q, SEG_K, SEG_V, SEG_GR, SEG_U, SEG_VG, SEG_GG = range(N_SEG)

GAMMA = tuple(1.0 - 2.0 ** (-5.0 - h) for h in range(HEADS))
CHUNK_DECAY = tuple(g ** CHUNK for g in GAMMA)

CHUNKS_PER_STEP = 2
ROWS_PER_STEP = 8
VMEM_LIMIT_BYTES = 56 * 1024 * 1024


def _silu(x):
    return x * (1.0 / (1.0 + jnp.exp(-x)))


def _rms_scale(x):
    return x * lax.rsqrt(jnp.mean(x * x, axis=-1, keepdims=True) + EPS)


def _layernorm(x, g, b):
    mu = jnp.mean(x, axis=-1, keepdims=True)
    xc = x - mu
    var = jnp.mean(xc * xc, axis=-1, keepdims=True)
    return xc * lax.rsqrt(var + EPS) * g + b


def _rope(x, cos, sin):
    return x * cos + pltpu.roll(x, HD // 2, 1) * sin


def _head(h):
    return slice(h * HD, (h + 1) * HD)


def _dot(a, b):
    return jnp.dot(a, b, preferred_element_type=F32)


def _chunk_rows(c):
    return slice(c * CHUNK, (c + 1) * CHUNK)


def _prompt_kernel(x_ref, ng_ref, win_ref, wout_ref, cq_ref, sq_ref, ck_ref, sk_ref,
                   qdec_ref, kdec_ref, decay_ref, ws_ref, gb_ref, lng_ref, lnb_ref, fg_ref,
                   y_ref, s_ref,
                   xb, qb, qdb, kb, kdb, vb, gr, ug, vnb, om, wtril, sb, kvs, scb, *, final_norm):
    t = pl.program_id(1)
    n_chunks = x_ref.shape[0] // CHUNK

    @pl.when(t == 0)
    def _start_of_row():
        s_ref[...] = jnp.zeros(s_ref.shape, F32)
        row = lax.broadcasted_iota(jnp.int32, (CHUNK, CHUNK), 0)
        col = lax.broadcasted_iota(jnp.int32, (CHUNK, CHUNK), 1)
        for h in range(HEADS):
            wtril[h] = jnp.where(row >= col, ws_ref[h], 0.0).astype(BF16)

    def proj(seg):
        return _dot(xb[...], win_ref[:, seg * WIDTH:(seg + 1) * WIDTH])

    xb[...] = (_rms_scale(x_ref[...]) * ng_ref[...]).astype(BF16)

    acc = proj(SEG_Q)
    cos, sin = cq_ref[...], sq_ref[...]
    for h in range(HEADS):
        r = _rope(acc[:, _head(h)], cos, sin)
        qb[:, _head(h)] = r.astype(BF16)
        qdb[:, _head(h)] = (r * qdec_ref[:, _head(h)]).astype(BF16)

    acc = proj(SEG_K)
    cos, sin = ck_ref[...], sk_ref[...]
    for h in range(HEADS):
        r = _rope(acc[:, _head(h)], cos, sin)
        kb[:, _head(h)] = r.astype(BF16)
        kdb[:, _head(h)] = (r * kdec_ref[:, _head(h)]).astype(BF16)

    vb[...] = proj(SEG_V).astype(BF16)

    for c in range(n_chunks):
        rows = _chunk_rows(c)
        for h in range(HEADS):
            hs = _head(h)
            sc = lax.dot_general(qb[rows, hs], kb[rows, hs], (((1,), (1,)), ((), ())), preferred_element_type=F32)
            scb[c, h] = (sc * decay_ref[h]).astype(BF16)
            kvs[c, h] = lax.dot_general(kdb[rows, hs], vb[rows, hs], (((0,), (0,)), ((), ())),
                                        preferred_element_type=F32)

    gr[...] = _silu(proj(SEG_GR))
    vnb[...] = _layernorm(proj(SEG_VG), lng_ref[...], lnb_ref[...]).astype(BF16)

    for h in range(HEADS):
        s = s_ref[h]
        for c in range(n_chunks):
            sb[c, h] = s.astype(BF16)
            s = s * CHUNK_DECAY[h] + kvs[c, h]
        s_ref[h] = s

    for c in range(n_chunks):
        rows = _chunk_rows(c)
        for h in range(HEADS):
            hs = _head(h)
            lhs = jnp.concatenate([scb[c, h], qdb[rows, hs]], axis=1)
            rhs = jnp.concatenate([vb[rows, hs], sb[c, h]], axis=0)
            om[rows, hs] = (_rms_scale(_dot(lhs, rhs)) * gr[rows, hs]).astype(BF16)

    ug[...] = proj(SEG_U)
    ug[...] = ug[...] * _silu(proj(SEG_GG))

    for c in range(n_chunks):
        rows = _chunk_rows(c)
        for h in range(HEADS):
            hs = _head(h)
            s = _dot(wtril[h], vnb[rows, hs]) + gb_ref[:, hs]
            om[rows, WIDTH + h * HD:WIDTH + (h + 1) * HD] = (ug[rows, hs] * s).astype(BF16)

    y = x_ref[...] + _dot(om[...], wout_ref[...])
    if final_norm:
        y = _rms_scale(y) * fg_ref[...]
    y_ref[...] = y


def _const_spec(shape):
    zeros = (0,) * len(shape)
    return pl.BlockSpec(shape, lambda *_: zeros)


def _layer_weight_spec(shape, layer):
    index = (layer,) + (0,) * len(shape)
    return pl.BlockSpec((None,) + shape, lambda *_: index, pipeline_mode=pl.Buffered(1))


def _prompt_layer(x, ng, win, wout, cq, sq, ck, sk, qdec, kdec, decay, ws, gb, lng, lnb, fg, *, layer, final_norm):
    batch, seq, _ = x.shape
    tok = CHUNKS_PER_STEP * CHUNK
    row_spec = pl.BlockSpec((None, tok, D_MODEL), lambda b, t: (b, t, 0))
    pos_spec = pl.BlockSpec((tok, HD), lambda b, t: (t, 0))
    bf16_rows = lambda width: pltpu.VMEM((tok, width), BF16)
    per_chunk_head = lambda dtype: pltpu.VMEM((CHUNKS_PER_STEP, HEADS, CHUNK, CHUNK), dtype)
    return pl.pallas_call(
        functools.partial(_prompt_kernel, final_norm=final_norm),
        grid=(batch, seq // tok),
        in_specs=[
            row_spec,
            _const_spec((1, D_MODEL)),
            _layer_weight_spec((D_MODEL, N_SEG * WIDTH), layer),
            _layer_weight_spec((MIX, D_MODEL), layer),
            pos_spec, pos_spec, pos_spec, pos_spec,
            _const_spec((tok, WIDTH)), _const_spec((tok, WIDTH)),
            _const_spec((HEADS, CHUNK, CHUNK)),
            _const_spec((HEADS, CHUNK, CHUNK)),
            _const_spec((CHUNK, WIDTH)),
            _const_spec((1, WIDTH)), _const_spec((1, WIDTH)),
            _const_spec((1, D_MODEL)),
        ],
        out_specs=[
            row_spec,
            pl.BlockSpec((None, HEADS, HD, HD), lambda b, t: (b, 0, 0, 0)),
        ],
        out_shape=[
            jax.ShapeDtypeStruct(x.shape, F32),
            jax.ShapeDtypeStruct((batch, HEADS, HD, HD), F32),
        ],
        scratch_shapes=[
            bf16_rows(D_MODEL),
            bf16_rows(WIDTH), bf16_rows(WIDTH),
            bf16_rows(WIDTH), bf16_rows(WIDTH),
            bf16_rows(WIDTH),
            pltpu.VMEM((tok, WIDTH), F32),
            pltpu.VMEM((tok, WIDTH), F32),
            bf16_rows(WIDTH),
            bf16_rows(MIX),
            pltpu.VMEM((HEADS, CHUNK, CHUNK), BF16),
            per_chunk_head(BF16),
            per_chunk_head(F32),
            per_chunk_head(BF16),
        ],
        compiler_params=pltpu.CompilerParams(
            dimension_semantics=("arbitrary", "arbitrary"),
            vmem_limit_bytes=VMEM_LIMIT_BYTES,
        ),
        name="prompt_layer",
    )(x, ng, win, wout, cq, sq, ck, sk, qdec, kdec, decay, ws, gb, lng, lnb, fg)


def _sample_kernel(x_ref, s0_ref, ng_ref, win_ref, wout_ref, cq_ref, sq_ref, ck_ref, sk_ref,
                   w0_ref, b0_ref, lng_ref, lnb_ref, fg_ref,
                   y_ref, snew_ref, vn_ref,
                   hs_ref, xb, qf, qb, ktf, vmask, scv, gr, ug, oacc, om):
    layer = pl.program_id(0)
    g = pl.program_id(1)
    n_rows = hs_ref.shape[0]
    last_layer = pl.num_programs(0) - 1
    last_group = pl.num_programs(1) - 1

    @pl.when((layer == 0) & (g == 0))
    def _load_tokens():
        hs_ref[...] = x_ref[...]

    @pl.when(g == 0)
    def _project():
        def proj(seg):
            return _dot(xb[...], win_ref[:, seg * WIDTH:(seg + 1) * WIDTH])

        xb[...] = (_rms_scale(hs_ref[...]) * ng_ref[...]).astype(BF16)
        acc = proj(SEG_Q)
        cos, sin = cq_ref[...], sq_ref[...]
        for h in range(HEADS):
            r = _rope(acc[:, _head(h)], cos, sin)
            qf[:, _head(h)] = r
            qb[:, _head(h)] = r.astype(BF16)
        acc = proj(SEG_K)
        cos, sin = ck_ref[...], sk_ref[...]
        for h in range(HEADS):
            r = _rope(acc[:, _head(h)], cos, sin)
            ktf[h] = r.T
            qk = jnp.sum(qf[:, _head(h)] * r, axis=-1, keepdims=True)
            scv[:, _head(h)] = jnp.broadcast_to(qk, (n_rows, HD))
        acc = proj(SEG_V)
        row = lax.broadcasted_iota(jnp.int32, (n_rows, HD), 0)
        for h in range(HEADS):
            v_h = acc[:, _head(h)]
            scv[:, _head(h)] = scv[:, _head(h)] * v_h
            for j in range(ROWS_PER_STEP):
                vmask[h, :, j * HD:(j + 1) * HD] = jnp.where((row & (ROWS_PER_STEP - 1)) == j, v_h, 0.0).astype(BF16)
        gr[...] = _silu(proj(SEG_GR))
        ug[...] = proj(SEG_U)
        ug[...] = ug[...] * _silu(proj(SEG_GG))
        vn = _layernorm(proj(SEG_VG), lng_ref[...], lnb_ref[...])
        vn_ref[...] = vn
        om[:, WIDTH:] = (ug[...] * (vn * w0_ref[...] + b0_ref[...])).astype(BF16)
        oacc[...] = jnp.zeros(oacc.shape, F32)

    col = lax.broadcasted_iota(jnp.int32, (HD, n_rows), 1)
    in_group = lax.shift_right_logical(col, 3) == g
    pair = pl.multiple_of(lax.shift_right_logical(g, 1) * 16, 16)
    slot0 = (g & 1) * ROWS_PER_STEP
    row16 = lax.broadcasted_iota(jnp.int32, (16, HD), 0)
    for h in range(HEADS):
        hs = _head(h)
        kt_g = jnp.where(in_group, ktf[h], 0.0).astype(BF16)
        kv = _dot(kt_g, vmask[h])
        q16 = qb[pl.ds(pair, 16), hs]
        o16 = jnp.zeros((16, HD), F32)
        for j in range(ROWS_PER_STEP):
            s_old = s0_ref[j, h]
            snew_ref[j, h] = s_old * GAMMA[h] + kv[:, j * HD:(j + 1) * HD]
            r = _dot(q16, s_old.astype(BF16))
            o16 = o16 + jnp.where(row16 == slot0 + j, r, 0.0)
        oacc[pl.ds(pair, 16), hs] = oacc[pl.ds(pair, 16), hs] + o16

    @pl.when(g == last_group)
    def _merge():
        for h in range(HEADS):
            hs = _head(h)
            o = scv[:, hs] + GAMMA[h] * oacc[:, hs]
            om[:, hs] = (_rms_scale(o) * gr[:, hs]).astype(BF16)
        hs_ref[...] = hs_ref[...] + _dot(om[...], wout_ref[...])

    @pl.when((g == last_group) & (layer == last_layer))
    def _final():
        y_ref[...] = _rms_scale(hs_ref[...]) * fg_ref[...]


def _sample_layers(x, state, ng, win, wout, cq, sq, ck, sk, w0, b0, lng, lnb, fg):
    depth = win.shape[0]
    n_rows = x.shape[0]
    n_groups = n_rows // ROWS_PER_STEP
    per_layer_vec = pl.BlockSpec((None, 1, WIDTH), lambda l, g: (l, 0, 0))
    state_spec = pl.BlockSpec((None, ROWS_PER_STEP, HEADS, HD, HD), lambda l, g: (l, g, 0, 0, 0))
    rows = lambda width, dtype: pltpu.VMEM((n_rows, width), dtype)
    return pl.pallas_call(
        _sample_kernel,
        grid=(depth, n_groups),
        in_specs=[
            _const_spec((n_rows, D_MODEL)),
            state_spec,
            per_layer_vec,
            pl.BlockSpec((None, D_MODEL, N_SEG * WIDTH), lambda l, g: (l, 0, 0), pipeline_mode=pl.Buffered(1)),
            pl.BlockSpec((None, MIX, D_MODEL), lambda l, g: (l, 0, 0), pipeline_mode=pl.Buffered(1)),
            _const_spec((1, HD)), _const_spec((1, HD)), _const_spec((1, HD)), _const_spec((1, HD)),
            per_layer_vec, per_layer_vec, per_layer_vec, per_layer_vec,
            _const_spec((1, D_MODEL)),
        ],
        out_specs=[
            _const_spec((n_rows, D_MODEL)),
            state_spec,
            pl.BlockSpec((None, n_rows, WIDTH), lambda l, g: (l, 0, 0)),
        ],
        out_shape=[
            jax.ShapeDtypeStruct((n_rows, D_MODEL), F32),
            jax.ShapeDtypeStruct(state.shape, F32),
            jax.ShapeDtypeStruct((depth, n_rows, WIDTH), F32),
        ],
        scratch_shapes=[
            rows(D_MODEL, F32),
            rows(D_MODEL, BF16),
            rows(WIDTH, F32), rows(WIDTH, BF16),
            pltpu.VMEM((HEADS, HD, n_rows), F32),
            pltpu.VMEM((HEADS, n_rows, ROWS_PER_STEP * HD), BF16),
            rows(WIDTH, F32),
            rows(WIDTH, F32), rows(WIDTH, F32),
            rows(WIDTH, F32),
            rows(MIX, BF16),
        ],
        compiler_params=pltpu.CompilerParams(
            dimension_semantics=("arbitrary", "arbitrary"),
            vmem_limit_bytes=VMEM_LIMIT_BYTES,
        ),
        name="sample_layers",
    )(x, state, ng, win, wout, cq, sq, ck, sk, w0, b0, lng, lnb, fg)


def _rope_tables(pos, scale):
    inv = ROPE_BASE ** (-jnp.arange(0, HD, 2, dtype=F32) / HD)
    ang = pos[:, None] * inv[None, :]
    c, s = jnp.cos(ang), jnp.sin(ang)
    return jnp.concatenate([c, c], axis=-1) * scale, jnp.concatenate([-s, s], axis=-1) * scale


def _retention_tables():
    lg = jnp.log(1.0 - 2.0 ** (-5.0 - jnp.arange(HEADS, dtype=F32)))
    idx = jnp.arange(CHUNK, dtype=F32)
    diff = idx[:, None] - idx[None, :]
    decay = jnp.where(diff >= 0, jnp.exp(lg[:, None, None] * jnp.maximum(diff, 0.0)), 0.0)
    q_dec = jnp.exp(lg[None, :] * (idx[:, None] + 1.0))
    k_dec = jnp.exp(lg[None, :] * (CHUNK - 1.0 - idx[:, None]))
    per_lane = lambda a: jnp.repeat(a, HD, axis=1)
    return decay, per_lane(q_dec), per_lane(k_dec)


def kernel(x_prompt, x_sample, state_ret, norm_g, w_in, w_out, gm_ws, gm_b, gm_ln_g, gm_ln_b, final_g):
    depth = w_in.shape[0]
    batch, seq, _ = x_prompt.shape
    n_rows = x_sample.shape[0]
    assert x_sample.shape[1] == 1 and seq % (CHUNKS_PER_STEP * CHUNK) == 0 and n_rows % 16 == 0

    win_b = w_in.astype(BF16)
    wout_b = w_out.astype(BF16)
    k_scale = HD ** -0.5
    pos_p = jnp.arange(seq, dtype=F32)
    pos_s = PAST_LEN + jnp.arange(1, dtype=F32)
    cq_p, sq_p = _rope_tables(pos_p, 1.0)
    ck_p, sk_p = _rope_tables(pos_p, k_scale)
    cq_s, sq_s = _rope_tables(pos_s, 1.0)
    ck_s, sk_s = _rope_tables(pos_s, k_scale)
    decay, q_dec, k_dec = _retention_tables()
    q_dec, k_dec = (jnp.tile(a, (CHUNKS_PER_STEP, 1)) for a in (q_dec, k_dec))
    row_vec = lambda a: a.reshape(depth, 1, -1)
    fg = final_g.reshape(1, D_MODEL)

    h_p = x_prompt
    states_p = []
    for l in range(depth):
        gb = jnp.repeat(gm_b[l].T, HD, axis=1)
        h_p, s_p = _prompt_layer(
            h_p, norm_g[l].reshape(1, D_MODEL), win_b, wout_b, cq_p, sq_p, ck_p, sk_p,
            q_dec, k_dec, decay, gm_ws[l], gb, gm_ln_g[l].reshape(1, WIDTH), gm_ln_b[l].reshape(1, WIDTH), fg,
            layer=l, final_norm=(l == depth - 1))
        states_p.append(s_p)

    w0 = jnp.repeat(gm_ws[:, :, 0, 0], HD, axis=1).reshape(depth, 1, WIDTH)
    b0 = jnp.repeat(gm_b[:, :, 0], HD, axis=1).reshape(depth, 1, WIDTH)
    y_s, states_s, vn_s = _sample_layers(
        x_sample.reshape(n_rows, D_MODEL), state_ret, row_vec(norm_g), win_b, wout_b,
        cq_s, sq_s, ck_s, sk_s, w0, b0, row_vec(gm_ln_g), row_vec(gm_ln_b), fg)

    return (h_p, y_s.reshape(n_rows, 1, D_MODEL), jnp.stack(states_p), states_s,
            vn_s.reshape(depth, n_rows, 1, HEADS, HD))
```

```python
import functools

import jax
import jax.numpy as jnp
from jax import lax
from jax.experimental import pallas as pl
from jax.experimental.pallas import tpu as pltpu

F32 = jnp.float32
BF16 = jnp.bfloat16

D_MODEL = 1024
HEADS = 8
HD = 128
WIDTH = HEADS * HD
N_SEG = 7
MIX = 2 * WIDTH
CHUNK = 128
PAST_LEN = 16384
ROPE_BASE = 10000.0
EPS = 1e-6
SEG_Q, SEG_K, SEG_V, SEG_GR, SEG_U, SEG_VG, SEG_GG = range(N_SEG)

GAMMA = tuple(1.0 - 2.0 ** (-5.0 - h) for h in range(HEADS))
CHUNK_DECAY = tuple(g ** CHUNK for g in GAMMA)

CHUNKS_PER_STEP = 4
ROWS_PER_STEP = 8
VMEM_LIMIT_BYTES = 56 * 1024 * 1024


def _silu(x):
    return x * (1.0 / (1.0 + jnp.exp(-x)))


def _rms_scale(x):
    return x * lax.rsqrt(jnp.mean(x * x, axis=-1, keepdims=True) + EPS)


def _layernorm(x, g, b):
    mu = jnp.mean(x, axis=-1, keepdims=True)
    xc = x - mu
    var = jnp.mean(xc * xc, axis=-1, keepdims=True)
    return xc * lax.rsqrt(var + EPS) * g + b


def _rope(x, cos, sin):
    return x * cos + pltpu.roll(x, HD // 2, 1) * sin


def _head(h):
    return slice(h * HD, (h + 1) * HD)


def _dot(a, b):
    return jnp.dot(a, b, preferred_element_type=F32)


def _chunk_rows(c):
    return slice(c * CHUNK, (c + 1) * CHUNK)


def _prompt_kernel(x_ref, ng_ref, win_ref, wout_ref, cq_ref, sq_ref, ck_ref, sk_ref,
                   qdec_ref, kdec_ref, decay_ref, ws_ref, gb_ref, lng_ref, lnb_ref, fg_ref,
                   y_ref, s_ref,
                   xb, qb, qdb, kb, kdb, vb, gr, ug, vnb, om, wtril, sb, kvs, scb, *, final_norm):
    t = pl.program_id(1)
    n_chunks = x_ref.shape[0] // CHUNK

    @pl.when(t == 0)
    def _start_of_row():
        s_ref[...] = jnp.zeros(s_ref.shape, F32)
        row = lax.broadcasted_iota(jnp.int32, (CHUNK, CHUNK), 0)
        col = lax.broadcasted_iota(jnp.int32, (CHUNK, CHUNK), 1)
        for h in range(HEADS):
            wtril[h] = jnp.where(row >= col, ws_ref[h], 0.0).astype(BF16)

    def proj(seg):
        return _dot(xb[...], win_ref[:, seg * WIDTH:(seg + 1) * WIDTH])

    xb[...] = (_rms_scale(x_ref[...]) * ng_ref[...]).astype(BF16)

    acc = proj(SEG_Q)
    cos, sin = cq_ref[...], sq_ref[...]
    for h in range(HEADS):
        r = _rope(acc[:, _head(h)], cos, sin)
        qb[:, _head(h)] = r.astype(BF16)
        qdb[:, _head(h)] = (r * qdec_ref[:, _head(h)]).astype(BF16)

    acc = proj(SEG_K)
    cos, sin = ck_ref[...], sk_ref[...]
    for h in range(HEADS):
        r = _rope(acc[:, _head(h)], cos, sin)
        kb[:, _head(h)] = r.astype(BF16)
        kdb[:, _head(h)] = (r * kdec_ref[:, _head(h)]).astype(BF16)

    vb[...] = proj(SEG_V).astype(BF16)

    for c in range(n_chunks):
        rows = _chunk_rows(c)
        for h in range(HEADS):
            hs = _head(h)
            sc = lax.dot_general(qb[rows, hs], kb[rows, hs], (((1,), (1,)), ((), ())), preferred_element_type=F32)
            scb[c, h] = (sc * decay_ref[h]).astype(BF16)
            kvs[c, h] = lax.dot_general(kdb[rows, hs], vb[rows, hs], (((0,), (0,)), ((), ())),
                                        preferred_element_type=F32)

    gr[...] = _silu(proj(SEG_GR))
    vnb[...] = _layernorm(proj(SEG_VG), lng_ref[...], lnb_ref[...]).astype(BF16)

    for h in range(HEADS):
        s = s_ref[h]
        for c in range(n_chunks):
            sb[c, h] = s.astype(BF16)
            s = s * CHUNK_DECAY[h] + kvs[c, h]
        s_ref[h] = s

    for c in range(n_chunks):
        rows = _chunk_rows(c)
        for h in range(HEADS):
            hs = _head(h)
            lhs = jnp.concatenate([scb[c, h], qdb[rows, hs]], axis=1)
            rhs = jnp.concatenate([vb[rows, hs], sb[c, h]], axis=0)
            om[rows, hs] = (_rms_scale(_dot(lhs, rhs)) * gr[rows, hs]).astype(BF16)

    ug[...] = proj(SEG_U)
    ug[...] = ug[...] * _silu(proj(SEG_GG))

    for c in range(n_chunks):
        rows = _chunk_rows(c)
        for h in range(HEADS):
            hs = _head(h)
            s = _dot(wtril[h], vnb[rows, hs]) + gb_ref[:, hs]
            om[rows, WIDTH + h * HD:WIDTH + (h + 1) * HD] = (ug[rows, hs] * s).astype(BF16)

    y = x_ref[...] + _dot(om[...], wout_ref[...])
    if final_norm:
        y = _rms_scale(y) * fg_ref[...]
    y_ref[...] = y


def _const_spec(shape):
    zeros = (0,) * len(shape)
    return pl.BlockSpec(shape, lambda *_: zeros)


def _layer_weight_spec(shape, layer):
    index = (layer,) + (0,) * len(shape)
    return pl.BlockSpec((None,) + shape, lambda *_: index, pipeline_mode=pl.Buffered(1))


def _prompt_layer(x, ng, win, wout, cq, sq, ck, sk, qdec, kdec, decay, ws, gb, lng, lnb, fg, *, layer, final_norm):
    batch, seq, _ = x.shape
    tok = CHUNKS_PER_STEP * CHUNK
    row_spec = pl.BlockSpec((None, tok, D_MODEL), lambda b, t: (b, t, 0))
    pos_spec = pl.BlockSpec((tok, HD), lambda b, t: (t, 0))
    bf16_rows = lambda width: pltpu.VMEM((tok, width), BF16)
    per_chunk_head = lambda dtype: pltpu.VMEM((CHUNKS_PER_STEP, HEADS, CHUNK, CHUNK), dtype)
    return pl.pallas_call(
        functools.partial(_prompt_kernel, final_norm=final_norm),
        grid=(batch, seq // tok),
        in_specs=[
            row_spec,
            _const_spec((1, D_MODEL)),
            _layer_weight_spec((D_MODEL, N_SEG * WIDTH), layer),
            _layer_weight_spec((MIX, D_MODEL), layer),
            pos_spec, pos_spec, pos_spec, pos_spec,
            _const_spec((tok, WIDTH)), _const_spec((tok, WIDTH)),
            _const_spec((HEADS, CHUNK, CHUNK)),
            _const_spec((HEADS, CHUNK, CHUNK)),
            _const_spec((CHUNK, WIDTH)),
            _const_spec((1, WIDTH)), _const_spec((1, WIDTH)),
            _const_spec((1, D_MODEL)),
        ],
        out_specs=[
            row_spec,
            pl.BlockSpec((None, HEADS, HD, HD), lambda b, t: (b, 0, 0, 0)),
        ],
        out_shape=[
            jax.ShapeDtypeStruct(x.shape, F32),
            jax.ShapeDtypeStruct((batch, HEADS, HD, HD), F32),
        ],
        scratch_shapes=[
            bf16_rows(D_MODEL),
            bf16_rows(WIDTH), bf16_rows(WIDTH),
            bf16_rows(WIDTH), bf16_rows(WIDTH),
            bf16_rows(WIDTH),
            pltpu.VMEM((tok, WIDTH), F32),
            pltpu.VMEM((tok, WIDTH), F32),
            bf16_rows(WIDTH),
            bf16_rows(MIX),
            pltpu.VMEM((HEADS, CHUNK, CHUNK), BF16),
            per_chunk_head(BF16),
            per_chunk_head(F32),
            per_chunk_head(BF16),
        ],
        compiler_params=pltpu.CompilerParams(
            dimension_semantics=("arbitrary", "arbitrary"),
            vmem_limit_bytes=VMEM_LIMIT_BYTES,
        ),
        name="prompt_layer",
    )(x, ng, win, wout, cq, sq, ck, sk, qdec, kdec, decay, ws, gb, lng, lnb, fg)


def _sample_kernel(x_ref, s0_ref, ng_ref, win_ref, wout_ref, cq_ref, sq_ref, ck_ref, sk_ref,
                   w0_ref, b0_ref, lng_ref, lnb_ref, fg_ref,
                   y_ref, snew_ref, vn_ref,
                   hs_ref, xb, qf, qb, ktf, vmask, scv, gr, ug, oacc, om):
    layer = pl.program_id(0)
    g = pl.program_id(1)
    n_rows = hs_ref.shape[0]
    last_layer = pl.num_programs(0) - 1
    last_group = pl.num_programs(1) - 1

    @pl.when((layer == 0) & (g == 0))
    def _load_tokens():
        hs_ref[...] = x_ref[...]

    @pl.when(g == 0)
    def _project():
        def proj(seg):
            return _dot(xb[...], win_ref[:, seg * WIDTH:(seg + 1) * WIDTH])

        xb[...] = (_rms_scale(hs_ref[...]) * ng_ref[...]).astype(BF16)
        acc = proj(SEG_Q)
        cos, sin = cq_ref[...], sq_ref[...]
        for h in range(HEADS):
            r = _rope(acc[:, _head(h)], cos, sin)
            qf[:, _head(h)] = r
            qb[:, _head(h)] = r.astype(BF16)
        acc = proj(SEG_K)
        cos, sin = ck_ref[...], sk_ref[...]
        for h in range(HEADS):
            r = _rope(acc[:, _head(h)], cos, sin)
            ktf[h] = r.T
            qk = jnp.sum(qf[:, _head(h)] * r, axis=-1, keepdims=True)
            scv[:, _head(h)] = jnp.broadcast_to(qk, (n_rows, HD))
        acc = proj(SEG_V)
        row = lax.broadcasted_iota(jnp.int32, (n_rows, HD), 0)
        for h in range(HEADS):
            v_h = acc[:, _head(h)]
            scv[:, _head(h)] = scv[:, _head(h)] * v_h
            for j in range(ROWS_PER_STEP):
                vmask[h, :, j * HD:(j + 1) * HD] = jnp.where((row & (ROWS_PER_STEP - 1)) == j, v_h, 0.0).astype(BF16)
        gr[...] = _silu(proj(SEG_GR))
        ug[...] = proj(SEG_U)
        ug[...] = ug[...] * _silu(proj(SEG_GG))
        vn = _layernorm(proj(SEG_VG), lng_ref[...], lnb_ref[...])
        vn_ref[...] = vn
        om[:, WIDTH:] = (ug[...] * (vn * w0_ref[...] + b0_ref[...])).astype(BF16)
        oacc[...] = jnp.zeros(oacc.shape, F32)

    col = lax.broadcasted_iota(jnp.int32, (HD, n_rows), 1)
    in_group = lax.shift_right_logical(col, 3) == g
    pair = pl.multiple_of(lax.shift_right_logical(g, 1) * 16, 16)
    slot0 = (g & 1) * ROWS_PER_STEP
    row16 = lax.broadcasted_iota(jnp.int32, (16, HD), 0)
    for h in range(HEADS):
        hs = _head(h)
        kt_g = jnp.where(in_group, ktf[h], 0.0).astype(BF16)
        kv = _dot(kt_g, vmask[h])
        q16 = qb[pl.ds(pair, 16), hs]
        o16 = jnp.zeros((16, HD), F32)
        for j in range(ROWS_PER_STEP):
            s_old = s0_ref[j, h]
            snew_ref[j, h] = s_old * GAMMA[h] + kv[:, j * HD:(j + 1) * HD]
            r = _dot(q16, s_old.astype(BF16))
            o16 = o16 + jnp.where(row16 == slot0 + j, r, 0.0)
        oacc[pl.ds(pair, 16), hs] = oacc[pl.ds(pair, 16), hs] + o16

    @pl.when(g == last_group)
    def _merge():
        for h in range(HEADS):
            hs = _head(h)
            o = scv[:, hs] + GAMMA[h] * oacc[:, hs]
            om[:, hs] = (_rms_scale(o) * gr[:, hs]).astype(BF16)
        hs_ref[...] = hs_ref[...] + _dot(om[...], wout_ref[...])

    @pl.when((g == last_group) & (layer == last_layer))
    def _final():
        y_ref[...] = _rms_scale(hs_ref[...]) * fg_ref[...]


def _sample_layers(x, state, ng, win, wout, cq, sq, ck, sk, w0, b0, lng, lnb, fg):
    depth = win.shape[0]
    n_rows = x.shape[0]
    n_groups = n_rows // ROWS_PER_STEP
    per_layer_vec = pl.BlockSpec((None, 1, WIDTH), lambda l, g: (l, 0, 0))
    state_spec = pl.BlockSpec((None, ROWS_PER_STEP, HEADS, HD, HD), lambda l, g: (l, g, 0, 0, 0))
    rows = lambda width, dtype: pltpu.VMEM((n_rows, width), dtype)
    return pl.pallas_call(
        _sample_kernel,
        grid=(depth, n_groups),
        in_specs=[
            _const_spec((n_rows, D_MODEL)),
            state_spec,
            per_layer_vec,
            pl.BlockSpec((None, D_MODEL, N_SEG * WIDTH), lambda l, g: (l, 0, 0), pipeline_mode=pl.Buffered(1)),
            pl.BlockSpec((None, MIX, D_MODEL), lambda l, g: (l, 0, 0), pipeline_mode=pl.Buffered(1)),
            _const_spec((1, HD)), _const_spec((1, HD)), _const_spec((1, HD)), _const_spec((1, HD)),
            per_layer_vec, per_layer_vec, per_layer_vec, per_layer_vec,
            _const_spec((1, D_MODEL)),
        ],
        out_specs=[
            _const_spec((n_rows, D_MODEL)),
            state_spec,
            pl.BlockSpec((None, n_rows, WIDTH), lambda l, g: (l, 0, 0)),
        ],
        out_shape=[
            jax.ShapeDtypeStruct((n_rows, D_MODEL), F32),
            jax.ShapeDtypeStruct(state.shape, F32),
            jax.ShapeDtypeStruct((depth, n_rows, WIDTH), F32),
        ],
        scratch_shapes=[
            rows(D_MODEL, F32),
            rows(D_MODEL, BF16),
            rows(WIDTH, F32), rows(WIDTH, BF16),
            pltpu.VMEM((HEADS, HD, n_rows), F32),
            pltpu.VMEM((HEADS, n_rows, ROWS_PER_STEP * HD), BF16),
            rows(WIDTH, F32),
            rows(WIDTH, F32), rows(WIDTH, F32),
            rows(WIDTH, F32),
            rows(MIX, BF16),
        ],
        compiler_params=pltpu.CompilerParams(
            dimension_semantics=("arbitrary", "arbitrary"),
            vmem_limit_bytes=VMEM_LIMIT_BYTES,
        ),
        name="sample_layers",
    )(x, state, ng, win, wout, cq, sq, ck, sk, w0, b0, lng, lnb, fg)


def _rope_tables(pos, scale):
    inv = ROPE_BASE ** (-jnp.arange(0, HD, 2, dtype=F32) / HD)
    ang = pos[:, None] * inv[None, :]
    c, s = jnp.cos(ang), jnp.sin(ang)
    return jnp.concatenate([c, c], axis=-1) * scale, jnp.concatenate([-s, s], axis=-1) * scale


def _retention_tables():
    lg = jnp.log(1.0 - 2.0 ** (-5.0 - jnp.arange(HEADS, dtype=F32)))
    idx = jnp.arange(CHUNK, dtype=F32)
    diff = idx[:, None] - idx[None, :]
    decay = jnp.where(diff >= 0, jnp.exp(lg[:, None, None] * jnp.maximum(diff, 0.0)), 0.0)
    q_dec = jnp.exp(lg[None, :] * (idx[:, None] + 1.0))
    k_dec = jnp.exp(lg[None, :] * (CHUNK - 1.0 - idx[:, None]))
    per_lane = lambda a: jnp.repeat(a, HD, axis=1)
    return decay, per_lane(q_dec), per_lane(k_dec)


def kernel(x_prompt, x_sample, state_ret, norm_g, w_in, w_out, gm_ws, gm_b, gm_ln_g, gm_ln_b, final_g):
    depth = w_in.shape[0]
    batch, seq, _ = x_prompt.shape
    n_rows = x_sample.shape[0]
    assert x_sample.shape[1] == 1 and seq % (CHUNKS_PER_STEP * CHUNK) == 0 and n_rows % 16 == 0

    win_b = w_in.astype(BF16)
    wout_b = w_out.astype(BF16)
    k_scale = HD ** -0.5
    pos_p = jnp.arange(seq, dtype=F32)
    pos_s = PAST_LEN + jnp.arange(1, dtype=F32)
    cq_p, sq_p = _rope_tables(pos_p, 1.0)
    ck_p, sk_p = _rope_tables(pos_p, k_scale)
    cq_s, sq_s = _rope_tables(pos_s, 1.0)
    ck_s, sk_s = _rope_tables(pos_s, k_scale)
    decay, q_dec, k_dec = _retention_tables()
    q_dec, k_dec = (jnp.tile(a, (CHUNKS_PER_STEP, 1)) for a in (q_dec, k_dec))
    row_vec = lambda a: a.reshape(depth, 1, -1)
    fg = final_g.reshape(1, D_MODEL)

    h_p = x_prompt
    states_p = []
    for l in range(depth):
        gb = jnp.repeat(gm_b[l].T, HD, axis=1)
        h_p, s_p = _prompt_layer(
            h_p, norm_g[l].reshape(1, D_MODEL), win_b, wout_b, cq_p, sq_p, ck_p, sk_p,
            q_dec, k_dec, decay, gm_ws[l], gb, gm_ln_g[l].reshape(1, WIDTH), gm_ln_b[l].reshape(1, WIDTH), fg,
            layer=l, final_norm=(l == depth - 1))
        states_p.append(s_p)

    w0 = jnp.repeat(gm_ws[:, :, 0, 0], HD, axis=1).reshape(depth, 1, WIDTH)
    b0 = jnp.repeat(gm_b[:, :, 0], HD, axis=1).reshape(depth, 1, WIDTH)
    y_s, states_s, vn_s = _sample_layers(
        x_sample.reshape(n_rows, D_MODEL), state_ret, row_vec(norm_g), win_b, wout_b,
        cq_s, sq_s, ck_s, sk_s, w0, b0, row_vec(gm_ln_g), row_vec(gm_ln_b), fg)

    return (h_p, y_s.reshape(n_rows, 1, D_MODEL), jnp.stack(states_p), states_s,
            vn_s.reshape(depth, n_rows, 1, HEADS, HD))
```

```python
import functools

import jax
import jax.numpy as jnp
import numpy as np
from jax import lax
from jax.experimental import pallas as pl
from jax.experimental.pallas import tpu as pltpu

F32 = jnp.float32
BF16 = jnp.bfloat16

D_MODEL = 1024
HEADS = 8
HD = 128
WIDTH = HEADS * HD
N_SEG = 7
MIX = 2 * WIDTH
CHUNK = 128
PAST_LEN = 16384
ROPE_BASE = 10000.0
EPS = 1e-6
SEG_Q, SEG_K, SEG_V, SEG_GR, SEG_U, SEG_VG, SEG_GG = range(N_SEG)

GAMMA = tuple(1.0 - 2.0 ** (-5.0 - h) for h in range(HEADS))
CHUNK_DECAY = tuple(g ** CHUNK for g in GAMMA)

CHUNKS_PER_STEP = 4
ROWS_PER_STEP = 8
VMEM_LIMIT_BYTES = 56 * 1024 * 1024


def _silu(x):
    return x * (1.0 / (1.0 + jnp.exp(-x)))


def _rms_scale(x):
    return x * lax.rsqrt(jnp.mean(x * x, axis=-1, keepdims=True) + EPS)


def _layernorm(x, g, b):
    mu = jnp.mean(x, axis=-1, keepdims=True)
    xc = x - mu
    var = jnp.mean(xc * xc, axis=-1, keepdims=True)
    return xc * lax.rsqrt(var + EPS) * g + b


def _rope(x, cos, sin):
    return x * cos + pltpu.roll(x, HD // 2, 1) * sin


def _head(h):
    return slice(h * HD, (h + 1) * HD)


def _dot(a, b):
    return jnp.dot(a, b, preferred_element_type=F32)


def _chunk_rows(c):
    return slice(c * CHUNK, (c + 1) * CHUNK)


def _prompt_kernel(x_ref, ng_ref, win_ref, wout_ref, cq_ref, sq_ref, ck_ref, sk_ref,
                   qdec_ref, kdec_ref, mask_ref, ws_ref, gb_ref, lng_ref, lnb_ref, fg_ref,
                   y_ref, s_ref,
                   xb, qdb, kdt, vb, gr, ug, vnb, om, wtril, sb, kvs, scb, *, final_norm):
    t = pl.program_id(1)
    n_chunks = x_ref.shape[0] // CHUNK

    @pl.when(t == 0)
    def _start_of_row():
        s_ref[...] = jnp.zeros(s_ref.shape, F32)
        row = lax.broadcasted_iota(jnp.int32, (CHUNK, CHUNK), 0)
        col = lax.broadcasted_iota(jnp.int32, (CHUNK, CHUNK), 1)
        for h in range(HEADS):
            wtril[h] = jnp.where(row >= col, ws_ref[h], 0.0).astype(BF16)

    def proj(seg):
        return _dot(xb[...], win_ref[:, seg * WIDTH:(seg + 1) * WIDTH])

    xb[...] = (_rms_scale(x_ref[...]) * ng_ref[...]).astype(BF16)

    acc = proj(SEG_Q)
    cos, sin = cq_ref[...], sq_ref[...]
    for h in range(HEADS):
        r = _rope(acc[:, _head(h)], cos, sin)
        qdb[:, _head(h)] = (r * qdec_ref[:, _head(h)]).astype(BF16)

    acc = proj(SEG_K)
    cos, sin = ck_ref[...], sk_ref[...]
    for h in range(HEADS):
        kd = _rope(acc[:, _head(h)], cos, sin) * kdec_ref[:, _head(h)]
        for c in range(n_chunks):
            kdt[c, h] = kd[_chunk_rows(c), :].T.astype(BF16)

    vb[...] = proj(SEG_V).astype(BF16)

    for c in range(n_chunks):
        rows = _chunk_rows(c)
        for h in range(HEADS):
            hs = _head(h)
            scb[c, h] = (_dot(qdb[rows, hs], kdt[c, h]) * mask_ref[h]).astype(BF16)
            kvs[c, h] = _dot(kdt[c, h], vb[rows, hs])

    gr[...] = _silu(proj(SEG_GR))
    vnb[...] = _layernorm(proj(SEG_VG), lng_ref[...], lnb_ref[...]).astype(BF16)

    for h in range(HEADS):
        s = s_ref[h]
        for c in range(n_chunks):
            sb[c, h] = s.astype(BF16)
            s = s * CHUNK_DECAY[h] + kvs[c, h]
        s_ref[h] = s

    for c in range(n_chunks):
        rows = _chunk_rows(c)
        for h in range(HEADS):
            hs = _head(h)
            lhs = jnp.concatenate([scb[c, h], qdb[rows, hs]], axis=1)
            rhs = jnp.concatenate([vb[rows, hs], sb[c, h]], axis=0)
            om[rows, hs] = (_rms_scale(_dot(lhs, rhs)) * gr[rows, hs]).astype(BF16)

    ug[...] = proj(SEG_U)
    ug[...] = ug[...] * _silu(proj(SEG_GG))

    for c in range(n_chunks):
        rows = _chunk_rows(c)
        for h in range(HEADS):
            hs = _head(h)
            s = _dot(wtril[h], vnb[rows, hs]) + gb_ref[:, hs]
            om[rows, WIDTH + h * HD:WIDTH + (h + 1) * HD] = (ug[rows, hs] * s).astype(BF16)

    y = x_ref[...] + _dot(om[...], wout_ref[...])
    if final_norm:
        y = _rms_scale(y) * fg_ref[...]
    y_ref[...] = y


def _const_spec(shape):
    zeros = (0,) * len(shape)
    return pl.BlockSpec(shape, lambda *_: zeros)


def _layer_weight_spec(shape, layer):
    index = (layer,) + (0,) * len(shape)
    return pl.BlockSpec((None,) + shape, lambda *_: index, pipeline_mode=pl.Buffered(1))


def _prompt_layer(x, ng, win, wout, cq, sq, ck, sk, qdec, kdec, mask, ws, gb, lng, lnb, fg, *, layer, final_norm):
    batch, seq, _ = x.shape
    tok = CHUNKS_PER_STEP * CHUNK
    row_spec = pl.BlockSpec((None, tok, D_MODEL), lambda b, t: (b, t, 0))
    pos_spec = pl.BlockSpec((tok, HD), lambda b, t: (t, 0))
    bf16_rows = lambda width: pltpu.VMEM((tok, width), BF16)
    per_chunk_head = lambda dtype: pltpu.VMEM((CHUNKS_PER_STEP, HEADS, CHUNK, CHUNK), dtype)
    return pl.pallas_call(
        functools.partial(_prompt_kernel, final_norm=final_norm),
        grid=(batch, seq // tok),
        in_specs=[
            row_spec,
            _const_spec((1, D_MODEL)),
            _layer_weight_spec((D_MODEL, N_SEG * WIDTH), layer),
            _layer_weight_spec((MIX, D_MODEL), layer),
            pos_spec, pos_spec, pos_spec, pos_spec,
            _const_spec((tok, WIDTH)), _const_spec((tok, WIDTH)),
            _const_spec((HEADS, CHUNK, CHUNK)),
            _const_spec((HEADS, CHUNK, CHUNK)),
            _const_spec((CHUNK, WIDTH)),
            _const_spec((1, WIDTH)), _const_spec((1, WIDTH)),
            _const_spec((1, D_MODEL)),
        ],
        out_specs=[
            row_spec,
            pl.BlockSpec((None, HEADS, HD, HD), lambda b, t: (b, 0, 0, 0)),
        ],
        out_shape=[
            jax.ShapeDtypeStruct(x.shape, F32),
            jax.ShapeDtypeStruct((batch, HEADS, HD, HD), F32),
        ],
        scratch_shapes=[
            bf16_rows(D_MODEL),
            bf16_rows(WIDTH),
            per_chunk_head(BF16),
            bf16_rows(WIDTH),
            pltpu.VMEM((tok, WIDTH), F32),
            pltpu.VMEM((tok, WIDTH), F32),
            bf16_rows(WIDTH),
            bf16_rows(MIX),
            pltpu.VMEM((HEADS, CHUNK, CHUNK), BF16),
            per_chunk_head(BF16),
            per_chunk_head(F32),
            per_chunk_head(BF16),
        ],
        compiler_params=pltpu.CompilerParams(
            dimension_semantics=("arbitrary", "arbitrary"),
            vmem_limit_bytes=VMEM_LIMIT_BYTES,
        ),
        name="prompt_layer",
    )(x, ng, win, wout, cq, sq, ck, sk, qdec, kdec, mask, ws, gb, lng, lnb, fg)


def _sample_kernel(x_ref, s0_ref, ng_ref, win_ref, wout_ref, cq_ref, sq_ref, ck_ref, sk_ref,
                   w0_ref, b0_ref, lng_ref, lnb_ref, fg_ref,
                   y_ref, snew_ref, vn_ref,
                   hs_ref, xb, qf, qb, ktf, vmask, scv, gr, ug, oacc, om):
    layer = pl.program_id(0)
    g = pl.program_id(1)
    n_rows = hs_ref.shape[0]
    last_layer = pl.num_programs(0) - 1
    last_group = pl.num_programs(1) - 1

    @pl.when((layer == 0) & (g == 0))
    def _load_tokens():
        hs_ref[...] = x_ref[...]

    @pl.when(g == 0)
    def _project():
        def proj(seg):
            return _dot(xb[...], win_ref[:, seg * WIDTH:(seg + 1) * WIDTH])

        xb[...] = (_rms_scale(hs_ref[...]) * ng_ref[...]).astype(BF16)
        acc = proj(SEG_Q)
        cos, sin = cq_ref[...], sq_ref[...]
        for h in range(HEADS):
            r = _rope(acc[:, _head(h)], cos, sin)
            qf[:, _head(h)] = r
            qb[:, _head(h)] = r.astype(BF16)
        acc = proj(SEG_K)
        cos, sin = ck_ref[...], sk_ref[...]
        for h in range(HEADS):
            r = _rope(acc[:, _head(h)], cos, sin)
            ktf[h] = r.T
            qk = jnp.sum(qf[:, _head(h)] * r, axis=-1, keepdims=True)
            scv[:, _head(h)] = jnp.broadcast_to(qk, (n_rows, HD))
        acc = proj(SEG_V)
        row = lax.broadcasted_iota(jnp.int32, (n_rows, HD), 0)
        for h in range(HEADS):
            v_h = acc[:, _head(h)]
            scv[:, _head(h)] = scv[:, _head(h)] * v_h
            for j in range(ROWS_PER_STEP):
                vmask[h, :, j * HD:(j + 1) * HD] = jnp.where((row & (ROWS_PER_STEP - 1)) == j, v_h, 0.0).astype(BF16)
        gr[...] = _silu(proj(SEG_GR))
        ug[...] = proj(SEG_U)
        ug[...] = ug[...] * _silu(proj(SEG_GG))
        vn = _layernorm(proj(SEG_VG), lng_ref[...], lnb_ref[...])
        vn_ref[...] = vn
        om[:, WIDTH:] = (ug[...] * (vn * w0_ref[...] + b0_ref[...])).astype(BF16)
        oacc[...] = jnp.zeros(oacc.shape, F32)

    col = lax.broadcasted_iota(jnp.int32, (HD, n_rows), 1)
    in_group = lax.shift_right_logical(col, 3) == g
    pair = pl.multiple_of(lax.shift_right_logical(g, 1) * 16, 16)
    slot0 = (g & 1) * ROWS_PER_STEP
    row16 = lax.broadcasted_iota(jnp.int32, (16, HD), 0)
    for h in range(HEADS):
        hs = _head(h)
        kt_g = jnp.where(in_group, ktf[h], 0.0).astype(BF16)
        kv = _dot(kt_g, vmask[h])
        q16 = qb[pl.ds(pair, 16), hs]
        o16 = jnp.zeros((16, HD), F32)
        for j in range(ROWS_PER_STEP):
            s_old = s0_ref[j, h]
            snew_ref[j, h] = s_old * GAMMA[h] + kv[:, j * HD:(j + 1) * HD]
            r = _dot(q16, s_old.astype(BF16))
            o16 = o16 + jnp.where(row16 == slot0 + j, r, 0.0)
        oacc[pl.ds(pair, 16), hs] = oacc[pl.ds(pair, 16), hs] + o16

    @pl.when(g == last_group)
    def _merge():
        for h in range(HEADS):
            hs = _head(h)
            o = scv[:, hs] + GAMMA[h] * oacc[:, hs]
            om[:, hs] = (_rms_scale(o) * gr[:, hs]).astype(BF16)
        hs_ref[...] = hs_ref[...] + _dot(om[...], wout_ref[...])

    @pl.when((g == last_group) & (layer == last_layer))
    def _final():
        y_ref[...] = _rms_scale(hs_ref[...]) * fg_ref[...]


def _sample_layers(x, state, ng, win, wout, cq, sq, ck, sk, w0, b0, lng, lnb, fg):
    depth = win.shape[0]
    n_rows = x.shape[0]
    n_groups = n_rows // ROWS_PER_STEP
    per_layer_vec = pl.BlockSpec((None, 1, WIDTH), lambda l, g: (l, 0, 0))
    state_spec = pl.BlockSpec((None, ROWS_PER_STEP, HEADS, HD, HD), lambda l, g: (l, g, 0, 0, 0))
    rows = lambda width, dtype: pltpu.VMEM((n_rows, width), dtype)
    return pl.pallas_call(
        _sample_kernel,
        grid=(depth, n_groups),
        in_specs=[
            _const_spec((n_rows, D_MODEL)),
            state_spec,
            per_layer_vec,
            pl.BlockSpec((None, D_MODEL, N_SEG * WIDTH), lambda l, g: (l, 0, 0), pipeline_mode=pl.Buffered(1)),
            pl.BlockSpec((None, MIX, D_MODEL), lambda l, g: (l, 0, 0), pipeline_mode=pl.Buffered(1)),
            _const_spec((1, HD)), _const_spec((1, HD)), _const_spec((1, HD)), _const_spec((1, HD)),
            per_layer_vec, per_layer_vec, per_layer_vec, per_layer_vec,
            _const_spec((1, D_MODEL)),
        ],
        out_specs=[
            _const_spec((n_rows, D_MODEL)),
            state_spec,
            pl.BlockSpec((None, n_rows, WIDTH), lambda l, g: (l, 0, 0)),
        ],
        out_shape=[
            jax.ShapeDtypeStruct((n_rows, D_MODEL), F32),
            jax.ShapeDtypeStruct(state.shape, F32),
            jax.ShapeDtypeStruct((depth, n_rows, WIDTH), F32),
        ],
        scratch_shapes=[
            rows(D_MODEL, F32),
            rows(D_MODEL, BF16),
            rows(WIDTH, F32), rows(WIDTH, BF16),
            pltpu.VMEM((HEADS, HD, n_rows), F32),
            pltpu.VMEM((HEADS, n_rows, ROWS_PER_STEP * HD), BF16),
            rows(WIDTH, F32),
            rows(WIDTH, F32), rows(WIDTH, F32),
            rows(WIDTH, F32),
            rows(MIX, BF16),
        ],
        compiler_params=pltpu.CompilerParams(
            dimension_semantics=("arbitrary", "arbitrary"),
            vmem_limit_bytes=VMEM_LIMIT_BYTES,
        ),
        name="sample_layers",
    )(x, state, ng, win, wout, cq, sq, ck, sk, w0, b0, lng, lnb, fg)


def _rope_tables(pos, scale):
    inv = ROPE_BASE ** (-np.arange(0, HD, 2, dtype=np.float64) / HD)
    ang = np.asarray(pos, np.float64)[:, None] * inv[None, :]
    c, s = np.cos(ang) * scale, np.sin(ang) * scale
    return (jnp.asarray(np.concatenate([c, c], axis=-1), F32),
            jnp.asarray(np.concatenate([-s, s], axis=-1), F32))


def _retention_tables(rows):
    lg = np.log(np.asarray(GAMMA, np.float64))
    idx = np.arange(CHUNK, dtype=np.float64)
    causal = idx[:, None] >= idx[None, :]
    mask = np.where(causal[None], np.exp(-lg * CHUNK)[:, None, None], 0.0)
    q_dec = np.exp(lg[None, :] * (idx[:, None] + 1.0))
    k_dec = np.exp(lg[None, :] * (CHUNK - 1.0 - idx[:, None]))
    per_lane = lambda a: np.tile(np.repeat(a, HD, axis=1), (rows // CHUNK, 1))
    return jnp.asarray(mask, F32), jnp.asarray(per_lane(q_dec), F32), jnp.asarray(per_lane(k_dec), F32)


def kernel(x_prompt, x_sample, state_ret, norm_g, w_in, w_out, gm_ws, gm_b, gm_ln_g, gm_ln_b, final_g):
    depth = w_in.shape[0]
    batch, seq, _ = x_prompt.shape
    n_rows = x_sample.shape[0]
    assert x_sample.shape[1] == 1 and seq % (CHUNKS_PER_STEP * CHUNK) == 0 and n_rows % 16 == 0

    win_b = w_in.astype(BF16)
    wout_b = w_out.astype(BF16)
    k_scale = HD ** -0.5
    pos_p = np.arange(seq)
    pos_s = PAST_LEN + np.arange(1)
    cq_p, sq_p = _rope_tables(pos_p, 1.0)
    ck_p, sk_p = _rope_tables(pos_p, k_scale)
    cq_s, sq_s = _rope_tables(pos_s, 1.0)
    ck_s, sk_s = _rope_tables(pos_s, k_scale)
    mask, q_dec, k_dec = _retention_tables(CHUNKS_PER_STEP * CHUNK)
    row_vec = lambda a: a.reshape(depth, 1, -1)
    fg = final_g.reshape(1, D_MODEL)

    h_p = x_prompt
    states_p = []
    for l in range(depth):
        gb = jnp.repeat(gm_b[l].T, HD, axis=1)
        h_p, s_p = _prompt_layer(
            h_p, norm_g[l].reshape(1, D_MODEL), win_b, wout_b, cq_p, sq_p, ck_p, sk_p,
            q_dec, k_dec, mask, gm_ws[l], gb, gm_ln_g[l].reshape(1, WIDTH), gm_ln_b[l].reshape(1, WIDTH), fg,
            layer=l, final_norm=(l == depth - 1))
        states_p.append(s_p)

    w0 = jnp.repeat(gm_ws[:, :, 0, 0], HD, axis=1).reshape(depth, 1, WIDTH)
    b0 = jnp.repeat(gm_b[:, :, 0], HD, axis=1).reshape(depth, 1, WIDTH)
    y_s, states_s, vn_s = _sample_layers(
        x_sample.reshape(n_rows, D_MODEL), state_ret, row_vec(norm_g), win_b, wout_b,
        cq_s, sq_s, ck_s, sk_s, w0, b0, row_vec(gm_ln_g), row_vec(gm_ln_b), fg)

    return (h_p, y_s.reshape(n_rows, 1, D_MODEL), jnp.stack(states_p), states_s,
            vn_s.reshape(depth, n_rows, 1, HEADS, HD))
```

```python
import functools

import jax
import jax.numpy as jnp
import numpy as np
from jax import lax
from jax.experimental import pallas as pl
from jax.experimental.pallas import tpu as pltpu

F32 = jnp.float32
BF16 = jnp.bfloat16

D_MODEL = 1024
HEADS = 8
HD = 128
LANES = 128
WIDTH = HEADS * HD
N_SEG = 7
MIX = 2 * WIDTH
CHUNK = 128
PAST_LEN = 16384
ROPE_BASE = 10000.0
EPS = 1e-6
SEG_Q, SEG_K, SEG_V, SEG_GR, SEG_U, SEG_VG, SEG_GG = range(N_SEG)

GAMMA = tuple(1.0 - 2.0 ** (-5.0 - h) for h in range(HEADS))
CHUNK_DECAY = tuple(g ** CHUNK for g in GAMMA)

CHUNKS_PER_STEP_READ = 4
CHUNKS_PER_STEP_UPDATE = 2
VMEM_LIMIT_BYTES = 56 * 1024 * 1024


def _silu(x):
    return x * (1.0 / (1.0 + jnp.exp(-x)))


def _rms_scale(x):
    return x * lax.rsqrt(jnp.mean(x * x, axis=-1, keepdims=True) + EPS)


def _layernorm(x, g, b):
    mu = jnp.mean(x, axis=-1, keepdims=True)
    xc = x - mu
    var = jnp.mean(xc * xc, axis=-1, keepdims=True)
    return xc * lax.rsqrt(var + EPS) * g + b


def _rope(x, cos, sin):
    return x * cos + pltpu.roll(x, HD // 2, 1) * sin


def _head(h):
    return slice(h * HD, (h + 1) * HD)


def _dot(a, b):
    return jnp.dot(a, b, preferred_element_type=F32)


def _const_spec(shape):
    zeros = (0,) * len(shape)
    return pl.BlockSpec(shape, lambda *_: zeros)


def _layer_spec(shape, layer, **kwargs):
    index = (layer,) + (0,) * len(shape)
    return pl.BlockSpec((None,) + shape, lambda *_: index, **kwargs)


def _layer_weight_spec(shape, layer):
    return _layer_spec(shape, layer, pipeline_mode=pl.Buffered(1))


def _chunk_rows(c):
    return slice(c * CHUNK, (c + 1) * CHUNK)


def _sample_state_step(col_refs, vrow_refs, state, snew_ref, opart_ref, rows):
    last = len(col_refs) - 1
    for l, col_ref in enumerate(col_refs):
        for j in range(rows):
            for h in range(HEADS):
                s_old = state(l, j, h)
                lane = h * 2 * rows + j
                if snew_ref is not None:
                    k_col = jnp.broadcast_to(col_ref[:, lane:lane + 1], (HD, HD))
                    v_row = vrow_refs[l][j:j + 1, _head(h)]
                    snew_ref[l, j, h] = s_old * GAMMA[h] + k_col * v_row
                if l == last:
                    q_col = jnp.broadcast_to(col_ref[:, lane + rows:lane + rows + 1], (HD, HD))
                    opart_ref[j:j + 1, _head(h)] = jnp.sum(q_col * s_old, axis=0, keepdims=True)


def _prompt_kernel(*refs, final_norm, n_stream, update):
    (x_ref, ng_ref, win_ref, wout_ref, cq_ref, sq_ref, ck_ref, sk_ref, qdec_ref, kdec_ref, mask_ref,
     ws_ref, gb_ref, lng_ref, lnb_ref, fg_ref) = refs[:16]
    refs = refs[16:]
    col_refs, refs = refs[:n_stream], refs[n_stream:]
    vrow_refs, refs = (refs[:n_stream], refs[n_stream:]) if update else ((), refs)
    s0_ref, y_ref, s_ref, opart_ref = refs[:4]
    snew_ref, refs = (refs[4], refs[5:]) if update else (None, refs[4:])
    xb, qdb, kdt, vb, gr, ug, vnb, om, wtril, sb, kvs, scb = refs

    t = pl.program_id(1)
    n_chunks = x_ref.shape[0] // CHUNK

    @pl.when(t == 0)
    def _start_of_row():
        s_ref[...] = jnp.zeros(s_ref.shape, F32)
        row = lax.broadcasted_iota(jnp.int32, (CHUNK, CHUNK), 0)
        col = lax.broadcasted_iota(jnp.int32, (CHUNK, CHUNK), 1)
        for h in range(HEADS):
            wtril[h] = jnp.where(row >= col, ws_ref[h], 0.0).astype(BF16)

    def proj(seg):
        return _dot(xb[...], win_ref[:, seg * WIDTH:(seg + 1) * WIDTH])

    xb[...] = (_rms_scale(x_ref[...]) * ng_ref[...]).astype(BF16)

    state = (lambda l, j, h: s0_ref[l, j, h]) if update else (lambda l, j, h: s0_ref[j, h])
    _sample_state_step(col_refs, vrow_refs, state, snew_ref, opart_ref, opart_ref.shape[0])

    acc = proj(SEG_Q)
    cos, sin = cq_ref[...], sq_ref[...]
    for h in range(HEADS):
        r = _rope(acc[:, _head(h)], cos, sin)
        qdb[:, _head(h)] = (r * qdec_ref[:, _head(h)]).astype(BF16)

    acc = proj(SEG_K)
    cos, sin = ck_ref[...], sk_ref[...]
    for h in range(HEADS):
        kd = _rope(acc[:, _head(h)], cos, sin) * kdec_ref[:, _head(h)]
        for c in range(n_chunks):
            kdt[c, h] = kd[_chunk_rows(c), :].T.astype(BF16)

    vb[...] = proj(SEG_V).astype(BF16)

    for c in range(n_chunks):
        rows = _chunk_rows(c)
        for h in range(HEADS):
            hs = _head(h)
            scb[c, h] = (_dot(qdb[rows, hs], kdt[c, h]) * mask_ref[h]).astype(BF16)
            kvs[c, h] = _dot(kdt[c, h], vb[rows, hs])

    gr[...] = _silu(proj(SEG_GR))
    vnb[...] = _layernorm(proj(SEG_VG), lng_ref[...], lnb_ref[...]).astype(BF16)

    for h in range(HEADS):
        s = s_ref[h]
        for c in range(n_chunks):
            sb[c, h] = s.astype(BF16)
            s = s * CHUNK_DECAY[h] + kvs[c, h]
        s_ref[h] = s

    for c in range(n_chunks):
        rows = _chunk_rows(c)
        for h in range(HEADS):
            hs = _head(h)
            lhs = jnp.concatenate([scb[c, h], qdb[rows, hs]], axis=1)
            rhs = jnp.concatenate([vb[rows, hs], sb[c, h]], axis=0)
            om[rows, hs] = (_rms_scale(_dot(lhs, rhs)) * gr[rows, hs]).astype(BF16)

    ug[...] = proj(SEG_U)
    ug[...] = ug[...] * _silu(proj(SEG_GG))

    for c in range(n_chunks):
        rows = _chunk_rows(c)
        for h in range(HEADS):
            hs = _head(h)
            s = _dot(wtril[h], vnb[rows, hs]) + gb_ref[:, hs]
            om[rows, WIDTH + h * HD:WIDTH + (h + 1) * HD] = (ug[rows, hs] * s).astype(BF16)

    y = x_ref[...] + _dot(om[...], wout_ref[...])
    if final_norm:
        y = _rms_scale(y) * fg_ref[...]
    y_ref[...] = y


def _prompt_layer(x, ng, win, wout, rope, qdec, kdec, mask, ws, gb, lng, lnb, fg, cols, vrows, states, *,
                  layer, chunks_per_step, final_norm, update):
    batch, seq, _ = x.shape
    depth, n_rows = states.shape[:2]
    tok = chunks_per_step * CHUNK
    steps_per_row = seq // tok
    rows = n_rows // (batch * steps_per_row)
    assert rows * batch * steps_per_row == n_rows and len(cols) == (depth if update else 1)
    group = lambda b, t: b * steps_per_row + t

    row_spec = pl.BlockSpec((None, tok, D_MODEL), lambda b, t: (b, t, 0))
    pos_spec = pl.BlockSpec((tok, HD), lambda b, t: (t, 0))
    col_spec = pl.BlockSpec((None, HD, LANES), lambda b, t: (group(b, t), 0, 0))
    vrow_spec = pl.BlockSpec((None, rows, WIDTH), lambda b, t: (group(b, t), 0, 0))
    if update:
        state_spec = pl.BlockSpec((depth, rows, HEADS, HD, HD), lambda b, t: (0, group(b, t), 0, 0, 0))
    else:
        state_spec = pl.BlockSpec((None, rows, HEADS, HD, HD), lambda b, t: (layer, group(b, t), 0, 0, 0))
    bf16_rows = lambda width: pltpu.VMEM((tok, width), BF16)
    per_chunk_head = lambda dtype: pltpu.VMEM((chunks_per_step, HEADS, CHUNK, CHUNK), dtype)

    out_specs = [row_spec, pl.BlockSpec((None, HEADS, HD, HD), lambda b, t: (b, 0, 0, 0)), vrow_spec]
    out_shape = [jax.ShapeDtypeStruct(x.shape, F32),
                 jax.ShapeDtypeStruct((batch, HEADS, HD, HD), F32),
                 jax.ShapeDtypeStruct((n_rows // rows, rows, WIDTH), F32)]
    if update:
        out_specs.append(state_spec)
        out_shape.append(jax.ShapeDtypeStruct(states.shape, F32))
    return pl.pallas_call(
        functools.partial(_prompt_kernel, final_norm=final_norm, n_stream=len(cols), update=update),
        grid=(batch, steps_per_row),
        in_specs=[
            row_spec,
            _const_spec((1, D_MODEL)),
            _layer_weight_spec((D_MODEL, N_SEG * WIDTH), layer),
            _layer_weight_spec((MIX, D_MODEL), layer),
            pos_spec, pos_spec, pos_spec, pos_spec,
            _const_spec((tok, WIDTH)), _const_spec((tok, WIDTH)),
            _const_spec((HEADS, CHUNK, CHUNK)),
            _const_spec((HEADS, CHUNK, CHUNK)),
            _const_spec((CHUNK, WIDTH)),
            _const_spec((1, WIDTH)), _const_spec((1, WIDTH)),
            _const_spec((1, D_MODEL)),
            *[col_spec] * len(cols),
            *[vrow_spec] * len(vrows),
            state_spec,
        ],
        out_specs=out_specs,
        out_shape=out_shape,
        scratch_shapes=[
            bf16_rows(D_MODEL),
            bf16_rows(WIDTH),
            per_chunk_head(BF16),
            bf16_rows(WIDTH),
            pltpu.VMEM((tok, WIDTH), F32),
            pltpu.VMEM((tok, WIDTH), F32),
            bf16_rows(WIDTH),
            bf16_rows(MIX),
            pltpu.VMEM((HEADS, CHUNK, CHUNK), BF16),
            per_chunk_head(BF16),
            per_chunk_head(F32),
            per_chunk_head(BF16),
        ],
        compiler_params=pltpu.CompilerParams(
            dimension_semantics=("arbitrary", "arbitrary"),
            vmem_limit_bytes=VMEM_LIMIT_BYTES,
        ),
        name="prompt_layer",
    )(x, ng, win, wout, *rope, qdec, kdec, mask, ws, gb, lng, lnb, fg, *cols, *vrows, states)


def _sample_kernel(*refs, post, pre, final):
    h_ref, refs = refs[0], refs[1:]
    if post:
        (scv_in, opart_ref, gr_in, m_in, wout_ref), refs = refs[:5], refs[5:]
    if pre:
        (ng_ref, win_ref, cq_ref, sq_ref, ck_ref, sk_ref, w0_ref, b0_ref, lng_ref, lnb_ref), refs = refs[:10], refs[10:]
    if final:
        fg_ref, refs = refs[0], refs[1:]
    if final:
        (y_ref,) = refs
    elif post:
        hout_ref, q_ref, k_ref, v_ref, scv_ref, gr_ref, m_ref, vn_ref, xb = refs
    else:
        q_ref, k_ref, v_ref, scv_ref, gr_ref, m_ref, vn_ref, xb = refs

    h = h_ref[...]
    if post:
        parts = []
        for hd in range(HEADS):
            hs = _head(hd)
            o = scv_in[:, hs] + GAMMA[hd] * opart_ref[:, hs]
            parts.append((_rms_scale(o) * gr_in[:, hs]).astype(BF16))
        om = jnp.concatenate(parts + [m_in[...]], axis=1)
        h = h + _dot(om, wout_ref[...])
    if final:
        y_ref[...] = _rms_scale(h) * fg_ref[...]
        return
    if post:
        hout_ref[...] = h

    def proj(seg):
        return _dot(xb[...], win_ref[:, seg * WIDTH:(seg + 1) * WIDTH])

    xb[...] = (_rms_scale(h) * ng_ref[...]).astype(BF16)
    acc = proj(SEG_Q)
    cos, sin = cq_ref[...], sq_ref[...]
    for hd in range(HEADS):
        q_ref[:, _head(hd)] = _rope(acc[:, _head(hd)], cos, sin)
    acc = proj(SEG_K)
    cos, sin = ck_ref[...], sk_ref[...]
    for hd in range(HEADS):
        hs = _head(hd)
        k = _rope(acc[:, hs], cos, sin)
        k_ref[:, hs] = k
        qk = jnp.sum(q_ref[:, hs] * k, axis=-1, keepdims=True)
        scv_ref[:, hs] = jnp.broadcast_to(qk, k.shape)
    v = proj(SEG_V)
    v_ref[...] = v
    scv_ref[...] = scv_ref[...] * v
    gr_ref[...] = _silu(proj(SEG_GR))
    ug = proj(SEG_U)
    ug = ug * _silu(proj(SEG_GG))
    vn = _layernorm(proj(SEG_VG), lng_ref[...], lnb_ref[...])
    vn_ref[...] = vn
    m_ref[...] = (ug * (vn * w0_ref[...] + b0_ref[...])).astype(BF16)


def _sample_stage(h, post_args, pre_args, fg, *, layer_post, layer_pre):
    n_rows = h.shape[0]
    post, pre = layer_post is not None, layer_pre is not None
    final = not pre
    full = _const_spec((n_rows, WIDTH))
    args, in_specs = [h], [full]
    if post:
        scv, opart, gr, m, wout = post_args
        args += [scv, opart, gr, m, wout]
        in_specs += [full, full, full, full, _layer_weight_spec((MIX, D_MODEL), layer_post)]
    if pre:
        ng, win, rope, w0, b0, lng, lnb = pre_args
        vec = _layer_spec((1, WIDTH), layer_pre)
        args += [ng, win, *rope, w0, b0, lng, lnb]
        in_specs += [vec, _layer_weight_spec((D_MODEL, N_SEG * WIDTH), layer_pre),
                     *[_const_spec((1, HD))] * 4, vec, vec, vec, vec]
    if final:
        args.append(fg)
        in_specs.append(_const_spec((1, D_MODEL)))
    f32_rows = jax.ShapeDtypeStruct((n_rows, WIDTH), F32)
    if final:
        out_shape, scratch = [f32_rows], []
    else:
        out_shape = ([f32_rows] if post else []) + [f32_rows] * 5 + [jax.ShapeDtypeStruct((n_rows, WIDTH), BF16), f32_rows]
        scratch = [pltpu.VMEM((n_rows, D_MODEL), BF16)]
    return pl.pallas_call(
        functools.partial(_sample_kernel, post=post, pre=pre, final=final),
        grid=(1,),
        in_specs=in_specs,
        out_specs=[full] * len(out_shape),
        out_shape=out_shape,
        scratch_shapes=scratch,
        compiler_params=pltpu.CompilerParams(
            dimension_semantics=("arbitrary",),
            vmem_limit_bytes=VMEM_LIMIT_BYTES,
        ),
        name="sample_stage",
    )(*args)


def _column_slabs(q, k, rows):
    groups = q.shape[0] // rows
    kq = jnp.stack([k.reshape(groups, rows, HEADS, HD), q.reshape(groups, rows, HEADS, HD)], axis=1)
    slab = kq.transpose(0, 4, 3, 1, 2).reshape(groups, HD, HEADS * 2 * rows)
    return jnp.pad(slab, ((0, 0), (0, 0), (0, LANES - HEADS * 2 * rows)))


def _rope_tables(pos, scale):
    inv = ROPE_BASE ** (-np.arange(0, HD, 2, dtype=np.float64) / HD)
    ang = np.asarray(pos, np.float64)[:, None] * inv[None, :]
    c, s = np.cos(ang) * scale, np.sin(ang) * scale
    return (jnp.asarray(np.concatenate([c, c], axis=-1), F32),
            jnp.asarray(np.concatenate([-s, s], axis=-1), F32))


def _retention_tables(rows):
    lg = np.log(np.asarray(GAMMA, np.float64))
    idx = np.arange(CHUNK, dtype=np.float64)
    causal = idx[:, None] >= idx[None, :]
    mask = np.where(causal[None], np.exp(-lg * CHUNK)[:, None, None], 0.0)
    q_dec = np.exp(lg[None, :] * (idx[:, None] + 1.0))
    k_dec = np.exp(lg[None, :] * (CHUNK - 1.0 - idx[:, None]))
    per_lane = lambda a: np.tile(np.repeat(a, HD, axis=1), (rows // CHUNK, 1))
    return jnp.asarray(mask, F32), jnp.asarray(per_lane(q_dec), F32), jnp.asarray(per_lane(k_dec), F32)


def kernel(x_prompt, x_sample, state_ret, norm_g, w_in, w_out, gm_ws, gm_b, gm_ln_g, gm_ln_b, final_g):
    depth = w_in.shape[0]
    batch, seq, _ = x_prompt.shape
    n_rows = x_sample.shape[0]
    assert x_sample.shape[1] == 1

    win_b = w_in.astype(BF16)
    wout_b = w_out.astype(BF16)
    k_scale = HD ** -0.5
    rope_p = (*_rope_tables(np.arange(seq), 1.0), *_rope_tables(np.arange(seq), k_scale))
    rope_s = (*_rope_tables(PAST_LEN + np.arange(1), 1.0), *_rope_tables(PAST_LEN + np.arange(1), k_scale))
    row_vec = lambda a: a.reshape(depth, 1, -1)
    fg = final_g.reshape(1, D_MODEL)
    w0 = jnp.repeat(gm_ws[:, :, 0, 0], HD, axis=1).reshape(depth, 1, WIDTH)
    b0 = jnp.repeat(gm_b[:, :, 0], HD, axis=1).reshape(depth, 1, WIDTH)
    pre_args = (row_vec(norm_g), win_b, rope_s, w0, b0, row_vec(gm_ln_g), row_vec(gm_ln_b))

    h_p = x_prompt
    h_s = x_sample.reshape(n_rows, D_MODEL)
    q, k, v, scv, gr, m, vn = _sample_stage(h_s, None, pre_args, fg, layer_post=None, layer_pre=0)
    states_p, vns, kv_history = [], [vn], []
    for l in range(depth):
        update = l == depth - 1
        chunks_per_step = CHUNKS_PER_STEP_UPDATE if update else CHUNKS_PER_STEP_READ
        tok = chunks_per_step * CHUNK
        rows = n_rows // (batch * (seq // tok))
        mask, q_dec, k_dec = _retention_tables(tok)
        kv_history.append((q, k, v))
        streamed = kv_history if update else kv_history[-1:]
        cols = [_column_slabs(qq, kk, rows) for qq, kk, _ in streamed]
        vrows = [vv.reshape(n_rows // rows, rows, WIDTH) for _, _, vv in streamed] if update else []
        gb = jnp.repeat(gm_b[l].T, HD, axis=1)
        outs = _prompt_layer(
            h_p, norm_g[l].reshape(1, D_MODEL), win_b, wout_b, rope_p, q_dec, k_dec, mask, gm_ws[l], gb,
            gm_ln_g[l].reshape(1, WIDTH), gm_ln_b[l].reshape(1, WIDTH), fg, cols, vrows, state_ret,
            layer=l, chunks_per_step=chunks_per_step, final_norm=update, update=update)
        h_p, s_p, opart = outs[:3]
        states_p.append(s_p)
        post_args = (scv, opart.reshape(n_rows, WIDTH), gr, m, wout_b)
        if update:
            states_s = outs[3]
            (y_s,) = _sample_stage(h_s, post_args, None, fg, layer_post=l, layer_pre=None)
        else:
            h_s, q, k, v, scv, gr, m, vn = _sample_stage(h_s, post_args, pre_args, fg, layer_post=l, layer_pre=l + 1)
            vns.append(vn)

    return (h_p, y_s.reshape(n_rows, 1, D_MODEL), jnp.stack(states_p), states_s,
            jnp.stack(vns).reshape(depth, n_rows, 1, HEADS, HD))
```

```python
import functools

import jax
import jax.numpy as jnp
import numpy as np
from jax import lax
from jax.experimental import pallas as pl
from jax.experimental.pallas import tpu as pltpu

F32 = jnp.float32
BF16 = jnp.bfloat16

D_MODEL = 1024
HEADS = 8
HD = 128
LANES, SUBLANES = 128, 8
WIDTH = HEADS * HD
N_SEG = 7
MIX = 2 * WIDTH
CHUNK = 128
PAST_LEN = 16384
ROPE_BASE = 10000.0
EPS = 1e-6
SEG_Q, SEG_K, SEG_V, SEG_GR, SEG_U, SEG_VG, SEG_GG = range(N_SEG)

GAMMA = tuple(1.0 - 2.0 ** (-5.0 - h) for h in range(HEADS))
CHUNK_DECAY = tuple(g ** CHUNK for g in GAMMA)

CHUNKS_PER_STEP_READ = 4
CHUNKS_PER_STEP_UPDATE = 2
VMEM_LIMIT_BYTES = 56 * 1024 * 1024


def _silu(x):
    return x * (1.0 / (1.0 + jnp.exp(-x)))


def _rms_scale(x):
    return x * lax.rsqrt(jnp.mean(x * x, axis=-1, keepdims=True) + EPS)


def _layernorm(x, g, b):
    mu = jnp.mean(x, axis=-1, keepdims=True)
    xc = x - mu
    var = jnp.mean(xc * xc, axis=-1, keepdims=True)
    return xc * lax.rsqrt(var + EPS) * g + b


def _rope(x, cos, sin):
    return x * cos + pltpu.roll(x, HD // 2, 1) * sin


def _head(h):
    return slice(h * HD, (h + 1) * HD)


def _dot(a, b):
    return jnp.dot(a, b, preferred_element_type=F32)


def _const_spec(shape):
    zeros = (0,) * len(shape)
    return pl.BlockSpec(shape, lambda *_: zeros)


def _layer_spec(shape, layer, **kwargs):
    index = (layer,) + (0,) * len(shape)
    return pl.BlockSpec((None,) + shape, lambda *_: index, **kwargs)


def _layer_weight_spec(shape, layer):
    return _layer_spec(shape, layer, pipeline_mode=pl.Buffered(1))


def _chunk_rows(c):
    return slice(c * CHUNK, (c + 1) * CHUNK)


def _sample_state_step(q_ref, kv_refs, gamma_ref, state, snew_ref, opart_ref, head0):
    heads = opart_ref.shape[1] // HD
    n_layers = max(len(kv_refs), 1)
    for l in range(n_layers):
        sources = ([kv_refs[l][0]] if kv_refs else []) + ([q_ref] if l == n_layers - 1 else [])
        tiles = [r[:, _head(hh)] for r in sources for hh in range(heads)]
        pad = jnp.zeros((LANES - len(tiles) * SUBLANES, HD), F32)
        cols = jnp.concatenate(tiles + [pad], axis=0).T
        q_base = heads * SUBLANES if kv_refs else 0
        for hh in range(heads):
            for j in range(SUBLANES):
                s_old = state(l, j, hh)
                lane = hh * SUBLANES + j
                if kv_refs:
                    k_col = jnp.broadcast_to(cols[:, lane:lane + 1], (HD, HD))
                    v_row = kv_refs[l][1][j:j + 1, _head(hh)]
                    snew_ref[l, j, hh] = s_old * gamma_ref[head0 + hh] + k_col * v_row
                if l == n_layers - 1:
                    q_col = jnp.broadcast_to(cols[:, q_base + lane:q_base + lane + 1], (HD, HD))
                    opart_ref[j:j + 1, _head(hh)] = jnp.sum(q_col * s_old, axis=0, keepdims=True)


def _prompt_kernel(*refs, final_norm, n_kv):
    (x_ref, ng_ref, win_ref, wout_ref, cq_ref, sq_ref, ck_ref, sk_ref, qdec_ref, kdec_ref, mask_ref,
     ws_ref, gb_ref, lng_ref, lnb_ref, fg_ref, gamma_ref, q_ref) = refs[:18]
    refs = refs[18:]
    kv_refs = [refs[2 * l:2 * l + 2] for l in range(n_kv)]
    refs = refs[2 * n_kv:]
    s0_ref, y_ref, s_ref, opart_ref = refs[:4]
    snew_ref, refs = (refs[4], refs[5:]) if n_kv else (None, refs[4:])
    xb, qdb, kdt, vb, gr, ug, vnb, om, wtril, sb, kvs, scb = refs

    t = pl.program_id(1)
    n_chunks = x_ref.shape[0] // CHUNK

    @pl.when(t == 0)
    def _start_of_row():
        s_ref[...] = jnp.zeros(s_ref.shape, F32)
        row = lax.broadcasted_iota(jnp.int32, (CHUNK, CHUNK), 0)
        col = lax.broadcasted_iota(jnp.int32, (CHUNK, CHUNK), 1)
        for h in range(HEADS):
            wtril[h] = jnp.where(row >= col, ws_ref[h], 0.0).astype(BF16)

    def proj(seg):
        return _dot(xb[...], win_ref[:, seg * WIDTH:(seg + 1) * WIDTH])

    xb[...] = (_rms_scale(x_ref[...]) * ng_ref[...]).astype(BF16)

    head_blocks = WIDTH // opart_ref.shape[1]
    head0 = lax.rem(pl.program_id(0) * pl.num_programs(1) + t, head_blocks) * (HEADS // head_blocks)
    state = (lambda l, j, h: s0_ref[l, j, h]) if n_kv else (lambda l, j, h: s0_ref[j, h])
    _sample_state_step(q_ref, kv_refs, gamma_ref, state, snew_ref, opart_ref, head0)

    acc = proj(SEG_Q)
    cos, sin = cq_ref[...], sq_ref[...]
    for h in range(HEADS):
        r = _rope(acc[:, _head(h)], cos, sin)
        qdb[:, _head(h)] = (r * qdec_ref[:, _head(h)]).astype(BF16)

    acc = proj(SEG_K)
    cos, sin = ck_ref[...], sk_ref[...]
    for h in range(HEADS):
        kd = _rope(acc[:, _head(h)], cos, sin) * kdec_ref[:, _head(h)]
        for c in range(n_chunks):
            kdt[c, h] = kd[_chunk_rows(c), :].T.astype(BF16)

    vb[...] = proj(SEG_V).astype(BF16)

    for c in range(n_chunks):
        rows = _chunk_rows(c)
        for h in range(HEADS):
            hs = _head(h)
            scb[c, h] = (_dot(qdb[rows, hs], kdt[c, h]) * mask_ref[h]).astype(BF16)
            kvs[c, h] = _dot(kdt[c, h], vb[rows, hs])

    gr[...] = _silu(proj(SEG_GR))
    vnb[...] = _layernorm(proj(SEG_VG), lng_ref[...], lnb_ref[...]).astype(BF16)

    for h in range(HEADS):
        s = s_ref[h]
        for c in range(n_chunks):
            sb[c, h] = s.astype(BF16)
            s = s * CHUNK_DECAY[h] + kvs[c, h]
        s_ref[h] = s

    for c in range(n_chunks):
        rows = _chunk_rows(c)
        for h in range(HEADS):
            hs = _head(h)
            lhs = jnp.concatenate([scb[c, h], qdb[rows, hs]], axis=1)
            rhs = jnp.concatenate([vb[rows, hs], sb[c, h]], axis=0)
            om[rows, hs] = (_rms_scale(_dot(lhs, rhs)) * gr[rows, hs]).astype(BF16)

    ug[...] = proj(SEG_U)
    ug[...] = ug[...] * _silu(proj(SEG_GG))

    for c in range(n_chunks):
        rows = _chunk_rows(c)
        for h in range(HEADS):
            hs = _head(h)
            s = _dot(wtril[h], vnb[rows, hs]) + gb_ref[:, hs]
            om[rows, WIDTH + h * HD:WIDTH + (h + 1) * HD] = (ug[rows, hs] * s).astype(BF16)

    y = x_ref[...] + _dot(om[...], wout_ref[...])
    if final_norm:
        y = _rms_scale(y) * fg_ref[...]
    y_ref[...] = y


def _prompt_layer(x, ng, win, wout, rope, qdec, kdec, mask, ws, gb, lng, lnb, fg, q, kvs, states, *,
                  layer, chunks_per_step, final_norm):
    batch, seq, _ = x.shape
    depth, n_rows = states.shape[:2]
    tok = chunks_per_step * CHUNK
    steps_per_row = seq // tok
    heads = n_rows * HEADS // (batch * steps_per_row * SUBLANES)
    head_blocks = HEADS // heads
    assert heads * head_blocks == HEADS and batch * steps_per_row == (n_rows // SUBLANES) * head_blocks
    assert len(kvs) in (0, depth)
    tile = lambda b, t: ((b * steps_per_row + t) // head_blocks, (b * steps_per_row + t) % head_blocks)

    row_spec = pl.BlockSpec((None, tok, D_MODEL), lambda b, t: (b, t, 0))
    pos_spec = pl.BlockSpec((tok, HD), lambda b, t: (t, 0))
    tile_spec = pl.BlockSpec((SUBLANES, heads * HD), tile)
    if kvs:
        state_spec = pl.BlockSpec((depth, SUBLANES, heads, HD, HD), lambda b, t: (0, *tile(b, t), 0, 0))
    else:
        state_spec = pl.BlockSpec((None, SUBLANES, heads, HD, HD), lambda b, t: (layer, *tile(b, t), 0, 0))
    gamma = jnp.asarray(np.broadcast_to(np.asarray(GAMMA)[:, None, None], (HEADS, 1, HD)), F32)
    bf16_rows = lambda width: pltpu.VMEM((tok, width), BF16)
    per_chunk_head = lambda dtype: pltpu.VMEM((chunks_per_step, HEADS, CHUNK, CHUNK), dtype)

    out_specs = [row_spec, pl.BlockSpec((None, HEADS, HD, HD), lambda b, t: (b, 0, 0, 0)), tile_spec]
    out_shape = [jax.ShapeDtypeStruct(x.shape, F32),
                 jax.ShapeDtypeStruct((batch, HEADS, HD, HD), F32),
                 jax.ShapeDtypeStruct((n_rows, WIDTH), F32)]
    if kvs:
        out_specs.append(state_spec)
        out_shape.append(jax.ShapeDtypeStruct(states.shape, F32))
    return pl.pallas_call(
        functools.partial(_prompt_kernel, final_norm=final_norm, n_kv=len(kvs)),
        grid=(batch, steps_per_row),
        in_specs=[
            row_spec,
            _const_spec((1, D_MODEL)),
            _layer_weight_spec((D_MODEL, N_SEG * WIDTH), layer),
            _layer_weight_spec((MIX, D_MODEL), layer),
            pos_spec, pos_spec, pos_spec, pos_spec,
            _const_spec((tok, WIDTH)), _const_spec((tok, WIDTH)),
            _const_spec((HEADS, CHUNK, CHUNK)),
            _const_spec((HEADS, CHUNK, CHUNK)),
            _const_spec((CHUNK, WIDTH)),
            _const_spec((1, WIDTH)), _const_spec((1, WIDTH)),
            _const_spec((1, D_MODEL)),
            _const_spec((HEADS, 1, HD)),
            tile_spec,
            *[tile_spec] * (2 * len(kvs)),
            state_spec,
        ],
        out_specs=out_specs,
        out_shape=out_shape,
        scratch_shapes=[
            bf16_rows(D_MODEL),
            bf16_rows(WIDTH),
            per_chunk_head(BF16),
            bf16_rows(WIDTH),
            pltpu.VMEM((tok, WIDTH), F32),
            pltpu.VMEM((tok, WIDTH), F32),
            bf16_rows(WIDTH),
            bf16_rows(MIX),
            pltpu.VMEM((HEADS, CHUNK, CHUNK), BF16),
            per_chunk_head(BF16),
            per_chunk_head(F32),
            per_chunk_head(BF16),
        ],
        compiler_params=pltpu.CompilerParams(
            dimension_semantics=("arbitrary", "arbitrary"),
            vmem_limit_bytes=VMEM_LIMIT_BYTES,
        ),
        name="prompt_layer",
    )(x, ng, win, wout, *rope, qdec, kdec, mask, ws, gb, lng, lnb, fg, gamma, q,
      *[a for kv in kvs for a in kv], states)


def _sample_kernel(*refs, post, pre, final):
    h_ref, refs = refs[0], refs[1:]
    if post:
        (scv_in, opart_ref, gr_in, m_in, wout_ref), refs = refs[:5], refs[5:]
    if pre:
        (ng_ref, win_ref, cq_ref, sq_ref, ck_ref, sk_ref, w0_ref, b0_ref, lng_ref, lnb_ref), refs = refs[:10], refs[10:]
    if final:
        fg_ref, refs = refs[0], refs[1:]
    if final:
        (y_ref,) = refs
    elif post:
        hout_ref, q_ref, k_ref, v_ref, scv_ref, gr_ref, m_ref, vn_ref, xb = refs
    else:
        q_ref, k_ref, v_ref, scv_ref, gr_ref, m_ref, vn_ref, xb = refs

    h = h_ref[...]
    if post:
        parts = []
        for hd in range(HEADS):
            hs = _head(hd)
            o = scv_in[:, hs] + GAMMA[hd] * opart_ref[:, hs]
            parts.append((_rms_scale(o) * gr_in[:, hs]).astype(BF16))
        om = jnp.concatenate(parts + [m_in[...]], axis=1)
        h = h + _dot(om, wout_ref[...])
    if final:
        y_ref[...] = _rms_scale(h) * fg_ref[...]
        return
    if post:
        hout_ref[...] = h

    def proj(seg):
        return _dot(xb[...], win_ref[:, seg * WIDTH:(seg + 1) * WIDTH])

    xb[...] = (_rms_scale(h) * ng_ref[...]).astype(BF16)
    acc = proj(SEG_Q)
    cos, sin = cq_ref[...], sq_ref[...]
    for hd in range(HEADS):
        q_ref[:, _head(hd)] = _rope(acc[:, _head(hd)], cos, sin)
    acc = proj(SEG_K)
    cos, sin = ck_ref[...], sk_ref[...]
    for hd in range(HEADS):
        hs = _head(hd)
        k = _rope(acc[:, hs], cos, sin)
        k_ref[:, hs] = k
        qk = jnp.sum(q_ref[:, hs] * k, axis=-1, keepdims=True)
        scv_ref[:, hs] = jnp.broadcast_to(qk, k.shape)
    v = proj(SEG_V)
    v_ref[...] = v
    scv_ref[...] = scv_ref[...] * v
    gr_ref[...] = _silu(proj(SEG_GR))
    ug = proj(SEG_U)
    ug = ug * _silu(proj(SEG_GG))
    vn = _layernorm(proj(SEG_VG), lng_ref[...], lnb_ref[...])
    vn_ref[...] = vn
    m_ref[...] = (ug * (vn * w0_ref[...] + b0_ref[...])).astype(BF16)


def _sample_stage(h, post_args, pre_args, fg, *, layer_post, layer_pre):
    n_rows = h.shape[0]
    post, pre = layer_post is not None, layer_pre is not None
    final = not pre
    full = _const_spec((n_rows, WIDTH))
    args, in_specs = [h], [full]
    if post:
        scv, opart, gr, m, wout = post_args
        args += [scv, opart, gr, m, wout]
        in_specs += [full, full, full, full, _layer_weight_spec((MIX, D_MODEL), layer_post)]
    if pre:
        ng, win, rope, w0, b0, lng, lnb = pre_args
        vec = _layer_spec((1, WIDTH), layer_pre)
        args += [ng, win, *rope, w0, b0, lng, lnb]
        in_specs += [vec, _layer_weight_spec((D_MODEL, N_SEG * WIDTH), layer_pre),
                     *[_const_spec((1, HD))] * 4, vec, vec, vec, vec]
    if final:
        args.append(fg)
        in_specs.append(_const_spec((1, D_MODEL)))
    f32_rows = jax.ShapeDtypeStruct((n_rows, WIDTH), F32)
    if final:
        out_shape, scratch = [f32_rows], []
    else:
        out_shape = ([f32_rows] if post else []) + [f32_rows] * 5 + [jax.ShapeDtypeStruct((n_rows, WIDTH), BF16), f32_rows]
        scratch = [pltpu.VMEM((n_rows, D_MODEL), BF16)]
    return pl.pallas_call(
        functools.partial(_sample_kernel, post=post, pre=pre, final=final),
        grid=(1,),
        in_specs=in_specs,
        out_specs=[full] * len(out_shape),
        out_shape=out_shape,
        scratch_shapes=scratch,
        compiler_params=pltpu.CompilerParams(
            dimension_semantics=("arbitrary",),
            vmem_limit_bytes=VMEM_LIMIT_BYTES,
        ),
        name="sample_stage",
    )(*args)


def _rope_tables(pos, scale):
    inv = ROPE_BASE ** (-np.arange(0, HD, 2, dtype=np.float64) / HD)
    ang = np.asarray(pos, np.float64)[:, None] * inv[None, :]
    c, s = np.cos(ang) * scale, np.sin(ang) * scale
    return (jnp.asarray(np.concatenate([c, c], axis=-1), F32),
            jnp.asarray(np.concatenate([-s, s], axis=-1), F32))


def _retention_tables(rows):
    lg = np.log(np.asarray(GAMMA, np.float64))
    idx = np.arange(CHUNK, dtype=np.float64)
    causal = idx[:, None] >= idx[None, :]
    mask = np.where(causal[None], np.exp(-lg * CHUNK)[:, None, None], 0.0)
    q_dec = np.exp(lg[None, :] * (idx[:, None] + 1.0))
    k_dec = np.exp(lg[None, :] * (CHUNK - 1.0 - idx[:, None]))
    per_lane = lambda a: np.tile(np.repeat(a, HD, axis=1), (rows // CHUNK, 1))
    return jnp.asarray(mask, F32), jnp.asarray(per_lane(q_dec), F32), jnp.asarray(per_lane(k_dec), F32)


def kernel(x_prompt, x_sample, state_ret, norm_g, w_in, w_out, gm_ws, gm_b, gm_ln_g, gm_ln_b, final_g):
    depth = w_in.shape[0]
    batch, seq, _ = x_prompt.shape
    n_rows = x_sample.shape[0]
    assert x_sample.shape[1] == 1

    win_b = w_in.astype(BF16)
    wout_b = w_out.astype(BF16)
    k_scale = HD ** -0.5
    rope_p = (*_rope_tables(np.arange(seq), 1.0), *_rope_tables(np.arange(seq), k_scale))
    rope_s = (*_rope_tables(PAST_LEN + np.arange(1), 1.0), *_rope_tables(PAST_LEN + np.arange(1), k_scale))
    row_vec = lambda a: a.reshape(depth, 1, -1)
    fg = final_g.reshape(1, D_MODEL)
    w0 = jnp.repeat(gm_ws[:, :, 0, 0], HD, axis=1).reshape(depth, 1, WIDTH)
    b0 = jnp.repeat(gm_b[:, :, 0], HD, axis=1).reshape(depth, 1, WIDTH)
    pre_args = (row_vec(norm_g), win_b, rope_s, w0, b0, row_vec(gm_ln_g), row_vec(gm_ln_b))

    h_p = x_prompt
    h_s = x_sample.reshape(n_rows, D_MODEL)
    q, k, v, scv, gr, m, vn = _sample_stage(h_s, None, pre_args, fg, layer_post=None, layer_pre=0)
    states_p, vns, kv_history = [], [vn], []
    for l in range(depth):
        update = l == depth - 1
        chunks_per_step = CHUNKS_PER_STEP_UPDATE if update else CHUNKS_PER_STEP_READ
        mask, q_dec, k_dec = _retention_tables(chunks_per_step * CHUNK)
        kv_history.append((k, v))
        gb = jnp.repeat(gm_b[l].T, HD, axis=1)
        outs = _prompt_layer(
            h_p, norm_g[l].reshape(1, D_MODEL), win_b, wout_b, rope_p, q_dec, k_dec, mask, gm_ws[l], gb,
            gm_ln_g[l].reshape(1, WIDTH), gm_ln_b[l].reshape(1, WIDTH), fg, q, kv_history if update else [],
            state_ret, layer=l, chunks_per_step=chunks_per_step, final_norm=update)
        h_p, s_p, opart = outs[:3]
        states_p.append(s_p)
        post_args = (scv, opart, gr, m, wout_b)
        if update:
            states_s = outs[3]
            (y_s,) = _sample_stage(h_s, post_args, None, fg, layer_post=l, layer_pre=None)
        else:
            h_s, q, k, v, scv, gr, m, vn = _sample_stage(h_s, post_args, pre_args, fg, layer_post=l, layer_pre=l + 1)
            vns.append(vn)

    return (h_p, y_s.reshape(n_rows, 1, D_MODEL), jnp.stack(states_p), states_s,
            jnp.stack(vns).reshape(depth, n_rows, 1, HEADS, HD))
```

```python
import functools

import jax
import jax.numpy as jnp
import numpy as np
from jax import lax
from jax.experimental import pallas as pl
from jax.experimental.pallas import tpu as pltpu

F32 = jnp.float32
BF16 = jnp.bfloat16

D_MODEL = 1024
HEADS = 8
HD = 128
LANES, SUBLANES = 128, 8
WIDTH = HEADS * HD
N_SEG = 7
MIX = 2 * WIDTH
CHUNK = 128
PAST_LEN = 16384
ROPE_BASE = 10000.0
EPS = 1e-6
SEG_Q, SEG_K, SEG_V, SEG_GR, SEG_U, SEG_VG, SEG_GG = range(N_SEG)

GAMMA = tuple(1.0 - 2.0 ** (-5.0 - h) for h in range(HEADS))
CHUNK_DECAY = tuple(g ** CHUNK for g in GAMMA)

CHUNKS_PER_STEP = 4
V7X_VMEM_BYTES = 64 * 1024 * 1024
VMEM_LIMIT_BYTES = V7X_VMEM_BYTES - 2 * 1024 * 1024


def _silu(x):
    return x * (1.0 / (1.0 + jnp.exp(-x)))


def _rms_scale(x):
    return x * lax.rsqrt(jnp.mean(x * x, axis=-1, keepdims=True) + EPS)


def _layernorm(x, g, b):
    mu = jnp.mean(x, axis=-1, keepdims=True)
    xc = x - mu
    var = jnp.mean(xc * xc, axis=-1, keepdims=True)
    return xc * lax.rsqrt(var + EPS) * g + b


def _rope(x, cos, sin):
    return x * cos + pltpu.roll(x, HD // 2, 1) * sin


def _head(h):
    return slice(h * HD, (h + 1) * HD)


def _dot(a, b):
    return jnp.dot(a, b, preferred_element_type=F32)


def _const_spec(shape):
    zeros = (0,) * len(shape)
    return pl.BlockSpec(shape, lambda *_: zeros)


def _layer_spec(shape, layer, **kwargs):
    index = (layer,) + (0,) * len(shape)
    return pl.BlockSpec((None,) + shape, lambda *_: index, **kwargs)


def _layer_weight_spec(shape, layer):
    return _layer_spec(shape, layer, pipeline_mode=pl.Buffered(1))


def _chunk_rows(c):
    return slice(c * CHUNK, (c + 1) * CHUNK)


def _sample_state_step(q_ref, k_ref, v_ref, gamma_ref, s0_ref, write_new, opart_ref, head0):
    heads = opart_ref.shape[1] // HD
    tiles = [r[:, _head(hh)] for r in (k_ref, q_ref) for hh in range(heads)]
    pad = jnp.zeros((LANES - len(tiles) * SUBLANES, HD), F32)
    cols = jnp.concatenate(tiles + [pad], axis=0).T
    for hh in range(heads):
        for j in range(SUBLANES):
            s_old = s0_ref[j, hh]
            k_lane = hh * SUBLANES + j
            q_lane = heads * SUBLANES + k_lane
            k_col = jnp.broadcast_to(cols[:, k_lane:k_lane + 1], (HD, HD))
            q_col = jnp.broadcast_to(cols[:, q_lane:q_lane + 1], (HD, HD))
            write_new(j, hh, s_old * gamma_ref[head0 + hh] + k_col * v_ref[j:j + 1, _head(hh)])
            opart_ref[j:j + 1, _head(hh)] = jnp.sum(q_col * s_old, axis=0, keepdims=True)


def _prompt_kernel(x_ref, ng_ref, win_ref, wout_ref, cq_ref, sq_ref, ck_ref, sk_ref, qdec_ref, kdec_ref, mask_ref,
                   ws_ref, gb_ref, lng_ref, lnb_ref, fg_ref, gamma_ref, q_ref, k_ref, v_ref, s0_ref, *refs,
                   layer, final_norm, aliased):
    y_ref, s_ref, opart_ref, snew_ref = refs[aliased:aliased + 4]
    xb, qdb, kdt, vb, gr, ug, vnb, om, wtril, sb, kvs, scb = refs[aliased + 4:]

    t = pl.program_id(1)
    n_chunks = x_ref.shape[0] // CHUNK

    @pl.when(t == 0)
    def _start_of_row():
        s_ref[...] = jnp.zeros(s_ref.shape, F32)
        row = lax.broadcasted_iota(jnp.int32, (CHUNK, CHUNK), 0)
        col = lax.broadcasted_iota(jnp.int32, (CHUNK, CHUNK), 1)
        for h in range(HEADS):
            wtril[h] = jnp.where(row >= col, ws_ref[h], 0.0).astype(BF16)

    def proj(seg, pair=None):
        lo, width = (seg * WIDTH, WIDTH) if pair is None else (seg * WIDTH + pair * 2 * HD, 2 * HD)
        return _dot(xb[...], win_ref[:, lo:lo + width])

    pairs = range(HEADS // 2)
    pair_cols = lambda pair: slice(pair * 2 * HD, (pair + 1) * 2 * HD)

    xb[...] = (_rms_scale(x_ref[...]) * ng_ref[...]).astype(BF16)

    head_blocks = WIDTH // opart_ref.shape[1]
    head0 = lax.rem(pl.program_id(0) * pl.num_programs(1) + t, head_blocks) * (HEADS // head_blocks)
    if aliased:
        def write_new(j, hh, s_new):
            snew_ref[j, hh] = s_new
    else:
        def write_new(j, hh, s_new):
            for l in range(snew_ref.shape[0]):
                snew_ref[l, j, hh] = s_new if l == layer else jnp.zeros_like(s_new)
    _sample_state_step(q_ref, k_ref, v_ref, gamma_ref, s0_ref, write_new, opart_ref, head0)

    cos, sin = cq_ref[...], sq_ref[...]
    for pair in pairs:
        acc = proj(SEG_Q, pair)
        for i in range(2):
            h = 2 * pair + i
            r = _rope(acc[:, _head(i)], cos, sin)
            for c in range(n_chunks):
                rows = _chunk_rows(c)
                qdb[rows, _head(h)] = (r[rows] * qdec_ref[:, _head(h)]).astype(BF16)

    cos, sin = ck_ref[...], sk_ref[...]
    for pair in pairs:
        acc = proj(SEG_K, pair)
        for i in range(2):
            h = 2 * pair + i
            r = _rope(acc[:, _head(i)], cos, sin)
            for c in range(n_chunks):
                kd = r[_chunk_rows(c)] * kdec_ref[:, _head(h)]
                kdt[c, h] = kd.T.astype(BF16)

    for pair in pairs:
        vb[:, pair_cols(pair)] = proj(SEG_V, pair).astype(BF16)

    for c in range(n_chunks):
        rows = _chunk_rows(c)
        for h in range(HEADS):
            hs = _head(h)
            scb[c, h] = (_dot(qdb[rows, hs], kdt[c, h]) * mask_ref[h]).astype(BF16)
            kvs[c, h] = _dot(kdt[c, h], vb[rows, hs])

    for pair in pairs:
        gr[:, pair_cols(pair)] = _silu(proj(SEG_GR, pair))
    vnb[...] = _layernorm(proj(SEG_VG), lng_ref[...], lnb_ref[...]).astype(BF16)

    for h in range(HEADS):
        s = s_ref[h]
        for c in range(n_chunks):
            sb[c, h] = s.astype(BF16)
            s = s * CHUNK_DECAY[h] + kvs[c, h]
        s_ref[h] = s

    for c in range(n_chunks):
        rows = _chunk_rows(c)
        for h in range(HEADS):
            hs = _head(h)
            lhs = jnp.concatenate([scb[c, h], qdb[rows, hs]], axis=1)
            rhs = jnp.concatenate([vb[rows, hs], sb[c, h]], axis=0)
            om[rows, hs] = (_rms_scale(_dot(lhs, rhs)) * gr[rows, hs]).astype(BF16)

    for pair in pairs:
        ug[:, pair_cols(pair)] = proj(SEG_U, pair) * _silu(proj(SEG_GG, pair))

    for c in range(n_chunks):
        rows = _chunk_rows(c)
        for h in range(HEADS):
            hs = _head(h)
            s = _dot(wtril[h], vnb[rows, hs]) + gb_ref[:, hs]
            om[rows, WIDTH + h * HD:WIDTH + (h + 1) * HD] = (ug[rows, hs] * s).astype(BF16)

    y = x_ref[...] + _dot(om[...], wout_ref[...])
    if final_norm:
        y = _rms_scale(y) * fg_ref[...]
    y_ref[...] = y


NEW_STATES_OPERAND, NEW_STATES_OUTPUT = 21, 3


def _prompt_layer(x, ng, win, wout, rope, qdec, kdec, mask, ws, gb, lng, lnb, fg, q, k, v, states, new_states, *,
                  layer, chunks_per_step, final_norm):
    batch, seq, _ = x.shape
    n_rows = states.shape[1]
    aliased = new_states is not None
    tok = chunks_per_step * CHUNK
    steps_per_row = seq // tok
    heads = n_rows * HEADS // (batch * steps_per_row * SUBLANES)
    head_blocks = HEADS // heads
    assert heads * head_blocks == HEADS and batch * steps_per_row == (n_rows // SUBLANES) * head_blocks
    tile = lambda b, t: ((b * steps_per_row + t) // head_blocks, (b * steps_per_row + t) % head_blocks)

    row_spec = pl.BlockSpec((None, tok, D_MODEL), lambda b, t: (b, t, 0))
    pos_spec = pl.BlockSpec((tok, HD), lambda b, t: (t, 0))
    tile_spec = pl.BlockSpec((SUBLANES, heads * HD), tile)
    state_spec = pl.BlockSpec((None, SUBLANES, heads, HD, HD), lambda b, t: (layer, *tile(b, t), 0, 0))
    all_layers_spec = pl.BlockSpec((states.shape[0], SUBLANES, heads, HD, HD), lambda b, t: (0, *tile(b, t), 0, 0))
    gamma = jnp.asarray(np.broadcast_to(np.asarray(GAMMA)[:, None, None], (HEADS, 1, HD)), F32)
    bf16_rows = lambda width: pltpu.VMEM((tok, width), BF16)
    per_chunk_head = lambda dtype: pltpu.VMEM((chunks_per_step, HEADS, CHUNK, CHUNK), dtype)

    return pl.pallas_call(
        functools.partial(_prompt_kernel, layer=layer, final_norm=final_norm, aliased=aliased),
        grid=(batch, steps_per_row),
        in_specs=[
            row_spec,
            _const_spec((1, D_MODEL)),
            _layer_weight_spec((D_MODEL, N_SEG * WIDTH), layer),
            _layer_weight_spec((MIX, D_MODEL), layer),
            pos_spec, pos_spec, pos_spec, pos_spec,
            _const_spec((CHUNK, WIDTH)), _const_spec((CHUNK, WIDTH)),
            _const_spec((HEADS, CHUNK, CHUNK)),
            _const_spec((HEADS, CHUNK, CHUNK)),
            _const_spec((CHUNK, WIDTH)),
            _const_spec((1, WIDTH)), _const_spec((1, WIDTH)),
            _const_spec((1, D_MODEL)),
            _const_spec((HEADS, 1, HD)),
            tile_spec, tile_spec, tile_spec,
            state_spec,
            *[pl.BlockSpec(memory_space=pl.ANY)] * aliased,
        ],
        out_specs=[row_spec, pl.BlockSpec((None, HEADS, HD, HD), lambda b, t: (b, 0, 0, 0)), tile_spec,
                   state_spec if aliased else all_layers_spec],
        out_shape=[jax.ShapeDtypeStruct(x.shape, F32),
                   jax.ShapeDtypeStruct((batch, HEADS, HD, HD), F32),
                   jax.ShapeDtypeStruct((n_rows, WIDTH), F32),
                   jax.ShapeDtypeStruct(states.shape, F32)],
        input_output_aliases={NEW_STATES_OPERAND: NEW_STATES_OUTPUT} if aliased else {},
        scratch_shapes=[
            bf16_rows(D_MODEL),
            bf16_rows(WIDTH),
            per_chunk_head(BF16),
            bf16_rows(WIDTH),
            pltpu.VMEM((tok, WIDTH), F32),
            pltpu.VMEM((tok, WIDTH), F32),
            bf16_rows(WIDTH),
            bf16_rows(MIX),
            pltpu.VMEM((HEADS, CHUNK, CHUNK), BF16),
            per_chunk_head(BF16),
            per_chunk_head(F32),
            per_chunk_head(BF16),
        ],
        compiler_params=pltpu.CompilerParams(
            dimension_semantics=("arbitrary", "arbitrary"),
            vmem_limit_bytes=VMEM_LIMIT_BYTES,
        ),
        name="prompt_layer",
    )(x, ng, win, wout, *rope, qdec, kdec, mask, ws, gb, lng, lnb, fg, gamma, q, k, v, states,
      *([new_states] if aliased else []))


def _sample_kernel(*refs, post, pre, final):
    h_ref, refs = refs[0], refs[1:]
    if post:
        (scv_in, opart_ref, gr_in, m_in, wout_ref), refs = refs[:5], refs[5:]
    if pre:
        (ng_ref, win_ref, cq_ref, sq_ref, ck_ref, sk_ref, w0_ref, b0_ref, lng_ref, lnb_ref), refs = refs[:10], refs[10:]
    if final:
        fg_ref, refs = refs[0], refs[1:]
    if final:
        (y_ref,) = refs
    elif post:
        hout_ref, q_ref, k_ref, v_ref, scv_ref, gr_ref, m_ref, vn_ref, xb = refs
    else:
        q_ref, k_ref, v_ref, scv_ref, gr_ref, m_ref, vn_ref, xb = refs

    h = h_ref[...]
    if post:
        parts = []
        for hd in range(HEADS):
            hs = _head(hd)
            o = scv_in[:, hs] + GAMMA[hd] * opart_ref[:, hs]
            parts.append((_rms_scale(o) * gr_in[:, hs]).astype(BF16))
        om = jnp.concatenate(parts + [m_in[...]], axis=1)
        h = h + _dot(om, wout_ref[...])
    if final:
        y_ref[...] = _rms_scale(h) * fg_ref[...]
        return
    if post:
        hout_ref[...] = h

    def proj(seg):
        return _dot(xb[...], win_ref[:, seg * WIDTH:(seg + 1) * WIDTH])

    xb[...] = (_rms_scale(h) * ng_ref[...]).astype(BF16)
    acc = proj(SEG_Q)
    cos, sin = cq_ref[...], sq_ref[...]
    for hd in range(HEADS):
        q_ref[:, _head(hd)] = _rope(acc[:, _head(hd)], cos, sin)
    acc = proj(SEG_K)
    cos, sin = ck_ref[...], sk_ref[...]
    for hd in range(HEADS):
        hs = _head(hd)
        k = _rope(acc[:, hs], cos, sin)
        k_ref[:, hs] = k
        qk = jnp.sum(q_ref[:, hs] * k, axis=-1, keepdims=True)
        scv_ref[:, hs] = jnp.broadcast_to(qk, k.shape)
    v = proj(SEG_V)
    v_ref[...] = v
    scv_ref[...] = scv_ref[...] * v
    gr_ref[...] = _silu(proj(SEG_GR))
    ug = proj(SEG_U)
    ug = ug * _silu(proj(SEG_GG))
    vn = _layernorm(proj(SEG_VG), lng_ref[...], lnb_ref[...])
    vn_ref[...] = vn
    m_ref[...] = (ug * (vn * w0_ref[...] + b0_ref[...])).astype(BF16)


def _sample_stage(h, post_args, pre_args, fg, *, layer_post, layer_pre):
    n_rows = h.shape[0]
    post, pre = layer_post is not None, layer_pre is not None
    final = not pre
    full = _const_spec((n_rows, WIDTH))
    args, in_specs = [h], [full]
    if post:
        scv, opart, gr, m, wout = post_args
        args += [scv, opart, gr, m, wout]
        in_specs += [full, full, full, full, _layer_weight_spec((MIX, D_MODEL), layer_post)]
    if pre:
        ng, win, rope, w0, b0, lng, lnb = pre_args
        vec = _layer_spec((1, WIDTH), layer_pre)
        args += [ng, win, *rope, w0, b0, lng, lnb]
        in_specs += [vec, _layer_weight_spec((D_MODEL, N_SEG * WIDTH), layer_pre),
                     *[_const_spec((1, HD))] * 4, vec, vec, vec, vec]
    if final:
        args.append(fg)
        in_specs.append(_const_spec((1, D_MODEL)))
    f32_rows = jax.ShapeDtypeStruct((n_rows, WIDTH), F32)
    if final:
        out_shape, scratch = [f32_rows], []
    else:
        out_shape = ([f32_rows] if post else []) + [f32_rows] * 5 + [jax.ShapeDtypeStruct((n_rows, WIDTH), BF16), f32_rows]
        scratch = [pltpu.VMEM((n_rows, D_MODEL), BF16)]
    return pl.pallas_call(
        functools.partial(_sample_kernel, post=post, pre=pre, final=final),
        grid=(1,),
        in_specs=in_specs,
        out_specs=[full] * len(out_shape),
        out_shape=out_shape,
        scratch_shapes=scratch,
        compiler_params=pltpu.CompilerParams(
            dimension_semantics=("arbitrary",),
            vmem_limit_bytes=VMEM_LIMIT_BYTES,
        ),
        name="sample_stage",
    )(*args)


def _rope_tables(pos, scale):
    inv = ROPE_BASE ** (-np.arange(0, HD, 2, dtype=np.float64) / HD)
    ang = np.asarray(pos, np.float64)[:, None] * inv[None, :]
    c, s = np.cos(ang) * scale, np.sin(ang) * scale
    return (jnp.asarray(np.concatenate([c, c], axis=-1), F32),
            jnp.asarray(np.concatenate([-s, s], axis=-1), F32))


def _retention_tables():
    lg = np.log(np.asarray(GAMMA, np.float64))
    idx = np.arange(CHUNK, dtype=np.float64)
    causal = idx[:, None] >= idx[None, :]
    mask = np.where(causal[None], np.exp(-lg * CHUNK)[:, None, None], 0.0)
    q_dec = np.exp(lg[None, :] * (idx[:, None] + 1.0))
    k_dec = np.exp(lg[None, :] * (CHUNK - 1.0 - idx[:, None]))
    per_lane = lambda a: np.repeat(a, HD, axis=1)
    return jnp.asarray(mask, F32), jnp.asarray(per_lane(q_dec), F32), jnp.asarray(per_lane(k_dec), F32)


def kernel(x_prompt, x_sample, state_ret, norm_g, w_in, w_out, gm_ws, gm_b, gm_ln_g, gm_ln_b, final_g):
    depth = w_in.shape[0]
    batch, seq, _ = x_prompt.shape
    n_rows = x_sample.shape[0]
    assert x_sample.shape[1] == 1

    win_b = w_in.astype(BF16)
    wout_b = w_out.astype(BF16)
    k_scale = HD ** -0.5
    rope_p = (*_rope_tables(np.arange(seq), 1.0), *_rope_tables(np.arange(seq), k_scale))
    rope_s = (*_rope_tables(PAST_LEN + np.arange(1), 1.0), *_rope_tables(PAST_LEN + np.arange(1), k_scale))
    row_vec = lambda a: a.reshape(depth, 1, -1)
    fg = final_g.reshape(1, D_MODEL)
    w0 = jnp.repeat(gm_ws[:, :, 0, 0], HD, axis=1).reshape(depth, 1, WIDTH)
    b0 = jnp.repeat(gm_b[:, :, 0], HD, axis=1).reshape(depth, 1, WIDTH)
    pre_args = (row_vec(norm_g), win_b, rope_s, w0, b0, row_vec(gm_ln_g), row_vec(gm_ln_b))

    h_p = x_prompt
    h_s = x_sample.reshape(n_rows, D_MODEL)
    q, k, v, scv, gr, m, vn = _sample_stage(h_s, None, pre_args, fg, layer_post=None, layer_pre=0)
    mask, q_dec, k_dec = _retention_tables()
    states_p, vns, states_s = [], [vn], None
    for l in range(depth):
        last = l == depth - 1
        gb = jnp.repeat(gm_b[l].T, HD, axis=1)
        h_p, s_p, opart, states_s = _prompt_layer(
            h_p, norm_g[l].reshape(1, D_MODEL), win_b, wout_b, rope_p, q_dec, k_dec, mask, gm_ws[l], gb,
            gm_ln_g[l].reshape(1, WIDTH), gm_ln_b[l].reshape(1, WIDTH), fg, q, k, v, state_ret, states_s,
            layer=l, chunks_per_step=CHUNKS_PER_STEP, final_norm=last)
        states_p.append(s_p)
        post_args = (scv, opart, gr, m, wout_b)
        if last:
            (y_s,) = _sample_stage(h_s, post_args, None, fg, layer_post=l, layer_pre=None)
        else:
            h_s, q, k, v, scv, gr, m, vn = _sample_stage(h_s, post_args, pre_args, fg, layer_post=l, layer_pre=l + 1)
            vns.append(vn)

    return (h_p, y_s.reshape(n_rows, 1, D_MODEL), jnp.stack(states_p), states_s,
            jnp.stack(vns).reshape(depth, n_rows, 1, HEADS, HD))
```

```python
import functools

import jax
import jax.numpy as jnp
import numpy as np
from jax import lax
from jax.experimental import pallas as pl
from jax.experimental.pallas import tpu as pltpu

F32 = jnp.float32
BF16 = jnp.bfloat16

D_MODEL = 1024
HEADS = 8
HD = 128
LANES, SUBLANES = 128, 8
WIDTH = HEADS * HD
N_SEG = 7
MIX = 2 * WIDTH
CHUNK = 128
PAST_LEN = 16384
ROPE_BASE = 10000.0
EPS = 1e-6
SEG_Q, SEG_K, SEG_V, SEG_GR, SEG_U, SEG_VG, SEG_GG = range(N_SEG)

GAMMA = tuple(1.0 - 2.0 ** (-5.0 - h) for h in range(HEADS))
CHUNK_DECAY = tuple(g ** CHUNK for g in GAMMA)
K_SCALE = HD ** -0.5

CHUNKS_PER_STEP = 4
V7X_VMEM_BYTES = 64 * 1024 * 1024
VMEM_LIMIT_BYTES = V7X_VMEM_BYTES - 2 * 1024 * 1024


def _silu(x):
    return x * (1.0 / (1.0 + jnp.exp(-x)))


def _rms_scale(x):
    return x * lax.rsqrt(jnp.mean(x * x, axis=-1, keepdims=True) + EPS)


def _layernorm(x, g, b):
    mu = jnp.mean(x, axis=-1, keepdims=True)
    xc = x - mu
    var = jnp.mean(xc * xc, axis=-1, keepdims=True)
    return xc * lax.rsqrt(var + EPS) * g + b


def _rope(x, cos, sin):
    return x * cos + pltpu.roll(x, HD // 2, 1) * sin


def _head(h):
    return slice(h * HD, (h + 1) * HD)


def _dot(a, b):
    return jnp.dot(a, b, preferred_element_type=F32)


def _const_spec(shape):
    zeros = (0,) * len(shape)
    return pl.BlockSpec(shape, lambda *_: zeros)


def _layer_spec(shape, layer, **kwargs):
    index = (layer,) + (0,) * len(shape)
    return pl.BlockSpec((None,) + shape, lambda *_: index, **kwargs)


def _layer_weight_spec(shape, layer):
    return _layer_spec(shape, layer, pipeline_mode=pl.Buffered(1))


def _chunk_rows(c):
    return slice(c * CHUNK, (c + 1) * CHUNK)


def _sample_state_step(q_ref, k_ref, v_ref, gamma_ref, s0_ref, write_new, opart_ref, head0):
    heads = opart_ref.shape[1] // HD
    tiles = [r[:, _head(hh)] for r in (k_ref, q_ref) for hh in range(heads)]
    pad = jnp.zeros((LANES - len(tiles) * SUBLANES, HD), F32)
    cols = jnp.concatenate(tiles + [pad], axis=0).T
    for hh in range(heads):
        for j in range(SUBLANES):
            s_old = s0_ref[j, hh]
            k_lane = hh * SUBLANES + j
            q_lane = heads * SUBLANES + k_lane
            k_col = jnp.broadcast_to(cols[:, k_lane:k_lane + 1], (HD, HD))
            q_col = jnp.broadcast_to(cols[:, q_lane:q_lane + 1], (HD, HD))
            write_new(j, hh, s_old * gamma_ref[head0 + hh] + k_col * v_ref[j:j + 1, _head(hh)])
            opart_ref[j:j + 1, _head(hh)] = jnp.sum(q_col * s_old, axis=0, keepdims=True)


def _prompt_kernel(x_ref, ng_ref, win_ref, wout_ref, cos_ref, sin_ref, qdec_ref, kdec_ref, mask_ref,
                   ws_ref, gb_ref, lng_ref, lnb_ref, fg_ref, gamma_ref, q_ref, k_ref, v_ref, s0_ref, *refs,
                   layer, final_norm, aliased):
    refs = refs[N_SHARED * aliased:]
    y_ref, sall_ref, opart_ref, snew_ref = refs[:4]
    xb, qdb, kdt, vb, gr, ug, vnb, om, wtril, sb, kvs, scb = refs[4:]
    s_ref = sall_ref if aliased else sall_ref.at[layer]
    ng, lng, lnb = (r[layer:layer + 1, :] for r in (ng_ref, lng_ref, lnb_ref))

    t = pl.program_id(1)
    n_chunks = x_ref.shape[0] // CHUNK

    @pl.when(t == 0)
    def _start_of_row():
        sall_ref[...] = jnp.zeros(sall_ref.shape, F32)
        row = lax.broadcasted_iota(jnp.int32, (CHUNK, CHUNK), 0)
        col = lax.broadcasted_iota(jnp.int32, (CHUNK, CHUNK), 1)
        for h in range(HEADS):
            wtril[h] = jnp.where(row >= col, ws_ref[h], 0.0).astype(BF16)

    def proj(seg, pair=None):
        lo, width = (seg * WIDTH, WIDTH) if pair is None else (seg * WIDTH + pair * 2 * HD, 2 * HD)
        return _dot(xb[...], win_ref[:, lo:lo + width])

    pairs = range(HEADS // 2)
    pair_cols = lambda pair: slice(pair * 2 * HD, (pair + 1) * 2 * HD)

    xb[...] = (_rms_scale(x_ref[...]) * ng).astype(BF16)

    head_blocks = WIDTH // opart_ref.shape[1]
    head0 = lax.rem(pl.program_id(0) * pl.num_programs(1) + t, head_blocks) * (HEADS // head_blocks)
    if aliased:
        def write_new(j, hh, s_new):
            snew_ref[j, hh] = s_new
    else:
        def write_new(j, hh, s_new):
            for l in range(snew_ref.shape[0]):
                snew_ref[l, j, hh] = s_new if l == layer else jnp.zeros_like(s_new)
    _sample_state_step(q_ref, k_ref, v_ref, gamma_ref, s0_ref, write_new, opart_ref, head0)

    cos, sin = cos_ref[...], sin_ref[...]
    for pair in pairs:
        acc = proj(SEG_Q, pair)
        for i in range(2):
            h = 2 * pair + i
            r = _rope(acc[:, _head(i)], cos, sin)
            for c in range(n_chunks):
                rows = _chunk_rows(c)
                qdb[rows, _head(h)] = (r[rows] * qdec_ref[:, _head(h)]).astype(BF16)

    for pair in pairs:
        acc = proj(SEG_K, pair)
        for i in range(2):
            h = 2 * pair + i
            r = _rope(acc[:, _head(i)], cos, sin)
            for c in range(n_chunks):
                kd = r[_chunk_rows(c)] * kdec_ref[:, _head(h)]
                kdt[c, h] = kd.T.astype(BF16)

    for pair in pairs:
        vb[:, pair_cols(pair)] = proj(SEG_V, pair).astype(BF16)

    for c in range(n_chunks):
        rows = _chunk_rows(c)
        for h in range(HEADS):
            hs = _head(h)
            scb[c, h] = (_dot(qdb[rows, hs], kdt[c, h]) * mask_ref[h]).astype(BF16)
            kvs[c, h] = _dot(kdt[c, h], vb[rows, hs])

    for pair in pairs:
        gr[:, pair_cols(pair)] = _silu(proj(SEG_GR, pair))
    vnb[...] = _layernorm(proj(SEG_VG), lng, lnb).astype(BF16)

    for h in range(HEADS):
        s = s_ref[h]
        for c in range(n_chunks):
            sb[c, h] = s.astype(BF16)
            s = s * CHUNK_DECAY[h] + kvs[c, h]
        s_ref[h] = s

    for c in range(n_chunks):
        rows = _chunk_rows(c)
        for h in range(HEADS):
            hs = _head(h)
            lhs = jnp.concatenate([scb[c, h], qdb[rows, hs]], axis=1)
            rhs = jnp.concatenate([vb[rows, hs], sb[c, h]], axis=0)
            om[rows, hs] = (_rms_scale(_dot(lhs, rhs)) * gr[rows, hs]).astype(BF16)

    for pair in pairs:
        ug[:, pair_cols(pair)] = proj(SEG_U, pair) * _silu(proj(SEG_GG, pair))

    for c in range(n_chunks):
        rows = _chunk_rows(c)
        for h in range(HEADS):
            hs = _head(h)
            s = _dot(wtril[h], vnb[rows, hs]) + gb_ref[:, hs]
            om[rows, WIDTH + h * HD:WIDTH + (h + 1) * HD] = (ug[rows, hs] * s).astype(BF16)

    y = x_ref[...] + _dot(om[...], wout_ref[...])
    if final_norm:
        y = _rms_scale(y) * fg_ref[...]
    y_ref[...] = y


N_PLAIN_OPERANDS = 19
N_SHARED = 2
OUT_PROMPT_STATES, OUT_SAMPLE_STATES = 1, 3


def _prompt_layer(x, ng, win, wout, rope, qdec, kdec, mask, ws, gb, lng, lnb, fg, q, k, v, states, shared, *,
                  layer, chunks_per_step, final_norm):
    batch, seq, _ = x.shape
    depth, n_rows = states.shape[:2]
    aliased = shared is not None
    tok = chunks_per_step * CHUNK
    steps_per_row = seq // tok
    heads = n_rows * HEADS // (batch * steps_per_row * SUBLANES)
    head_blocks = HEADS // heads
    assert heads * head_blocks == HEADS and batch * steps_per_row == (n_rows // SUBLANES) * head_blocks
    tile = lambda b, t: ((b * steps_per_row + t) // head_blocks, (b * steps_per_row + t) % head_blocks)

    row_spec = pl.BlockSpec((None, tok, D_MODEL), lambda b, t: (b, t, 0))
    pos_spec = pl.BlockSpec((tok, HD), lambda b, t: (t, 0))
    tile_spec = pl.BlockSpec((SUBLANES, heads * HD), tile)
    state_spec = pl.BlockSpec((None, SUBLANES, heads, HD, HD), lambda b, t: (layer, *tile(b, t), 0, 0))
    all_layers_spec = pl.BlockSpec((depth, SUBLANES, heads, HD, HD), lambda b, t: (0, *tile(b, t), 0, 0))
    if aliased:
        pstate_spec = pl.BlockSpec((None, None, HEADS, HD, HD), lambda b, t: (layer, b, 0, 0, 0))
    else:
        pstate_spec = pl.BlockSpec((depth, None, HEADS, HD, HD), lambda b, t: (0, b, 0, 0, 0))
    gamma = jnp.asarray(np.broadcast_to(np.asarray(GAMMA)[:, None, None], (HEADS, 1, HD)), F32)
    bf16_rows = lambda width: pltpu.VMEM((tok, width), BF16)
    per_chunk_head = lambda dtype: pltpu.VMEM((chunks_per_step, HEADS, CHUNK, CHUNK), dtype)

    return pl.pallas_call(
        functools.partial(_prompt_kernel, layer=layer, final_norm=final_norm, aliased=aliased),
        grid=(batch, steps_per_row),
        in_specs=[
            row_spec,
            _const_spec((depth, D_MODEL)),
            _layer_weight_spec((D_MODEL, N_SEG * WIDTH), layer),
            _layer_weight_spec((MIX, D_MODEL), layer),
            pos_spec, pos_spec,
            _const_spec((CHUNK, WIDTH)), _const_spec((CHUNK, WIDTH)),
            _const_spec((HEADS, CHUNK, CHUNK)),
            _layer_spec((HEADS, CHUNK, CHUNK), layer),
            _const_spec((CHUNK, WIDTH)),
            _const_spec((depth, WIDTH)), _const_spec((depth, WIDTH)),
            _const_spec((1, D_MODEL)),
            _const_spec((HEADS, 1, HD)),
            tile_spec, tile_spec, tile_spec,
            state_spec,
            *[pl.BlockSpec(memory_space=pl.ANY)] * (N_SHARED * aliased),
        ],
        out_specs=[row_spec, pstate_spec, tile_spec, state_spec if aliased else all_layers_spec],
        out_shape=[jax.ShapeDtypeStruct(x.shape, F32),
                   jax.ShapeDtypeStruct((depth, batch, HEADS, HD, HD), F32),
                   jax.ShapeDtypeStruct((n_rows, WIDTH), F32),
                   jax.ShapeDtypeStruct(states.shape, F32)],
        input_output_aliases=({N_PLAIN_OPERANDS: OUT_PROMPT_STATES, N_PLAIN_OPERANDS + 1: OUT_SAMPLE_STATES}
                              if aliased else {}),
        scratch_shapes=[
            bf16_rows(D_MODEL),
            bf16_rows(WIDTH),
            per_chunk_head(BF16),
            bf16_rows(WIDTH),
            pltpu.VMEM((tok, WIDTH), F32),
            pltpu.VMEM((tok, WIDTH), F32),
            bf16_rows(WIDTH),
            bf16_rows(MIX),
            pltpu.VMEM((HEADS, CHUNK, CHUNK), BF16),
            per_chunk_head(BF16),
            per_chunk_head(F32),
            per_chunk_head(BF16),
        ],
        compiler_params=pltpu.CompilerParams(
            dimension_semantics=("arbitrary", "arbitrary"),
            vmem_limit_bytes=VMEM_LIMIT_BYTES,
        ),
        name="prompt_layer",
    )(x, ng, win, wout, *rope, qdec, kdec, mask, ws, gb, lng, lnb, fg, gamma, q, k, v, states,
      *(shared if aliased else ()))


def _sample_kernel(*refs, post, pre, final, layer_pre):
    h_ref, refs = refs[0], refs[1:]
    if post:
        (scv_in, opart_ref, gr_in, m_in, wout_ref), refs = refs[:5], refs[5:]
    if pre:
        (ng_ref, win_ref, cos_ref, sin_ref, ws_ref, gmb_ref, lng_ref, lnb_ref), refs = refs[:8], refs[8:]
    if final:
        fg_ref, refs = refs[0], refs[1:]
    if final:
        (y_ref,) = refs
    elif post:
        hout_ref, q_ref, k_ref, v_ref, scv_ref, gr_ref, m_ref, vn_ref, xb = refs
    else:
        q_ref, k_ref, v_ref, scv_ref, gr_ref, m_ref, vn_ref, xb = refs

    h = h_ref[...]
    if post:
        parts = []
        for hd in range(HEADS):
            hs = _head(hd)
            o = scv_in[:, hs] + GAMMA[hd] * opart_ref[:, hs]
            parts.append((_rms_scale(o) * gr_in[:, hs]).astype(BF16))
        om = jnp.concatenate(parts + [m_in[...]], axis=1)
        h = h + _dot(om, wout_ref[...])
    if final:
        y_ref[...] = _rms_scale(h) * fg_ref[...]
        return
    if post:
        hout_ref[...] = h

    def proj(seg):
        return _dot(xb[...], win_ref[:, seg * WIDTH:(seg + 1) * WIDTH])

    ng, lng, lnb = (r[layer_pre:layer_pre + 1, :] for r in (ng_ref, lng_ref, lnb_ref))
    xb[...] = (_rms_scale(h) * ng).astype(BF16)
    acc = proj(SEG_Q)
    cos, sin = cos_ref[...], sin_ref[...]
    for hd in range(HEADS):
        q_ref[:, _head(hd)] = _rope(acc[:, _head(hd)], cos, sin)
    acc = proj(SEG_K)
    for hd in range(HEADS):
        hs = _head(hd)
        k = _rope(acc[:, hs], cos, sin) * K_SCALE
        k_ref[:, hs] = k
        qk = jnp.sum(q_ref[:, hs] * k, axis=-1, keepdims=True)
        scv_ref[:, hs] = jnp.broadcast_to(qk, k.shape)
    v = proj(SEG_V)
    v_ref[...] = v
    scv_ref[...] = scv_ref[...] * v
    gr_ref[...] = _silu(proj(SEG_GR))
    ug = proj(SEG_U)
    ug = ug * _silu(proj(SEG_GG))
    vn = _layernorm(proj(SEG_VG), lng, lnb)
    vn_ref[...] = vn
    for hd in range(HEADS):
        hs = _head(hd)
        s = vn[:, hs] * ws_ref[hd, 0:1, 0:1] + gmb_ref[layer_pre, hd:hd + 1, 0:1]
        m_ref[:, hs] = (ug[:, hs] * s).astype(BF16)


def _sample_stage(h, post_args, pre_args, fg, *, layer_post, layer_pre):
    n_rows = h.shape[0]
    post, pre = layer_post is not None, layer_pre is not None
    final = not pre
    full = _const_spec((n_rows, WIDTH))
    token_spec = pl.BlockSpec((n_rows, None, D_MODEL), lambda *_: (0, 0, 0))
    args, in_specs = [h], [token_spec if h.ndim == 3 else full]
    if post:
        scv, opart, gr, m, wout = post_args
        args += [scv, opart, gr, m, wout]
        in_specs += [full, full, full, full, _layer_weight_spec((MIX, D_MODEL), layer_post)]
    if pre:
        ng, win, rope, ws, gmb, lng, lnb = pre_args
        depth = win.shape[0]
        vecs = _const_spec((depth, WIDTH))
        args += [ng, win, *rope, ws, gmb, lng, lnb]
        in_specs += [vecs, _layer_weight_spec((D_MODEL, N_SEG * WIDTH), layer_pre),
                     _const_spec((1, HD)), _const_spec((1, HD)),
                     _layer_spec((HEADS, CHUNK, CHUNK), layer_pre), _const_spec((depth, HEADS, CHUNK)), vecs, vecs]
    if final:
        args.append(fg)
        in_specs.append(_const_spec((1, D_MODEL)))
    f32_rows = jax.ShapeDtypeStruct((n_rows, WIDTH), F32)
    if final:
        out_shape, scratch = [jax.ShapeDtypeStruct((n_rows, 1, D_MODEL), F32)], []
    else:
        out_shape = ([f32_rows] if post else []) + [f32_rows] * 5 + [jax.ShapeDtypeStruct((n_rows, WIDTH), BF16), f32_rows]
        scratch = [pltpu.VMEM((n_rows, D_MODEL), BF16)]
    return pl.pallas_call(
        functools.partial(_sample_kernel, post=post, pre=pre, final=final, layer_pre=layer_pre),
        grid=(1,),
        in_specs=in_specs,
        out_specs=[token_spec] if final else [full] * len(out_shape),
        out_shape=out_shape,
        scratch_shapes=scratch,
        compiler_params=pltpu.CompilerParams(
            dimension_semantics=("arbitrary",),
            vmem_limit_bytes=VMEM_LIMIT_BYTES,
        ),
        name="sample_stage",
    )(*args)


def _rope_tables(pos):
    inv = ROPE_BASE ** (-np.arange(0, HD, 2, dtype=np.float64) / HD)
    ang = np.asarray(pos, np.float64)[:, None] * inv[None, :]
    c, s = np.cos(ang), np.sin(ang)
    return (jnp.asarray(np.concatenate([c, c], axis=-1), F32),
            jnp.asarray(np.concatenate([-s, s], axis=-1), F32))


def _retention_tables():
    lg = np.log(np.asarray(GAMMA, np.float64))
    idx = np.arange(CHUNK, dtype=np.float64)
    causal = idx[:, None] >= idx[None, :]
    mask = np.where(causal[None], np.exp(-lg * CHUNK)[:, None, None], 0.0)
    q_dec = np.exp(lg[None, :] * (idx[:, None] + 1.0))
    k_dec = np.exp(lg[None, :] * (CHUNK - 1.0 - idx[:, None])) * K_SCALE
    per_lane = lambda a: np.repeat(a, HD, axis=1)
    return jnp.asarray(mask, F32), jnp.asarray(per_lane(q_dec), F32), jnp.asarray(per_lane(k_dec), F32)


def kernel(x_prompt, x_sample, state_ret, norm_g, w_in, w_out, gm_ws, gm_b, gm_ln_g, gm_ln_b, final_g):
    depth = w_in.shape[0]
    batch, seq, _ = x_prompt.shape
    n_rows = x_sample.shape[0]
    assert x_sample.shape[1] == 1

    win_b = w_in.astype(BF16)
    wout_b = w_out.astype(BF16)
    rope_p = _rope_tables(np.arange(seq))
    rope_s = _rope_tables(PAST_LEN + np.arange(1))
    mask, q_dec, k_dec = _retention_tables()
    fg = final_g.reshape(1, D_MODEL)
    pre_args = (norm_g, win_b, rope_s, gm_ws, gm_b, gm_ln_g, gm_ln_b)

    h_p, h_s, shared, vns = x_prompt, x_sample, None, []
    q, k, v, scv, gr, m, vn = _sample_stage(h_s, None, pre_args, fg, layer_post=None, layer_pre=0)
    for l in range(depth):
        last = l == depth - 1
        vns.append(vn)
        gb = jnp.repeat(gm_b[l].T, HD, axis=1)
        h_p, states_p, opart, states_s = _prompt_layer(
            h_p, norm_g, win_b, wout_b, rope_p, q_dec, k_dec, mask, gm_ws, gb, gm_ln_g, gm_ln_b, fg,
            q, k, v, state_ret, shared, layer=l, chunks_per_step=CHUNKS_PER_STEP, final_norm=last)
        shared = (states_p, states_s)
        post_args = (scv, opart, gr, m, wout_b)
        if last:
            (y_s,) = _sample_stage(h_s, post_args, None, fg, layer_post=l, layer_pre=None)
        else:
            h_s, q, k, v, scv, gr, m, vn = _sample_stage(h_s, post_args, pre_args, fg, layer_post=l, layer_pre=l + 1)

    return h_p, y_s, states_p, states_s, jnp.stack(vns).reshape(depth, n_rows, 1, HEADS, HD)
```

```python
import functools

import jax
import jax.numpy as jnp
import numpy as np
from jax import lax
from jax.experimental import pallas as pl
from jax.experimental.pallas import tpu as pltpu

F32 = jnp.float32
BF16 = jnp.bfloat16

D_MODEL = 1024
HEADS = 8
HD = 128
LANES, SUBLANES = 128, 8
WIDTH = HEADS * HD
N_SEG = 7
MIX = 2 * WIDTH
CHUNK = 128
PAST_LEN = 16384
ROPE_BASE = 10000.0
EPS = 1e-6
SEG_Q, SEG_K, SEG_V, SEG_GR, SEG_U, SEG_VG, SEG_GG = range(N_SEG)

GAMMA = tuple(1.0 - 2.0 ** (-5.0 - h) for h in range(HEADS))
CHUNK_DECAY = tuple(g ** CHUNK for g in GAMMA)
K_SCALE = HD ** -0.5

CHUNKS_PER_STEP = 4
V7X_VMEM_BYTES = 64 * 1024 * 1024
VMEM_LIMIT_BYTES = V7X_VMEM_BYTES - 2 * 1024 * 1024


def _silu(x):
    return x * (1.0 / (1.0 + jnp.exp(-x)))


def _rms_scale(x):
    return x * lax.rsqrt(jnp.mean(x * x, axis=-1, keepdims=True) + EPS)


def _layernorm(x, g, b):
    mu = jnp.mean(x, axis=-1, keepdims=True)
    xc = x - mu
    var = jnp.mean(xc * xc, axis=-1, keepdims=True)
    return xc * lax.rsqrt(var + EPS) * g + b


def _rope(x, cos, sin):
    return x * cos + pltpu.roll(x, HD // 2, 1) * sin


def _head(h):
    return slice(h * HD, (h + 1) * HD)


def _dot(a, b):
    return jnp.dot(a, b, preferred_element_type=F32)


def _const_spec(shape):
    zeros = (0,) * len(shape)
    return pl.BlockSpec(shape, lambda *_: zeros)


def _layer_spec(shape, layer, **kwargs):
    index = (layer,) + (0,) * len(shape)
    return pl.BlockSpec((None,) + shape, lambda *_: index, **kwargs)


def _resident_spec(shape):
    zeros = (0,) * len(shape)
    return pl.BlockSpec(shape, lambda *_: zeros, pipeline_mode=pl.Buffered(1))


def _chunk_rows(c):
    return slice(c * CHUNK, (c + 1) * CHUNK)


def _sample_state_step(q_ref, k_ref, v_ref, gamma_ref, s0_ref, write_new, opart_ref, head0):
    heads = opart_ref.shape[1] // HD
    tiles = [r[:, _head(hh)] for r in (k_ref, q_ref) for hh in range(heads)]
    pad = jnp.zeros((LANES - len(tiles) * SUBLANES, HD), F32)
    cols = jnp.concatenate(tiles + [pad], axis=0).T
    for hh in range(heads):
        for j in range(SUBLANES):
            s_old = s0_ref[j, hh]
            k_lane = hh * SUBLANES + j
            q_lane = heads * SUBLANES + k_lane
            k_col = jnp.broadcast_to(cols[:, k_lane:k_lane + 1], (HD, HD))
            q_col = jnp.broadcast_to(cols[:, q_lane:q_lane + 1], (HD, HD))
            write_new(j, hh, s_old * gamma_ref[head0 + hh] + k_col * v_ref[j:j + 1, _head(hh)])
            opart_ref[j:j + 1, _head(hh)] = jnp.sum(q_col * s_old, axis=0, keepdims=True)


def _prompt_kernel(x_ref, ng_ref, win_ref, wout_ref, cos_ref, sin_ref, qdec_ref, kdec_ref, mask_ref,
                   ws_ref, gb_ref, lng_ref, lnb_ref, fg_ref, gamma_ref, q_ref, k_ref, v_ref, s0_ref, *refs,
                   layer, final_norm, aliased):
    refs = refs[N_SHARED * aliased:]
    y_ref, sall_ref, opart_ref, snew_ref = refs[:4]
    xb, qdb, kdt, vb, gr, ug, vnb, om, wtril, sb, kvs, scb = refs[4:]
    s_ref = sall_ref if aliased else sall_ref.at[layer]
    ng, lng, lnb = (r[layer:layer + 1, :] for r in (ng_ref, lng_ref, lnb_ref))

    t = pl.program_id(1)
    n_chunks = x_ref.shape[0] // CHUNK

    @pl.when(t == 0)
    def _start_of_row():
        sall_ref[...] = jnp.zeros(sall_ref.shape, F32)
        row = lax.broadcasted_iota(jnp.int32, (CHUNK, CHUNK), 0)
        col = lax.broadcasted_iota(jnp.int32, (CHUNK, CHUNK), 1)
        for h in range(HEADS):
            wtril[h] = jnp.where(row >= col, ws_ref[h], 0.0).astype(BF16)

    def proj(seg, pair=None):
        lo, width = (seg * WIDTH, WIDTH) if pair is None else (seg * WIDTH + pair * 2 * HD, 2 * HD)
        return _dot(xb[...], win_ref[:, lo:lo + width])

    pairs = range(HEADS // 2)
    pair_cols = lambda pair: slice(pair * 2 * HD, (pair + 1) * 2 * HD)

    xb[...] = (_rms_scale(x_ref[...]) * ng).astype(BF16)

    head_blocks = WIDTH // opart_ref.shape[1]
    head0 = lax.rem(pl.program_id(0) * pl.num_programs(1) + t, head_blocks) * (HEADS // head_blocks)
    if aliased:
        def write_new(j, hh, s_new):
            snew_ref[j, hh] = s_new
    else:
        def write_new(j, hh, s_new):
            for l in range(snew_ref.shape[0]):
                snew_ref[l, j, hh] = s_new if l == layer else jnp.zeros_like(s_new)
    _sample_state_step(q_ref, k_ref, v_ref, gamma_ref, s0_ref, write_new, opart_ref, head0)

    cos, sin = cos_ref[...], sin_ref[...]
    for pair in pairs:
        acc = proj(SEG_Q, pair)
        for i in range(2):
            h = 2 * pair + i
            r = _rope(acc[:, _head(i)], cos, sin)
            for c in range(n_chunks):
                rows = _chunk_rows(c)
                qdb[rows, _head(h)] = (r[rows] * qdec_ref[:, _head(h)]).astype(BF16)

    for pair in pairs:
        acc = proj(SEG_K, pair)
        for i in range(2):
            h = 2 * pair + i
            r = _rope(acc[:, _head(i)], cos, sin)
            for c in range(n_chunks):
                kd = r[_chunk_rows(c)] * kdec_ref[:, _head(h)]
                kdt[c, h] = kd.T.astype(BF16)

    for pair in pairs:
        vb[:, pair_cols(pair)] = proj(SEG_V, pair).astype(BF16)

    for c in range(n_chunks):
        rows = _chunk_rows(c)
        for h in range(HEADS):
            hs = _head(h)
            scb[c, h] = (_dot(qdb[rows, hs], kdt[c, h]) * mask_ref[h]).astype(BF16)
            kvs[c, h] = _dot(kdt[c, h], vb[rows, hs])

    for pair in pairs:
        gr[:, pair_cols(pair)] = _silu(proj(SEG_GR, pair))
    vnb[...] = _layernorm(proj(SEG_VG), lng, lnb).astype(BF16)

    for h in range(HEADS):
        s = s_ref[h]
        for c in range(n_chunks):
            sb[c, h] = s.astype(BF16)
            s = s * CHUNK_DECAY[h] + kvs[c, h]
        s_ref[h] = s

    for c in range(n_chunks):
        rows = _chunk_rows(c)
        for h in range(HEADS):
            hs = _head(h)
            lhs = jnp.concatenate([scb[c, h], qdb[rows, hs]], axis=1)
            rhs = jnp.concatenate([vb[rows, hs], sb[c, h]], axis=0)
            om[rows, hs] = (_rms_scale(_dot(lhs, rhs)) * gr[rows, hs]).astype(BF16)

    for pair in pairs:
        ug[:, pair_cols(pair)] = proj(SEG_U, pair) * _silu(proj(SEG_GG, pair))

    for c in range(n_chunks):
        rows = _chunk_rows(c)
        for h in range(HEADS):
            hs = _head(h)
            s = _dot(wtril[h], vnb[rows, hs]) + gb_ref[:, hs]
            om[rows, WIDTH + h * HD:WIDTH + (h + 1) * HD] = (ug[rows, hs] * s).astype(BF16)

    y = x_ref[...] + _dot(om[...], wout_ref[...])
    if final_norm:
        y = _rms_scale(y) * fg_ref[...]
    y_ref[...] = y


N_PLAIN_OPERANDS = 19
N_SHARED = 2
OUT_PROMPT_STATES, OUT_SAMPLE_STATES = 1, 3


def _prompt_layer(x, ng, win, wout, rope, qdec, kdec, mask, ws, gb, lng, lnb, fg, q, k, v, states, shared, *,
                  layer, chunks_per_step, final_norm):
    batch, seq, _ = x.shape
    depth, n_rows = states.shape[:2]
    aliased = shared is not None
    tok = chunks_per_step * CHUNK
    steps_per_row = seq // tok
    heads = n_rows * HEADS // (batch * steps_per_row * SUBLANES)
    head_blocks = HEADS // heads
    assert heads * head_blocks == HEADS and batch * steps_per_row == (n_rows // SUBLANES) * head_blocks
    tile = lambda b, t: ((b * steps_per_row + t) // head_blocks, (b * steps_per_row + t) % head_blocks)

    row_spec = pl.BlockSpec((None, tok, D_MODEL), lambda b, t: (b, t, 0))
    pos_spec = pl.BlockSpec((tok, HD), lambda b, t: (t, 0))
    tile_spec = pl.BlockSpec((SUBLANES, heads * HD), tile)
    state_spec = pl.BlockSpec((None, SUBLANES, heads, HD, HD), lambda b, t: (layer, *tile(b, t), 0, 0))
    all_layers_spec = pl.BlockSpec((depth, SUBLANES, heads, HD, HD), lambda b, t: (0, *tile(b, t), 0, 0))
    if aliased:
        pstate_spec = pl.BlockSpec((None, None, HEADS, HD, HD), lambda b, t: (layer, b, 0, 0, 0))
    else:
        pstate_spec = pl.BlockSpec((depth, None, HEADS, HD, HD), lambda b, t: (0, b, 0, 0, 0))
    gamma = jnp.asarray(np.broadcast_to(np.asarray(GAMMA)[:, None, None], (HEADS, 1, HD)), F32)
    bf16_rows = lambda width: pltpu.VMEM((tok, width), BF16)
    per_chunk_head = lambda dtype: pltpu.VMEM((chunks_per_step, HEADS, CHUNK, CHUNK), dtype)

    return pl.pallas_call(
        functools.partial(_prompt_kernel, layer=layer, final_norm=final_norm, aliased=aliased),
        grid=(batch, steps_per_row),
        in_specs=[
            row_spec,
            _const_spec((depth, D_MODEL)),
            _resident_spec((D_MODEL, N_SEG * WIDTH)),
            _resident_spec((MIX, D_MODEL)),
            pos_spec, pos_spec,
            _const_spec((CHUNK, WIDTH)), _const_spec((CHUNK, WIDTH)),
            _const_spec((HEADS, CHUNK, CHUNK)),
            _layer_spec((HEADS, CHUNK, CHUNK), layer),
            _const_spec((CHUNK, WIDTH)),
            _const_spec((depth, WIDTH)), _const_spec((depth, WIDTH)),
            _const_spec((1, D_MODEL)),
            _const_spec((HEADS, 1, HD)),
            tile_spec, tile_spec, tile_spec,
            state_spec,
            *[pl.BlockSpec(memory_space=pl.ANY)] * (N_SHARED * aliased),
        ],
        out_specs=[row_spec, pstate_spec, tile_spec, state_spec if aliased else all_layers_spec],
        out_shape=[jax.ShapeDtypeStruct(x.shape, F32),
                   jax.ShapeDtypeStruct((depth, batch, HEADS, HD, HD), F32),
                   jax.ShapeDtypeStruct((n_rows, WIDTH), F32),
                   jax.ShapeDtypeStruct(states.shape, F32)],
        input_output_aliases=({N_PLAIN_OPERANDS: OUT_PROMPT_STATES, N_PLAIN_OPERANDS + 1: OUT_SAMPLE_STATES}
                              if aliased else {}),
        scratch_shapes=[
            bf16_rows(D_MODEL),
            bf16_rows(WIDTH),
            per_chunk_head(BF16),
            bf16_rows(WIDTH),
            pltpu.VMEM((tok, WIDTH), F32),
            pltpu.VMEM((tok, WIDTH), F32),
            bf16_rows(WIDTH),
            bf16_rows(MIX),
            pltpu.VMEM((HEADS, CHUNK, CHUNK), BF16),
            per_chunk_head(BF16),
            per_chunk_head(F32),
            per_chunk_head(BF16),
        ],
        compiler_params=pltpu.CompilerParams(
            dimension_semantics=("arbitrary", "arbitrary"),
            vmem_limit_bytes=VMEM_LIMIT_BYTES,
        ),
        name="prompt_layer",
    )(x, ng, win, wout, *rope, qdec, kdec, mask, ws, gb, lng, lnb, fg, gamma, q, k, v, states,
      *(shared if aliased else ()))


N_WOUT_BLOCKS = MIX // WIDTH


def _finish_layer(h, scv_ref, opart_ref, gr_ref, m_ref, wout_ref):
    parts = []
    for hd in range(HEADS):
        hs = _head(hd)
        o = scv_ref[:, hs] + GAMMA[hd] * opart_ref[:, hs]
        parts.append((_rms_scale(o) * gr_ref[:, hs]).astype(BF16))
    om = jnp.concatenate(parts + [m_ref[...]], axis=1)
    return h + _dot(om, wout_ref[...])


def _sample_start_kernel(*refs, post, layer):
    h_ref, refs = refs[0], refs[1:]
    if post:
        post_refs, refs = refs[:5], refs[5:]
    ng_ref, win_ref, wout_ref, cos_ref, sin_ref, ws_ref, gmb_ref, lng_ref, lnb_ref = refs[:9]
    refs = refs[9:]
    if post:
        hout_ref, refs = refs[0], refs[1:]
    q_ref, k_ref, v_ref, scv_ref, gr_ref, m_ref, vn_ref, winb_ref, woutb_ref, xb, u_keep = refs
    j = pl.program_id(0)

    @pl.when(j == 0)
    def _tokens():
        h = h_ref[...]
        if post:
            h = _finish_layer(h, *post_refs)
            hout_ref[...] = h
        xb[...] = (_rms_scale(h) * ng_ref[layer:layer + 1, :]).astype(BF16)

    @pl.when(j < N_SEG)
    def _segment():
        w = win_ref[...].astype(BF16)
        winb_ref[...] = w
        acc = _dot(xb[...], w)

        @pl.when(j == SEG_Q)
        def _():
            for hd in range(HEADS):
                q_ref[:, _head(hd)] = _rope(acc[:, _head(hd)], cos_ref[...], sin_ref[...])

        @pl.when(j == SEG_K)
        def _():
            for hd in range(HEADS):
                hs = _head(hd)
                k = _rope(acc[:, hs], cos_ref[...], sin_ref[...]) * K_SCALE
                k_ref[:, hs] = k
                qk = jnp.sum(q_ref[:, hs] * k, axis=-1, keepdims=True)
                scv_ref[:, hs] = jnp.broadcast_to(qk, k.shape)

        @pl.when(j == SEG_V)
        def _():
            v_ref[...] = acc
            scv_ref[...] = scv_ref[...] * acc

        @pl.when(j == SEG_GR)
        def _():
            gr_ref[...] = _silu(acc)

        @pl.when(j == SEG_U)
        def _():
            u_keep[...] = acc

        @pl.when(j == SEG_VG)
        def _():
            vn_ref[...] = _layernorm(acc, lng_ref[layer:layer + 1, :], lnb_ref[layer:layer + 1, :])

        @pl.when(j == SEG_GG)
        def _():
            for hd in range(HEADS):
                hs = _head(hd)
                s = vn_ref[:, hs] * ws_ref[hd, 0:1, 0:1] + gmb_ref[layer, hd:hd + 1, 0:1]
                m_ref[:, hs] = (u_keep[:, hs] * _silu(acc[:, hs]) * s).astype(BF16)

    @pl.when(j >= N_SEG)
    def _out_weight():
        woutb_ref[...] = wout_ref[...].astype(BF16)


def _sample_final_kernel(h_ref, scv_ref, opart_ref, gr_ref, m_ref, wout_ref, fg_ref, y_ref):
    y_ref[...] = _rms_scale(_finish_layer(h_ref[...], scv_ref, opart_ref, gr_ref, m_ref, wout_ref)) * fg_ref[...]


def _token_spec(n_rows):
    return pl.BlockSpec((n_rows, None, D_MODEL), lambda *_: (0, 0, 0))


def _sample_start(h, post_args, ng, w_in, w_out, rope, ws, gmb, lng, lnb, *, layer):
    n_rows, depth = h.shape[0], w_in.shape[0]
    post = post_args is not None
    full = _const_spec((n_rows, WIDTH))
    vecs = _const_spec((depth, WIDTH))
    seg = lambda j: jnp.minimum(j, N_SEG - 1)
    out_block = lambda j: jnp.clip(j - N_SEG, 0, N_WOUT_BLOCKS - 1)
    f32_rows = jax.ShapeDtypeStruct((n_rows, WIDTH), F32)
    args = [h] + (list(post_args) if post else []) + [ng, w_in, w_out, *rope, ws, gmb, lng, lnb]
    in_specs = [_token_spec(n_rows) if h.ndim == 3 else full]
    if post:
        in_specs += [full, full, full, full, _resident_spec((MIX, D_MODEL))]
    in_specs += [vecs,
                 pl.BlockSpec((None, D_MODEL, WIDTH), lambda j: (layer, 0, seg(j))),
                 pl.BlockSpec((None, WIDTH, D_MODEL), lambda j: (layer, out_block(j), 0)),
                 _const_spec((1, HD)), _const_spec((1, HD)),
                 _layer_spec((HEADS, CHUNK, CHUNK), layer), _const_spec((depth, HEADS, CHUNK)), vecs, vecs]
    out_shape = (([f32_rows] if post else []) + [f32_rows] * 5
                 + [jax.ShapeDtypeStruct((n_rows, WIDTH), BF16), f32_rows,
                    jax.ShapeDtypeStruct((D_MODEL, N_SEG * WIDTH), BF16), jax.ShapeDtypeStruct((MIX, D_MODEL), BF16)])
    out_specs = [full] * (len(out_shape) - 2) + [pl.BlockSpec((D_MODEL, WIDTH), lambda j: (0, seg(j))),
                                                 pl.BlockSpec((WIDTH, D_MODEL), lambda j: (out_block(j), 0))]
    return pl.pallas_call(
        functools.partial(_sample_start_kernel, post=post, layer=layer),
        grid=(N_SEG + N_WOUT_BLOCKS,),
        in_specs=in_specs,
        out_specs=out_specs,
        out_shape=out_shape,
        scratch_shapes=[pltpu.VMEM((n_rows, D_MODEL), BF16),
                        pltpu.VMEM((n_rows, WIDTH), F32)],
        compiler_params=pltpu.CompilerParams(
            dimension_semantics=("arbitrary",),
            vmem_limit_bytes=VMEM_LIMIT_BYTES,
        ),
        name="sample_start",
    )(*args)


def _sample_final(h, post_args, fg):
    n_rows = h.shape[0]
    full = _const_spec((n_rows, WIDTH))
    return pl.pallas_call(
        _sample_final_kernel,
        grid=(1,),
        in_specs=[full, full, full, full, full, _const_spec((MIX, D_MODEL)), _const_spec((1, D_MODEL))],
        out_specs=_token_spec(n_rows),
        out_shape=jax.ShapeDtypeStruct((n_rows, 1, D_MODEL), F32),
        compiler_params=pltpu.CompilerParams(
            dimension_semantics=("arbitrary",),
            vmem_limit_bytes=VMEM_LIMIT_BYTES,
        ),
        name="sample_final",
    )(h, *post_args, fg)


def _rope_tables(pos):
    inv = ROPE_BASE ** (-np.arange(0, HD, 2, dtype=np.float64) / HD)
    ang = np.asarray(pos, np.float64)[:, None] * inv[None, :]
    c, s = np.cos(ang), np.sin(ang)
    return (jnp.asarray(np.concatenate([c, c], axis=-1), F32),
            jnp.asarray(np.concatenate([-s, s], axis=-1), F32))


def _retention_tables():
    lg = np.log(np.asarray(GAMMA, np.float64))
    idx = np.arange(CHUNK, dtype=np.float64)
    causal = idx[:, None] >= idx[None, :]
    mask = np.where(causal[None], np.exp(-lg * CHUNK)[:, None, None], 0.0)
    q_dec = np.exp(lg[None, :] * (idx[:, None] + 1.0))
    k_dec = np.exp(lg[None, :] * (CHUNK - 1.0 - idx[:, None])) * K_SCALE
    per_lane = lambda a: np.repeat(a, HD, axis=1)
    return jnp.asarray(mask, F32), jnp.asarray(per_lane(q_dec), F32), jnp.asarray(per_lane(k_dec), F32)


def kernel(x_prompt, x_sample, state_ret, norm_g, w_in, w_out, gm_ws, gm_b, gm_ln_g, gm_ln_b, final_g):
    depth = w_in.shape[0]
    batch, seq, _ = x_prompt.shape
    n_rows = x_sample.shape[0]
    assert x_sample.shape[1] == 1

    rope_p = _rope_tables(np.arange(seq))
    rope_s = _rope_tables(PAST_LEN + np.arange(1))
    mask, q_dec, k_dec = _retention_tables()
    fg = final_g.reshape(1, D_MODEL)
    start = functools.partial(_sample_start, ng=norm_g, w_in=w_in, w_out=w_out, rope=rope_s, ws=gm_ws, gmb=gm_b,
                              lng=gm_ln_g, lnb=gm_ln_b)

    h_p, h_s, shared, vns = x_prompt, x_sample, None, []
    q, k, v, scv, gr, m, vn, win_b, wout_b = start(h_s, None, layer=0)
    for l in range(depth):
        vns.append(vn)
        gb = jnp.repeat(gm_b[l].T, HD, axis=1)
        h_p, states_p, opart, states_s = _prompt_layer(
            h_p, norm_g, win_b, wout_b, rope_p, q_dec, k_dec, mask, gm_ws, gb, gm_ln_g, gm_ln_b, fg,
            q, k, v, state_ret, shared, layer=l, chunks_per_step=CHUNKS_PER_STEP, final_norm=l == depth - 1)
        shared = (states_p, states_s)
        post_args = (scv, opart, gr, m, wout_b)
        if l == depth - 1:
            y_s = _sample_final(h_s, post_args, fg)
        else:
            h_s, q, k, v, scv, gr, m, vn, win_b, wout_b = start(h_s, post_args, layer=l + 1)

    return h_p, y_s, states_p, states_s, jnp.stack(vns).reshape(depth, n_rows, 1, HEADS, HD)
```

```python
import functools

import jax
import jax.numpy as jnp
import numpy as np
from jax import lax
from jax.experimental import pallas as pl
from jax.experimental.pallas import tpu as pltpu

F32 = jnp.float32
BF16 = jnp.bfloat16

D_MODEL = 1024
HEADS = 8
HD = 128
LANES, SUBLANES = 128, 8
WIDTH = HEADS * HD
N_SEG = 7
MIX = 2 * WIDTH
CHUNK = 128
PAST_LEN = 16384
ROPE_BASE = 10000.0
EPS = 1e-6
SEG_Q, SEG_K, SEG_V, SEG_GR, SEG_U, SEG_VG, SEG_GG = range(N_SEG)

GAMMA = tuple(1.0 - 2.0 ** (-5.0 - h) for h in range(HEADS))
CHUNK_DECAY = tuple(g ** CHUNK for g in GAMMA)
K_SCALE = HD ** -0.5

CHUNKS_PER_STEP = 4
V7X_VMEM_BYTES = 64 * 1024 * 1024
VMEM_LIMIT_BYTES = V7X_VMEM_BYTES - 2 * 1024 * 1024


def _silu(x):
    return x * (1.0 / (1.0 + jnp.exp(-x)))


def _rms_scale(x):
    return x * lax.rsqrt(jnp.mean(x * x, axis=-1, keepdims=True) + EPS)


def _layernorm(x, g, b):
    mu = jnp.mean(x, axis=-1, keepdims=True)
    xc = x - mu
    var = jnp.mean(xc * xc, axis=-1, keepdims=True)
    return xc * lax.rsqrt(var + EPS) * g + b


def _rope(x, cos, sin):
    return x * cos + pltpu.roll(x, HD // 2, 1) * sin


def _head(h):
    return slice(h * HD, (h + 1) * HD)


def _dot(a, b):
    return jnp.dot(a, b, preferred_element_type=F32)


def _const_spec(shape):
    zeros = (0,) * len(shape)
    return pl.BlockSpec(shape, lambda *_: zeros)


def _layer_spec(shape, layer, **kwargs):
    index = (layer,) + (0,) * len(shape)
    return pl.BlockSpec((None,) + shape, lambda *_: index, **kwargs)


def _resident_spec(shape):
    zeros = (0,) * len(shape)
    return pl.BlockSpec(shape, lambda *_: zeros, pipeline_mode=pl.Buffered(1))


def _chunk_rows(c):
    return slice(c * CHUNK, (c + 1) * CHUNK)


def _sample_state_step(q_ref, k_ref, v_ref, gamma_ref, s0_ref, write_new, opart_ref, head0):
    heads = opart_ref.shape[1] // HD
    tiles = [r[:, _head(hh)] for r in (k_ref, q_ref) for hh in range(heads)]
    pad = jnp.zeros((LANES - len(tiles) * SUBLANES, HD), F32)
    cols = jnp.concatenate(tiles + [pad], axis=0).T
    for hh in range(heads):
        for j in range(SUBLANES):
            s_old = s0_ref[j, hh]
            k_lane = hh * SUBLANES + j
            q_lane = heads * SUBLANES + k_lane
            k_col = jnp.broadcast_to(cols[:, k_lane:k_lane + 1], (HD, HD))
            q_col = jnp.broadcast_to(cols[:, q_lane:q_lane + 1], (HD, HD))
            write_new(j, hh, s_old * gamma_ref[head0 + hh] + k_col * v_ref[j:j + 1, _head(hh)])
            opart_ref[j:j + 1, _head(hh)] = jnp.sum(q_col * s_old, axis=0, keepdims=True)


def _prompt_kernel(x_ref, win_ref, wout_ref, cos_ref, sin_ref, qdec_ref, kdec_ref, mask_ref,
                   ws_ref, gb_ref, lng_ref, lnb_ref, fg_ref, gamma_ref, q_ref, k_ref, v_ref, s0_ref, *refs,
                   layer, final_norm, aliased):
    refs = refs[N_SHARED * aliased:]
    y_ref, sall_ref, opart_ref, snew_ref = refs[:4]
    xb, qdb, kdt, vb, gr, ug, vnb, om, wtril, sb, kvs, scb = refs[4:]
    s_ref = sall_ref if aliased else sall_ref.at[layer]
    lng, lnb = (r[layer:layer + 1, :] for r in (lng_ref, lnb_ref))

    t = pl.program_id(1)
    n_chunks = x_ref.shape[0] // CHUNK

    @pl.when(t == 0)
    def _start_of_row():
        sall_ref[...] = jnp.zeros(sall_ref.shape, F32)
        row = lax.broadcasted_iota(jnp.int32, (CHUNK, CHUNK), 0)
        col = lax.broadcasted_iota(jnp.int32, (CHUNK, CHUNK), 1)
        for h in range(HEADS):
            wtril[h] = jnp.where(row >= col, ws_ref[h], 0.0).astype(BF16)

    def proj(seg, pair=None):
        lo, width = (seg * WIDTH, WIDTH) if pair is None else (seg * WIDTH + pair * 2 * HD, 2 * HD)
        return _dot(xb[...], win_ref[:, lo:lo + width])

    pairs = range(HEADS // 2)
    pair_cols = lambda pair: slice(pair * 2 * HD, (pair + 1) * 2 * HD)

    xb[...] = _rms_scale(x_ref[...]).astype(BF16)

    head_blocks = WIDTH // opart_ref.shape[1]
    head0 = lax.rem(pl.program_id(0) * pl.num_programs(1) + t, head_blocks) * (HEADS // head_blocks)
    if aliased:
        def write_new(j, hh, s_new):
            snew_ref[j, hh] = s_new
    else:
        def write_new(j, hh, s_new):
            for l in range(snew_ref.shape[0]):
                snew_ref[l, j, hh] = s_new if l == layer else jnp.zeros_like(s_new)
    _sample_state_step(q_ref, k_ref, v_ref, gamma_ref, s0_ref, write_new, opart_ref, head0)

    cos, sin = cos_ref[...], sin_ref[...]
    for pair in pairs:
        acc = proj(SEG_Q, pair)
        for i in range(2):
            h = 2 * pair + i
            r = _rope(acc[:, _head(i)], cos, sin)
            for c in range(n_chunks):
                rows = _chunk_rows(c)
                qdb[rows, _head(h)] = (r[rows] * qdec_ref[:, _head(h)]).astype(BF16)

    for pair in pairs:
        acc = proj(SEG_K, pair)
        for i in range(2):
            h = 2 * pair + i
            r = _rope(acc[:, _head(i)], cos, sin)
            for c in range(n_chunks):
                kd = r[_chunk_rows(c)] * kdec_ref[:, _head(h)]
                kdt[c, h] = kd.T.astype(BF16)

    for pair in pairs:
        vb[:, pair_cols(pair)] = proj(SEG_V, pair).astype(BF16)

    for c in range(n_chunks):
        rows = _chunk_rows(c)
        for h in range(HEADS):
            hs = _head(h)
            scb[c, h] = (_dot(qdb[rows, hs], kdt[c, h]) * mask_ref[h]).astype(BF16)
            kvs[c, h] = _dot(kdt[c, h], vb[rows, hs])

    for pair in pairs:
        gr[:, pair_cols(pair)] = _silu(proj(SEG_GR, pair))
    vnb[...] = _layernorm(proj(SEG_VG), lng, lnb).astype(BF16)

    for h in range(HEADS):
        s = s_ref[h]
        for c in range(n_chunks):
            sb[c, h] = s.astype(BF16)
            s = s * CHUNK_DECAY[h] + kvs[c, h]
        s_ref[h] = s

    for c in range(n_chunks):
        rows = _chunk_rows(c)
        for h in range(HEADS):
            hs = _head(h)
            lhs = jnp.concatenate([scb[c, h], qdb[rows, hs]], axis=1)
            rhs = jnp.concatenate([vb[rows, hs], sb[c, h]], axis=0)
            om[rows, hs] = (_rms_scale(_dot(lhs, rhs)) * gr[rows, hs]).astype(BF16)

    for pair in pairs:
        ug[:, pair_cols(pair)] = proj(SEG_U, pair) * _silu(proj(SEG_GG, pair))

    for c in range(n_chunks):
        rows = _chunk_rows(c)
        for h in range(HEADS):
            hs = _head(h)
            s = _dot(wtril[h], vnb[rows, hs]) + gb_ref[:, hs]
            om[rows, WIDTH + h * HD:WIDTH + (h + 1) * HD] = (ug[rows, hs] * s).astype(BF16)

    y = x_ref[...] + _dot(om[...], wout_ref[...])
    if final_norm:
        y = _rms_scale(y) * fg_ref[...]
    y_ref[...] = y


N_PLAIN_OPERANDS = 18
N_SHARED = 2
OUT_PROMPT_STATES, OUT_SAMPLE_STATES = 1, 3


def _prompt_layer(x, win, wout, rope, qdec, kdec, mask, ws, gb, lng, lnb, fg, q, k, v, states, shared, *,
                  layer, chunks_per_step, final_norm):
    batch, seq, _ = x.shape
    depth, n_rows = states.shape[:2]
    aliased = shared is not None
    tok = chunks_per_step * CHUNK
    steps_per_row = seq // tok
    heads = n_rows * HEADS // (batch * steps_per_row * SUBLANES)
    head_blocks = HEADS // heads
    assert heads * head_blocks == HEADS and batch * steps_per_row == (n_rows // SUBLANES) * head_blocks
    tile = lambda b, t: ((b * steps_per_row + t) // head_blocks, (b * steps_per_row + t) % head_blocks)

    row_spec = pl.BlockSpec((None, tok, D_MODEL), lambda b, t: (b, t, 0))
    pos_spec = pl.BlockSpec((tok, HD), lambda b, t: (t, 0))
    tile_spec = pl.BlockSpec((SUBLANES, heads * HD), tile)
    state_spec = pl.BlockSpec((None, SUBLANES, heads, HD, HD), lambda b, t: (layer, *tile(b, t), 0, 0))
    all_layers_spec = pl.BlockSpec((depth, SUBLANES, heads, HD, HD), lambda b, t: (0, *tile(b, t), 0, 0))
    if aliased:
        pstate_spec = pl.BlockSpec((None, None, HEADS, HD, HD), lambda b, t: (layer, b, 0, 0, 0))
    else:
        pstate_spec = pl.BlockSpec((depth, None, HEADS, HD, HD), lambda b, t: (0, b, 0, 0, 0))
    gamma = jnp.asarray(np.broadcast_to(np.asarray(GAMMA)[:, None, None], (HEADS, 1, HD)), F32)
    bf16_rows = lambda width: pltpu.VMEM((tok, width), BF16)
    per_chunk_head = lambda dtype: pltpu.VMEM((chunks_per_step, HEADS, CHUNK, CHUNK), dtype)

    return pl.pallas_call(
        functools.partial(_prompt_kernel, layer=layer, final_norm=final_norm, aliased=aliased),
        grid=(batch, steps_per_row),
        in_specs=[
            row_spec,
            _resident_spec((D_MODEL, N_SEG * WIDTH)),
            _resident_spec((MIX, D_MODEL)),
            pos_spec, pos_spec,
            _const_spec((CHUNK, WIDTH)), _const_spec((CHUNK, WIDTH)),
            _const_spec((HEADS, CHUNK, CHUNK)),
            _layer_spec((HEADS, CHUNK, CHUNK), layer),
            _const_spec((CHUNK, WIDTH)),
            _const_spec((depth, WIDTH)), _const_spec((depth, WIDTH)),
            _const_spec((1, D_MODEL)),
            _const_spec((HEADS, 1, HD)),
            tile_spec, tile_spec, tile_spec,
            state_spec,
            *[pl.BlockSpec(memory_space=pl.ANY)] * (N_SHARED * aliased),
        ],
        out_specs=[row_spec, pstate_spec, tile_spec, state_spec if aliased else all_layers_spec],
        out_shape=[jax.ShapeDtypeStruct(x.shape, F32),
                   jax.ShapeDtypeStruct((depth, batch, HEADS, HD, HD), F32),
                   jax.ShapeDtypeStruct((n_rows, WIDTH), F32),
                   jax.ShapeDtypeStruct(states.shape, F32)],
        input_output_aliases=({N_PLAIN_OPERANDS: OUT_PROMPT_STATES, N_PLAIN_OPERANDS + 1: OUT_SAMPLE_STATES}
                              if aliased else {}),
        scratch_shapes=[
            bf16_rows(D_MODEL),
            bf16_rows(WIDTH),
            per_chunk_head(BF16),
            bf16_rows(WIDTH),
            pltpu.VMEM((tok, WIDTH), F32),
            pltpu.VMEM((tok, WIDTH), F32),
            bf16_rows(WIDTH),
            bf16_rows(MIX),
            pltpu.VMEM((HEADS, CHUNK, CHUNK), BF16),
            per_chunk_head(BF16),
            per_chunk_head(F32),
            per_chunk_head(BF16),
        ],
        compiler_params=pltpu.CompilerParams(
            dimension_semantics=("arbitrary", "arbitrary"),
            vmem_limit_bytes=VMEM_LIMIT_BYTES,
        ),
        name="prompt_layer",
    )(x, win, wout, *rope, qdec, kdec, mask, ws, gb, lng, lnb, fg, gamma, q, k, v, states,
      *(shared if aliased else ()))


N_WOUT_BLOCKS = MIX // WIDTH


def _finish_layer(h, scv_ref, opart_ref, gr_ref, m_ref, wout_ref):
    parts = []
    for hd in range(HEADS):
        hs = _head(hd)
        o = scv_ref[:, hs] + GAMMA[hd] * opart_ref[:, hs]
        parts.append((_rms_scale(o) * gr_ref[:, hs]).astype(BF16))
    om = jnp.concatenate(parts + [m_ref[...]], axis=1)
    return h + _dot(om, wout_ref[...])


def _sample_start_kernel(*refs, post, layer):
    h_ref, refs = refs[0], refs[1:]
    if post:
        post_refs, refs = refs[:5], refs[5:]
    ng_ref, win_ref, wout_ref, cos_ref, sin_ref, ws_ref, gmb_ref, lng_ref, lnb_ref = refs[:9]
    refs = refs[9:]
    if post:
        hout_ref, refs = refs[0], refs[1:]
    q_ref, k_ref, v_ref, scv_ref, gr_ref, m_ref, vn_ref, winb_ref, woutb_ref, xb, u_keep = refs
    j = pl.program_id(0)

    @pl.when(j == 0)
    def _tokens():
        h = h_ref[...]
        if post:
            h = _finish_layer(h, *post_refs)
            hout_ref[...] = h
        xb[...] = _rms_scale(h).astype(BF16)

    @pl.when(j < N_SEG)
    def _segment():
        gain = ng_ref[layer:layer + 1, :]
        tiles = [gain[:, i * LANES:(i + 1) * LANES] for i in range(D_MODEL // LANES)]
        pad = jnp.zeros((LANES - len(tiles), LANES), F32)
        gain_cols = jnp.concatenate(tiles + [pad], axis=0).T
        for i in range(D_MODEL // LANES):
            rows = slice(i * LANES, (i + 1) * LANES)
            winb_ref[rows, :] = (win_ref[rows, :] * gain_cols[:, i:i + 1]).astype(BF16)
        acc = _dot(xb[...], winb_ref[...])

        @pl.when(j == SEG_Q)
        def _():
            for hd in range(HEADS):
                q_ref[:, _head(hd)] = _rope(acc[:, _head(hd)], cos_ref[...], sin_ref[...])

        @pl.when(j == SEG_K)
        def _():
            for hd in range(HEADS):
                hs = _head(hd)
                k = _rope(acc[:, hs], cos_ref[...], sin_ref[...]) * K_SCALE
                k_ref[:, hs] = k
                qk = jnp.sum(q_ref[:, hs] * k, axis=-1, keepdims=True)
                scv_ref[:, hs] = jnp.broadcast_to(qk, k.shape)

        @pl.when(j == SEG_V)
        def _():
            v_ref[...] = acc
            scv_ref[...] = scv_ref[...] * acc

        @pl.when(j == SEG_GR)
        def _():
            gr_ref[...] = _silu(acc)

        @pl.when(j == SEG_U)
        def _():
            u_keep[...] = acc

        @pl.when(j == SEG_VG)
        def _():
            vn_ref[...] = _layernorm(acc, lng_ref[layer:layer + 1, :], lnb_ref[layer:layer + 1, :])

        @pl.when(j == SEG_GG)
        def _():
            for hd in range(HEADS):
                hs = _head(hd)
                s = vn_ref[:, hs] * ws_ref[hd, 0:1, 0:1] + gmb_ref[layer, hd:hd + 1, 0:1]
                m_ref[:, hs] = (u_keep[:, hs] * _silu(acc[:, hs]) * s).astype(BF16)

    @pl.when(j >= N_SEG)
    def _out_weight():
        woutb_ref[...] = wout_ref[...].astype(BF16)


def _sample_final_kernel(h_ref, scv_ref, opart_ref, gr_ref, m_ref, wout_ref, fg_ref, y_ref):
    y_ref[...] = _rms_scale(_finish_layer(h_ref[...], scv_ref, opart_ref, gr_ref, m_ref, wout_ref)) * fg_ref[...]


def _token_spec(n_rows):
    return pl.BlockSpec((n_rows, None, D_MODEL), lambda *_: (0, 0, 0))


def _sample_start(h, post_args, ng, w_in, w_out, rope, ws, gmb, lng, lnb, *, layer):
    n_rows, depth = h.shape[0], w_in.shape[0]
    post = post_args is not None
    full = _const_spec((n_rows, WIDTH))
    vecs = _const_spec((depth, WIDTH))
    seg = lambda j: jnp.minimum(j, N_SEG - 1)
    out_block = lambda j: jnp.clip(j - N_SEG, 0, N_WOUT_BLOCKS - 1)
    f32_rows = jax.ShapeDtypeStruct((n_rows, WIDTH), F32)
    args = [h] + (list(post_args) if post else []) + [ng, w_in, w_out, *rope, ws, gmb, lng, lnb]
    in_specs = [_token_spec(n_rows) if h.ndim == 3 else full]
    if post:
        in_specs += [full, full, full, full, _resident_spec((MIX, D_MODEL))]
    in_specs += [vecs,
                 pl.BlockSpec((None, D_MODEL, WIDTH), lambda j: (layer, 0, seg(j))),
                 pl.BlockSpec((None, WIDTH, D_MODEL), lambda j: (layer, out_block(j), 0)),
                 _const_spec((1, HD)), _const_spec((1, HD)),
                 _layer_spec((HEADS, CHUNK, CHUNK), layer), _const_spec((depth, HEADS, CHUNK)), vecs, vecs]
    out_shape = (([f32_rows] if post else []) + [f32_rows] * 5
                 + [jax.ShapeDtypeStruct((n_rows, WIDTH), BF16), f32_rows,
                    jax.ShapeDtypeStruct((D_MODEL, N_SEG * WIDTH), BF16), jax.ShapeDtypeStruct((MIX, D_MODEL), BF16)])
    out_specs = [full] * (len(out_shape) - 2) + [pl.BlockSpec((D_MODEL, WIDTH), lambda j: (0, seg(j))),
                                                 pl.BlockSpec((WIDTH, D_MODEL), lambda j: (out_block(j), 0))]
    return pl.pallas_call(
        functools.partial(_sample_start_kernel, post=post, layer=layer),
        grid=(N_SEG + N_WOUT_BLOCKS,),
        in_specs=in_specs,
        out_specs=out_specs,
        out_shape=out_shape,
        scratch_shapes=[pltpu.VMEM((n_rows, D_MODEL), BF16),
                        pltpu.VMEM((n_rows, WIDTH), F32)],
        compiler_params=pltpu.CompilerParams(
            dimension_semantics=("arbitrary",),
            vmem_limit_bytes=VMEM_LIMIT_BYTES,
        ),
        name="sample_start",
    )(*args)


def _sample_final(h, post_args, fg):
    n_rows = h.shape[0]
    full = _const_spec((n_rows, WIDTH))
    return pl.pallas_call(
        _sample_final_kernel,
        grid=(1,),
        in_specs=[full, full, full, full, full, _const_spec((MIX, D_MODEL)), _const_spec((1, D_MODEL))],
        out_specs=_token_spec(n_rows),
        out_shape=jax.ShapeDtypeStruct((n_rows, 1, D_MODEL), F32),
        compiler_params=pltpu.CompilerParams(
            dimension_semantics=("arbitrary",),
            vmem_limit_bytes=VMEM_LIMIT_BYTES,
        ),
        name="sample_final",
    )(h, *post_args, fg)


def _rope_tables(pos):
    inv = ROPE_BASE ** (-np.arange(0, HD, 2, dtype=np.float64) / HD)
    ang = np.asarray(pos, np.float64)[:, None] * inv[None, :]
    c, s = np.cos(ang), np.sin(ang)
    return (jnp.asarray(np.concatenate([c, c], axis=-1), F32),
            jnp.asarray(np.concatenate([-s, s], axis=-1), F32))


def _retention_tables():
    lg = np.log(np.asarray(GAMMA, np.float64))
    idx = np.arange(CHUNK, dtype=np.float64)
    causal = idx[:, None] >= idx[None, :]
    mask = np.where(causal[None], np.exp(-lg * CHUNK)[:, None, None], 0.0)
    q_dec = np.exp(lg[None, :] * (idx[:, None] + 1.0))
    k_dec = np.exp(lg[None, :] * (CHUNK - 1.0 - idx[:, None])) * K_SCALE
    per_lane = lambda a: np.repeat(a, HD, axis=1)
    return jnp.asarray(mask, F32), jnp.asarray(per_lane(q_dec), F32), jnp.asarray(per_lane(k_dec), F32)


def kernel(x_prompt, x_sample, state_ret, norm_g, w_in, w_out, gm_ws, gm_b, gm_ln_g, gm_ln_b, final_g):
    depth = w_in.shape[0]
    batch, seq, _ = x_prompt.shape
    n_rows = x_sample.shape[0]
    assert x_sample.shape[1] == 1

    rope_p = _rope_tables(np.arange(seq))
    rope_s = _rope_tables(PAST_LEN + np.arange(1))
    mask, q_dec, k_dec = _retention_tables()
    fg = final_g.reshape(1, D_MODEL)
    start = functools.partial(_sample_start, ng=norm_g, w_in=w_in, w_out=w_out, rope=rope_s, ws=gm_ws, gmb=gm_b,
                              lng=gm_ln_g, lnb=gm_ln_b)

    h_p, h_s, shared, vns = x_prompt, x_sample, None, []
    q, k, v, scv, gr, m, vn, win_b, wout_b = start(h_s, None, layer=0)
    for l in range(depth):
        vns.append(vn)
        gb = jnp.repeat(gm_b[l].T, HD, axis=1)
        h_p, states_p, opart, states_s = _prompt_layer(
            h_p, win_b, wout_b, rope_p, q_dec, k_dec, mask, gm_ws, gb, gm_ln_g, gm_ln_b, fg,
            q, k, v, state_ret, shared, layer=l, chunks_per_step=CHUNKS_PER_STEP, final_norm=l == depth - 1)
        shared = (states_p, states_s)
        post_args = (scv, opart, gr, m, wout_b)
        if l == depth - 1:
            y_s = _sample_final(h_s, post_args, fg)
        else:
            h_s, q, k, v, scv, gr, m, vn, win_b, wout_b = start(h_s, post_args, layer=l + 1)

    return h_p, y_s, states_p, states_s, jnp.stack(vns).reshape(depth, n_rows, 1, HEADS, HD)
```

```python
import functools

import jax
import jax.numpy as jnp
import numpy as np
from jax import lax
from jax.experimental import pallas as pl
from jax.experimental.pallas import tpu as pltpu

F32 = jnp.float32
BF16 = jnp.bfloat16

D_MODEL = 1024
HEADS = 8
HD = 128
LANES, SUBLANES = 128, 8
WIDTH = HEADS * HD
N_SEG = 7
MIX = 2 * WIDTH
CHUNK = 128
PAST_LEN = 16384
ROPE_BASE = 10000.0
EPS = 1e-6
SEG_Q, SEG_K, SEG_V, SEG_GR, SEG_U, SEG_VG, SEG_GG = range(N_SEG)

GAMMA = tuple(1.0 - 2.0 ** (-5.0 - h) for h in range(HEADS))
CHUNK_DECAY = tuple(g ** CHUNK for g in GAMMA)
K_SCALE = HD ** -0.5

CHUNKS_PER_STEP = 4
V7X_VMEM_BYTES = 64 * 1024 * 1024
VMEM_LIMIT_BYTES = V7X_VMEM_BYTES - 2 * 1024 * 1024


def _silu(x):
    return x * (1.0 / (1.0 + jnp.exp(-x)))


def _rms_scale(x):
    return x * lax.rsqrt(jnp.mean(x * x, axis=-1, keepdims=True) + EPS)


def _layernorm(x, g, b):
    mu = jnp.mean(x, axis=-1, keepdims=True)
    xc = x - mu
    var = jnp.mean(xc * xc, axis=-1, keepdims=True)
    return xc * lax.rsqrt(var + EPS) * g + b


def _rope(x, cos, sin):
    return x * cos + pltpu.roll(x, HD // 2, 1) * sin


def _head(h):
    return slice(h * HD, (h + 1) * HD)


def _dot(a, b):
    return jnp.dot(a, b, preferred_element_type=F32)


def _const_spec(shape):
    zeros = (0,) * len(shape)
    return pl.BlockSpec(shape, lambda *_: zeros)


def _layer_spec(shape, layer, **kwargs):
    index = (layer,) + (0,) * len(shape)
    return pl.BlockSpec((None,) + shape, lambda *_: index, **kwargs)


def _resident_spec(shape):
    zeros = (0,) * len(shape)
    return pl.BlockSpec(shape, lambda *_: zeros, pipeline_mode=pl.Buffered(1))


def _chunk_rows(c):
    return slice(c * CHUNK, (c + 1) * CHUNK)


def _sample_state_step(q_ref, k_ref, v_ref, gamma_ref, s0_ref, write_new, opart_ref, head0):
    heads = opart_ref.shape[1] // HD
    tiles = [r[:, _head(hh)] for r in (k_ref, q_ref) for hh in range(heads)]
    pad = jnp.zeros((LANES - len(tiles) * SUBLANES, HD), F32)
    cols = jnp.concatenate(tiles + [pad], axis=0).T
    for hh in range(heads):
        for j in range(SUBLANES):
            s_old = s0_ref[j, hh]
            k_lane = hh * SUBLANES + j
            q_lane = heads * SUBLANES + k_lane
            k_col = jnp.broadcast_to(cols[:, k_lane:k_lane + 1], (HD, HD))
            q_col = jnp.broadcast_to(cols[:, q_lane:q_lane + 1], (HD, HD))
            write_new(j, hh, s_old * gamma_ref[head0 + hh] + k_col * v_ref[j:j + 1, _head(hh)])
            opart_ref[j:j + 1, _head(hh)] = jnp.sum(q_col * s_old, axis=0, keepdims=True)


def _prompt_kernel(x_ref, win_ref, wout_ref, cos_ref, sin_ref, qdec_ref, kdec_ref, mask_ref,
                   ws_ref, gb_ref, lng_ref, lnb_ref, fg_ref, gamma_ref, q_ref, k_ref, v_ref, s0_ref, *refs,
                   layer, final_norm, aliased):
    refs = refs[N_SHARED * aliased:]
    y_ref, sall_ref, opart_ref, snew_ref = refs[:4]
    xb, qdb, kdt, vb, gr, ug, vnb, om, wtril, sb, kvs, scb = refs[4:]
    s_ref = sall_ref if aliased else sall_ref.at[layer]
    lng, lnb = (r[layer:layer + 1, :] for r in (lng_ref, lnb_ref))

    t = pl.program_id(1)
    n_chunks = x_ref.shape[0] // CHUNK

    @pl.when(t == 0)
    def _start_of_row():
        sall_ref[...] = jnp.zeros(sall_ref.shape, F32)
        row = lax.broadcasted_iota(jnp.int32, (CHUNK, CHUNK), 0)
        col = lax.broadcasted_iota(jnp.int32, (CHUNK, CHUNK), 1)
        for h in range(HEADS):
            wtril[h] = jnp.where(row >= col, ws_ref[h], 0.0).astype(BF16)

    def proj(seg, pair=None):
        lo, width = (seg * WIDTH, WIDTH) if pair is None else (seg * WIDTH + pair * 2 * HD, 2 * HD)
        return _dot(xb[...], win_ref[:, lo:lo + width])

    pairs = range(HEADS // 2)
    pair_cols = lambda pair: slice(pair * 2 * HD, (pair + 1) * 2 * HD)

    xb[...] = _rms_scale(x_ref[...]).astype(BF16)

    head_blocks = WIDTH // opart_ref.shape[1]
    head0 = lax.rem(pl.program_id(0) * pl.num_programs(1) + t, head_blocks) * (HEADS // head_blocks)
    if aliased:
        def write_new(j, hh, s_new):
            snew_ref[j, hh] = s_new
    else:
        def write_new(j, hh, s_new):
            for l in range(snew_ref.shape[0]):
                snew_ref[l, j, hh] = s_new if l == layer else jnp.zeros_like(s_new)
    _sample_state_step(q_ref, k_ref, v_ref, gamma_ref, s0_ref, write_new, opart_ref, head0)

    cos, sin = cos_ref[...], sin_ref[...]
    for pair in pairs:
        acc = proj(SEG_Q, pair)
        for i in range(2):
            h = 2 * pair + i
            r = _rope(acc[:, _head(i)], cos, sin)
            for c in range(n_chunks):
                rows = _chunk_rows(c)
                qdb[rows, _head(h)] = (r[rows] * qdec_ref[:, _head(h)]).astype(BF16)

    for pair in pairs:
        acc = proj(SEG_K, pair)
        for i in range(2):
            h = 2 * pair + i
            r = _rope(acc[:, _head(i)], cos, sin)
            for c in range(n_chunks):
                kd = r[_chunk_rows(c)] * kdec_ref[:, _head(h)]
                kdt[c, h] = kd.T.astype(BF16)

    for pair in pairs:
        vb[:, pair_cols(pair)] = proj(SEG_V, pair).astype(BF16)

    for c in range(n_chunks):
        rows = _chunk_rows(c)
        for h in range(HEADS):
            hs = _head(h)
            scb[c, h] = (_dot(qdb[rows, hs], kdt[c, h]) * mask_ref[h]).astype(BF16)
            kvs[c, h] = _dot(kdt[c, h], vb[rows, hs])

    for pair in pairs:
        gr[:, pair_cols(pair)] = _silu(proj(SEG_GR, pair))
    vnb[...] = _layernorm(proj(SEG_VG), lng, lnb).astype(BF16)

    for h in range(HEADS):
        s = s_ref[h]
        for c in range(n_chunks):
            sb[c, h] = s.astype(BF16)
            s = s * CHUNK_DECAY[h] + kvs[c, h]
        s_ref[h] = s

    for c in range(n_chunks):
        rows = _chunk_rows(c)
        for h in range(HEADS):
            hs = _head(h)
            lhs = jnp.concatenate([scb[c, h], qdb[rows, hs]], axis=1)
            rhs = jnp.concatenate([vb[rows, hs], sb[c, h]], axis=0)
            om[rows, hs] = (_rms_scale(_dot(lhs, rhs)) * gr[rows, hs]).astype(BF16)

    for pair in pairs:
        ug[:, pair_cols(pair)] = proj(SEG_U, pair) * _silu(proj(SEG_GG, pair))

    for c in range(n_chunks):
        rows = _chunk_rows(c)
        for h in range(HEADS):
            hs = _head(h)
            s = _dot(wtril[h], vnb[rows, hs]) + gb_ref[:, hs]
            om[rows, WIDTH + h * HD:WIDTH + (h + 1) * HD] = (ug[rows, hs] * s).astype(BF16)

    half = x_ref.shape[0] // 2
    for rows in (slice(0, half), slice(half, 2 * half)):
        y = x_ref[rows, :] + _dot(om[rows, :], wout_ref[...])
        if final_norm:
            y = _rms_scale(y) * fg_ref[...]
        y_ref[rows, :] = y


N_PLAIN_OPERANDS = 18
N_SHARED = 2
OUT_PROMPT_STATES, OUT_SAMPLE_STATES = 1, 3


def _prompt_layer(x, win, wout, rope, qdec, kdec, mask, ws, gb, lng, lnb, fg, q, k, v, states, shared, *,
                  layer, chunks_per_step, final_norm):
    batch, seq, _ = x.shape
    depth, n_rows = states.shape[:2]
    aliased = shared is not None
    tok = chunks_per_step * CHUNK
    steps_per_row = seq // tok
    heads = n_rows * HEADS // (batch * steps_per_row * SUBLANES)
    head_blocks = HEADS // heads
    assert heads * head_blocks == HEADS and batch * steps_per_row == (n_rows // SUBLANES) * head_blocks
    tile = lambda b, t: ((b * steps_per_row + t) // head_blocks, (b * steps_per_row + t) % head_blocks)

    row_spec = pl.BlockSpec((None, tok, D_MODEL), lambda b, t: (b, t, 0))
    pos_spec = pl.BlockSpec((tok, HD), lambda b, t: (t, 0))
    tile_spec = pl.BlockSpec((SUBLANES, heads * HD), tile)
    state_spec = pl.BlockSpec((None, SUBLANES, heads, HD, HD), lambda b, t: (layer, *tile(b, t), 0, 0))
    all_layers_spec = pl.BlockSpec((depth, SUBLANES, heads, HD, HD), lambda b, t: (0, *tile(b, t), 0, 0))
    if aliased:
        pstate_spec = pl.BlockSpec((None, None, HEADS, HD, HD), lambda b, t: (layer, b, 0, 0, 0))
    else:
        pstate_spec = pl.BlockSpec((depth, None, HEADS, HD, HD), lambda b, t: (0, b, 0, 0, 0))
    gamma = jnp.asarray(np.broadcast_to(np.asarray(GAMMA)[:, None, None], (HEADS, 1, HD)), F32)
    bf16_rows = lambda width: pltpu.VMEM((tok, width), BF16)
    per_chunk_head = lambda dtype: pltpu.VMEM((chunks_per_step, HEADS, CHUNK, CHUNK), dtype)

    return pl.pallas_call(
        functools.partial(_prompt_kernel, layer=layer, final_norm=final_norm, aliased=aliased),
        grid=(batch, steps_per_row),
        in_specs=[
            row_spec,
            _resident_spec((D_MODEL, N_SEG * WIDTH)),
            _resident_spec((MIX, D_MODEL)),
            pos_spec, pos_spec,
            _const_spec((CHUNK, WIDTH)), _const_spec((CHUNK, WIDTH)),
            _const_spec((HEADS, CHUNK, CHUNK)),
            _layer_spec((HEADS, CHUNK, CHUNK), layer),
            _const_spec((CHUNK, WIDTH)),
            _const_spec((depth, WIDTH)), _const_spec((depth, WIDTH)),
            _const_spec((1, D_MODEL)),
            _const_spec((HEADS, 1, HD)),
            tile_spec, tile_spec, tile_spec,
            state_spec,
            *[pl.BlockSpec(memory_space=pl.ANY)] * (N_SHARED * aliased),
        ],
        out_specs=[row_spec, pstate_spec, tile_spec, state_spec if aliased else all_layers_spec],
        out_shape=[jax.ShapeDtypeStruct(x.shape, F32),
                   jax.ShapeDtypeStruct((depth, batch, HEADS, HD, HD), F32),
                   jax.ShapeDtypeStruct((n_rows, WIDTH), F32),
                   jax.ShapeDtypeStruct(states.shape, F32)],
        input_output_aliases=({N_PLAIN_OPERANDS: OUT_PROMPT_STATES, N_PLAIN_OPERANDS + 1: OUT_SAMPLE_STATES}
                              if aliased else {}),
        scratch_shapes=[
            bf16_rows(D_MODEL),
            bf16_rows(WIDTH),
            per_chunk_head(BF16),
            bf16_rows(WIDTH),
            pltpu.VMEM((tok, WIDTH), F32),
            pltpu.VMEM((tok, WIDTH), F32),
            bf16_rows(WIDTH),
            bf16_rows(MIX),
            pltpu.VMEM((HEADS, CHUNK, CHUNK), BF16),
            per_chunk_head(BF16),
            per_chunk_head(F32),
            per_chunk_head(BF16),
        ],
        compiler_params=pltpu.CompilerParams(
            dimension_semantics=("arbitrary", "arbitrary"),
            vmem_limit_bytes=VMEM_LIMIT_BYTES,
        ),
        name="prompt_layer",
    )(x, win, wout, *rope, qdec, kdec, mask, ws, gb, lng, lnb, fg, gamma, q, k, v, states,
      *(shared if aliased else ()))


N_WOUT_BLOCKS = MIX // WIDTH


def _finish_layer(h, scv_ref, opart_ref, gr_ref, m_ref, wout_ref):
    parts = []
    for hd in range(HEADS):
        hs = _head(hd)
        o = scv_ref[:, hs] + GAMMA[hd] * opart_ref[:, hs]
        parts.append((_rms_scale(o) * gr_ref[:, hs]).astype(BF16))
    om = jnp.concatenate(parts + [m_ref[...]], axis=1)
    return h + _dot(om, wout_ref[...])


def _sample_start_kernel(*refs, post, layer):
    h_ref, refs = refs[0], refs[1:]
    if post:
        post_refs, refs = refs[:5], refs[5:]
    ng_ref, win_ref, wout_ref, cos_ref, sin_ref, ws_ref, gmb_ref, lng_ref, lnb_ref = refs[:9]
    refs = refs[9:]
    if post:
        hout_ref, refs = refs[0], refs[1:]
    q_ref, k_ref, v_ref, scv_ref, gr_ref, m_ref, vn_ref, winb_ref, woutb_ref, xb, u_keep = refs
    j = pl.program_id(0)

    @pl.when(j == 0)
    def _tokens():
        h = h_ref[...]
        if post:
            h = _finish_layer(h, *post_refs)
            hout_ref[...] = h
        xb[...] = _rms_scale(h).astype(BF16)

    @pl.when(j < N_SEG)
    def _segment():
        gain = ng_ref[layer:layer + 1, :]
        tiles = [gain[:, i * LANES:(i + 1) * LANES] for i in range(D_MODEL // LANES)]
        pad = jnp.zeros((LANES - len(tiles), LANES), F32)
        gain_cols = jnp.concatenate(tiles + [pad], axis=0).T
        for i in range(D_MODEL // LANES):
            rows = slice(i * LANES, (i + 1) * LANES)
            winb_ref[rows, :] = (win_ref[rows, :] * gain_cols[:, i:i + 1]).astype(BF16)
        acc = _dot(xb[...], winb_ref[...])

        @pl.when(j == SEG_Q)
        def _():
            for hd in range(HEADS):
                q_ref[:, _head(hd)] = _rope(acc[:, _head(hd)], cos_ref[...], sin_ref[...])

        @pl.when(j == SEG_K)
        def _():
            for hd in range(HEADS):
                hs = _head(hd)
                k = _rope(acc[:, hs], cos_ref[...], sin_ref[...]) * K_SCALE
                k_ref[:, hs] = k
                qk = jnp.sum(q_ref[:, hs] * k, axis=-1, keepdims=True)
                scv_ref[:, hs] = jnp.broadcast_to(qk, k.shape)

        @pl.when(j == SEG_V)
        def _():
            v_ref[...] = acc
            scv_ref[...] = scv_ref[...] * acc

        @pl.when(j == SEG_GR)
        def _():
            gr_ref[...] = _silu(acc)

        @pl.when(j == SEG_U)
        def _():
            u_keep[...] = acc

        @pl.when(j == SEG_VG)
        def _():
            vn_ref[...] = _layernorm(acc, lng_ref[layer:layer + 1, :], lnb_ref[layer:layer + 1, :])

        @pl.when(j == SEG_GG)
        def _():
            for hd in range(HEADS):
                hs = _head(hd)
                s = vn_ref[:, hs] * ws_ref[hd, 0:1, 0:1] + gmb_ref[layer, hd:hd + 1, 0:1]
                m_ref[:, hs] = (u_keep[:, hs] * _silu(acc[:, hs]) * s).astype(BF16)

    @pl.when(j >= N_SEG)
    def _out_weight():
        woutb_ref[...] = wout_ref[...].astype(BF16)


def _sample_final_kernel(h_ref, scv_ref, opart_ref, gr_ref, m_ref, wout_ref, fg_ref, y_ref):
    y_ref[...] = _rms_scale(_finish_layer(h_ref[...], scv_ref, opart_ref, gr_ref, m_ref, wout_ref)) * fg_ref[...]


def _token_spec(n_rows):
    return pl.BlockSpec((n_rows, None, D_MODEL), lambda *_: (0, 0, 0))


def _sample_start(h, post_args, ng, w_in, w_out, rope, ws, gmb, lng, lnb, *, layer):
    n_rows, depth = h.shape[0], w_in.shape[0]
    post = post_args is not None
    full = _const_spec((n_rows, WIDTH))
    vecs = _const_spec((depth, WIDTH))
    seg = lambda j: jnp.minimum(j, N_SEG - 1)
    out_block = lambda j: jnp.clip(j - N_SEG, 0, N_WOUT_BLOCKS - 1)
    f32_rows = jax.ShapeDtypeStruct((n_rows, WIDTH), F32)
    args = [h] + (list(post_args) if post else []) + [ng, w_in, w_out, *rope, ws, gmb, lng, lnb]
    in_specs = [_token_spec(n_rows) if h.ndim == 3 else full]
    if post:
        in_specs += [full, full, full, full, _resident_spec((MIX, D_MODEL))]
    in_specs += [vecs,
                 pl.BlockSpec((None, D_MODEL, WIDTH), lambda j: (layer, 0, seg(j))),
                 pl.BlockSpec((None, WIDTH, D_MODEL), lambda j: (layer, out_block(j), 0)),
                 _const_spec((1, HD)), _const_spec((1, HD)),
                 _layer_spec((HEADS, CHUNK, CHUNK), layer), _const_spec((depth, HEADS, CHUNK)), vecs, vecs]
    out_shape = (([f32_rows] if post else []) + [f32_rows] * 5
                 + [jax.ShapeDtypeStruct((n_rows, WIDTH), BF16), f32_rows,
                    jax.ShapeDtypeStruct((D_MODEL, N_SEG * WIDTH), BF16), jax.ShapeDtypeStruct((MIX, D_MODEL), BF16)])
    out_specs = [full] * (len(out_shape) - 2) + [pl.BlockSpec((D_MODEL, WIDTH), lambda j: (0, seg(j))),
                                                 pl.BlockSpec((WIDTH, D_MODEL), lambda j: (out_block(j), 0))]
    return pl.pallas_call(
        functools.partial(_sample_start_kernel, post=post, layer=layer),
        grid=(N_SEG + N_WOUT_BLOCKS,),
        in_specs=in_specs,
        out_specs=out_specs,
        out_shape=out_shape,
        scratch_shapes=[pltpu.VMEM((n_rows, D_MODEL), BF16),
                        pltpu.VMEM((n_rows, WIDTH), F32)],
        compiler_params=pltpu.CompilerParams(
            dimension_semantics=("arbitrary",),
            vmem_limit_bytes=VMEM_LIMIT_BYTES,
        ),
        name="sample_start",
    )(*args)


def _sample_final(h, post_args, fg):
    n_rows = h.shape[0]
    full = _const_spec((n_rows, WIDTH))
    return pl.pallas_call(
        _sample_final_kernel,
        grid=(1,),
        in_specs=[full, full, full, full, full, _const_spec((MIX, D_MODEL)), _const_spec((1, D_MODEL))],
        out_specs=_token_spec(n_rows),
        out_shape=jax.ShapeDtypeStruct((n_rows, 1, D_MODEL), F32),
        compiler_params=pltpu.CompilerParams(
            dimension_semantics=("arbitrary",),
            vmem_limit_bytes=VMEM_LIMIT_BYTES,
        ),
        name="sample_final",
    )(h, *post_args, fg)


def _rope_tables(pos):
    inv = ROPE_BASE ** (-np.arange(0, HD, 2, dtype=np.float64) / HD)
    ang = np.asarray(pos, np.float64)[:, None] * inv[None, :]
    c, s = np.cos(ang), np.sin(ang)
    return (jnp.asarray(np.concatenate([c, c], axis=-1), F32),
            jnp.asarray(np.concatenate([-s, s], axis=-1), F32))


def _retention_tables():
    lg = np.log(np.asarray(GAMMA, np.float64))
    idx = np.arange(CHUNK, dtype=np.float64)
    causal = idx[:, None] >= idx[None, :]
    mask = np.where(causal[None], np.exp(-lg * CHUNK)[:, None, None], 0.0)
    q_dec = np.exp(lg[None, :] * (idx[:, None] + 1.0))
    k_dec = np.exp(lg[None, :] * (CHUNK - 1.0 - idx[:, None])) * K_SCALE
    per_lane = lambda a: np.repeat(a, HD, axis=1)
    return jnp.asarray(mask, F32), jnp.asarray(per_lane(q_dec), F32), jnp.asarray(per_lane(k_dec), F32)


def kernel(x_prompt, x_sample, state_ret, norm_g, w_in, w_out, gm_ws, gm_b, gm_ln_g, gm_ln_b, final_g):
    depth = w_in.shape[0]
    batch, seq, _ = x_prompt.shape
    n_rows = x_sample.shape[0]
    assert x_sample.shape[1] == 1

    rope_p = _rope_tables(np.arange(seq))
    rope_s = _rope_tables(PAST_LEN + np.arange(1))
    mask, q_dec, k_dec = _retention_tables()
    fg = final_g.reshape(1, D_MODEL)
    start = functools.partial(_sample_start, ng=norm_g, w_in=w_in, w_out=w_out, rope=rope_s, ws=gm_ws, gmb=gm_b,
                              lng=gm_ln_g, lnb=gm_ln_b)

    h_p, h_s, shared, vns = x_prompt, x_sample, None, []
    q, k, v, scv, gr, m, vn, win_b, wout_b = start(h_s, None, layer=0)
    for l in range(depth):
        vns.append(vn)
        gb = jnp.repeat(gm_b[l].T, HD, axis=1)
        h_p, states_p, opart, states_s = _prompt_layer(
            h_p, win_b, wout_b, rope_p, q_dec, k_dec, mask, gm_ws, gb, gm_ln_g, gm_ln_b, fg,
            q, k, v, state_ret, shared, layer=l, chunks_per_step=CHUNKS_PER_STEP, final_norm=l == depth - 1)
        shared = (states_p, states_s)
        post_args = (scv, opart, gr, m, wout_b)
        if l == depth - 1:
            y_s = _sample_final(h_s, post_args, fg)
        else:
            h_s, q, k, v, scv, gr, m, vn, win_b, wout_b = start(h_s, post_args, layer=l + 1)

    return h_p, y_s, states_p, states_s, jnp.stack(vns).reshape(depth, n_rows, 1, HEADS, HD)
```

```python
import functools

import jax
import jax.numpy as jnp
import numpy as np
from jax import lax
from jax.experimental import pallas as pl
from jax.experimental.pallas import tpu as pltpu

F32 = jnp.float32
BF16 = jnp.bfloat16

D_MODEL = 1024
HEADS = 8
HD = 128
LANES, SUBLANES = 128, 8
WIDTH = HEADS * HD
N_SEG = 7
MIX = 2 * WIDTH
CHUNK = 128
PAST_LEN = 16384
ROPE_BASE = 10000.0
EPS = 1e-6
SEG_Q, SEG_K, SEG_V, SEG_GR, SEG_U, SEG_VG, SEG_GG = range(N_SEG)

GAMMA = tuple(1.0 - 2.0 ** (-5.0 - h) for h in range(HEADS))
CHUNK_DECAY = tuple(g ** CHUNK for g in GAMMA)
K_SCALE = HD ** -0.5
MASK_SCALE = tuple(g ** -CHUNK for g in GAMMA)

CHUNKS_PER_STEP = 4
V7X_VMEM_BYTES = 64 * 1024 * 1024
VMEM_LIMIT_BYTES = V7X_VMEM_BYTES - 1024 * 1024


def _silu(x):
    return x * (1.0 / (1.0 + jnp.exp(-x)))


def _rms_scale(x):
    return x * lax.rsqrt(jnp.mean(x * x, axis=-1, keepdims=True) + EPS)


def _layernorm(x, g, b):
    mu = jnp.mean(x, axis=-1, keepdims=True)
    xc = x - mu
    var = jnp.mean(xc * xc, axis=-1, keepdims=True)
    return xc * lax.rsqrt(var + EPS) * g + b


def _rope(x, cos, sin):
    return x * cos + pltpu.roll(x, HD // 2, 1) * sin


def _head(h):
    return slice(h * HD, (h + 1) * HD)


def _dot(a, b):
    return jnp.dot(a, b, preferred_element_type=F32)


def _const_spec(shape):
    zeros = (0,) * len(shape)
    return pl.BlockSpec(shape, lambda *_: zeros)


def _layer_spec(shape, layer, **kwargs):
    index = (layer,) + (0,) * len(shape)
    return pl.BlockSpec((None,) + shape, lambda *_: index, **kwargs)


def _resident_spec(shape):
    zeros = (0,) * len(shape)
    return pl.BlockSpec(shape, lambda *_: zeros, pipeline_mode=pl.Buffered(1))


def _gain_columns(gain):
    tiles = [gain[:, i * LANES:(i + 1) * LANES] for i in range(D_MODEL // LANES)]
    pad = jnp.zeros((LANES - len(tiles), LANES), F32)
    return jnp.concatenate(tiles + [pad], axis=0).T


def _chunk_rows(c):
    return slice(c * CHUNK, (c + 1) * CHUNK)


def _sample_state_step(q_ref, k_ref, v_ref, gamma_ref, s0_ref, write_new, opart_ref, head0):
    heads = opart_ref.shape[1] // HD
    tiles = [r[:, _head(hh)] for r in (k_ref, q_ref) for hh in range(heads)]
    pad = jnp.zeros((LANES - len(tiles) * SUBLANES, HD), F32)
    cols = jnp.concatenate(tiles + [pad], axis=0).T
    for hh in range(heads):
        for j in range(SUBLANES):
            s_old = s0_ref[j, hh]
            k_lane = hh * SUBLANES + j
            q_lane = heads * SUBLANES + k_lane
            k_col = jnp.broadcast_to(cols[:, k_lane:k_lane + 1], (HD, HD))
            q_col = jnp.broadcast_to(cols[:, q_lane:q_lane + 1], (HD, HD))
            write_new(j, hh, s_old * gamma_ref[head0 + hh] + k_col * v_ref[j:j + 1, _head(hh)])
            opart_ref[j:j + 1, _head(hh)] = jnp.sum(q_col * s_old, axis=0, keepdims=True)


def _prompt_kernel(x_ref, win_ref, wout_ref, cos_ref, sin_ref, qdec_ref, kdec_ref,
                   ws_ref, gb_ref, lng_ref, lnb_ref, fg_ref, gamma_ref, q_ref, k_ref, v_ref, s0_ref, *refs,
                   layer, final_norm, aliased, convert_next):
    if convert_next:
        ng_ref, wnext_ref = refs[:2]
    refs = refs[2 * convert_next + N_SHARED * aliased:]
    y_ref, sall_ref, opart_ref, snew_ref = refs[:4]
    if convert_next:
        wnext_b_ref = refs[4]
    xb, qdb, kdt, vb, gr, ug, vnb, om, wtril, sb, kvs, scb, *gain_rows = refs[4 + convert_next:]
    s_ref = sall_ref if aliased else sall_ref.at[layer]
    lng, lnb = (r[layer:layer + 1, :] for r in (lng_ref, lnb_ref))

    t = pl.program_id(1)
    n_chunks = x_ref.shape[0] // CHUNK

    row = lax.broadcasted_iota(jnp.int32, (CHUNK, CHUNK), 0)
    col = lax.broadcasted_iota(jnp.int32, (CHUNK, CHUNK), 1)
    causal = row >= col

    @pl.when(t == 0)
    def _start_of_row():
        sall_ref[...] = jnp.zeros(sall_ref.shape, F32)
        for h in range(HEADS):
            wtril[h] = jnp.where(causal, ws_ref[h], 0.0).astype(BF16)

    def proj(seg, pair=None):
        lo, width = (seg * WIDTH, WIDTH) if pair is None else (seg * WIDTH + pair * 2 * HD, 2 * HD)
        return _dot(xb[...], win_ref[:, lo:lo + width])

    pairs = range(HEADS // 2)
    pair_cols = lambda pair: slice(pair * 2 * HD, (pair + 1) * 2 * HD)

    xb[...] = _rms_scale(x_ref[...]).astype(BF16)

    step = pl.program_id(0) * pl.num_programs(1) + t
    head_blocks = WIDTH // opart_ref.shape[1]
    head0 = lax.rem(step, head_blocks) * (HEADS // head_blocks)
    if aliased:
        def write_new(j, hh, s_new):
            snew_ref[j, hh] = s_new
    else:
        def write_new(j, hh, s_new):
            for l in range(snew_ref.shape[0]):
                snew_ref[l, j, hh] = s_new if l == layer else jnp.zeros_like(s_new)
    _sample_state_step(q_ref, k_ref, v_ref, gamma_ref, s0_ref, write_new, opart_ref, head0)

    if convert_next:
        gain_rows, = gain_rows
        n = wnext_ref.shape[0]
        per_group = LANES // n
        lane = lax.broadcasted_iota(jnp.int32, (LANES, LANES), 1)
        gain_cols = _gain_columns(ng_ref[layer + 1:layer + 2, :])
        group_gain = jnp.sum(jnp.where(lane == step // per_group, gain_cols, 0.0), axis=1, keepdims=True)
        gain_rows[...] = jnp.broadcast_to(group_gain, (LANES, LANES))
        g = gain_rows[pl.ds(pl.multiple_of(lax.rem(step, per_group) * n, n), n), 0:1]
        wnext_b_ref[...] = (wnext_ref[...] * g).astype(BF16)

    cos, sin = cos_ref[...], sin_ref[...]
    for pair in pairs:
        acc = proj(SEG_Q, pair)
        for i in range(2):
            h = 2 * pair + i
            r = _rope(acc[:, _head(i)], cos, sin)
            for c in range(n_chunks):
                rows = _chunk_rows(c)
                qdb[rows, _head(h)] = (r[rows] * qdec_ref[:, _head(h)]).astype(BF16)

    for pair in pairs:
        acc = proj(SEG_K, pair)
        for i in range(2):
            h = 2 * pair + i
            r = _rope(acc[:, _head(i)], cos, sin)
            for c in range(n_chunks):
                kd = r[_chunk_rows(c)] * kdec_ref[:, _head(h)]
                kdt[c, h] = kd.T.astype(BF16)

    for pair in pairs:
        vb[:, pair_cols(pair)] = proj(SEG_V, pair).astype(BF16)

    for c in range(n_chunks):
        rows = _chunk_rows(c)
        for h in range(HEADS):
            hs = _head(h)
            scb[c, h] = jnp.where(causal, _dot(qdb[rows, hs], kdt[c, h]) * MASK_SCALE[h], 0.0).astype(BF16)
            kvs[c, h] = _dot(kdt[c, h], vb[rows, hs])

    for pair in pairs:
        gr[:, pair_cols(pair)] = _silu(proj(SEG_GR, pair))
    vnb[...] = _layernorm(proj(SEG_VG), lng, lnb).astype(BF16)

    for h in range(HEADS):
        s = s_ref[h]
        for c in range(n_chunks):
            sb[c, h] = s.astype(BF16)
            s = s * CHUNK_DECAY[h] + kvs[c, h]
        s_ref[h] = s

    for c in range(n_chunks):
        rows = _chunk_rows(c)
        for h in range(HEADS):
            hs = _head(h)
            lhs = jnp.concatenate([scb[c, h], qdb[rows, hs]], axis=1)
            rhs = jnp.concatenate([vb[rows, hs], sb[c, h]], axis=0)
            om[rows, hs] = (_rms_scale(_dot(lhs, rhs)) * gr[rows, hs]).astype(BF16)

    for pair in pairs:
        ug[:, pair_cols(pair)] = proj(SEG_U, pair) * _silu(proj(SEG_GG, pair))

    for c in range(n_chunks):
        rows = _chunk_rows(c)
        for h in range(HEADS):
            hs = _head(h)
            s = _dot(wtril[h], vnb[rows, hs]) + gb_ref[:, hs]
            om[rows, WIDTH + h * HD:WIDTH + (h + 1) * HD] = (ug[rows, hs] * s).astype(BF16)

    half = x_ref.shape[0] // 2
    for rows in (slice(0, half), slice(half, 2 * half)):
        y = x_ref[rows, :] + _dot(om[rows, :], wout_ref[...])
        if final_norm:
            y = _rms_scale(y) * fg_ref[...]
        y_ref[rows, :] = y


N_SHARED = 2
OUT_PROMPT_STATES, OUT_SAMPLE_STATES = 1, 3


def _prompt_layer(x, win, wout, rope, qdec, kdec, ws, gb, lng, lnb, fg, q, k, v, states, next_weights, shared, *,
                  layer, chunks_per_step, final_norm):
    batch, seq, _ = x.shape
    depth, n_rows = states.shape[:2]
    aliased = shared is not None
    convert_next = next_weights is not None
    tok = chunks_per_step * CHUNK
    steps_per_row = seq // tok
    rows_next = D_MODEL // (batch * steps_per_row)
    assert rows_next * batch * steps_per_row == D_MODEL and LANES % rows_next == 0 and rows_next % (2 * SUBLANES) == 0
    heads = n_rows * HEADS // (batch * steps_per_row * SUBLANES)
    head_blocks = HEADS // heads
    assert heads * head_blocks == HEADS and batch * steps_per_row == (n_rows // SUBLANES) * head_blocks
    step = lambda b, t: b * steps_per_row + t
    tile = lambda b, t: (step(b, t) // head_blocks, step(b, t) % head_blocks)

    row_spec = pl.BlockSpec((None, tok, D_MODEL), lambda b, t: (b, t, 0))
    pos_spec = pl.BlockSpec((tok, HD), lambda b, t: (t, 0))
    tile_spec = pl.BlockSpec((SUBLANES, heads * HD), tile)
    state_spec = pl.BlockSpec((None, SUBLANES, heads, HD, HD), lambda b, t: (layer, *tile(b, t), 0, 0))
    all_layers_spec = pl.BlockSpec((depth, SUBLANES, heads, HD, HD), lambda b, t: (0, *tile(b, t), 0, 0))
    if aliased:
        pstate_spec = pl.BlockSpec((None, None, HEADS, HD, HD), lambda b, t: (layer, b, 0, 0, 0))
    else:
        pstate_spec = pl.BlockSpec((depth, None, HEADS, HD, HD), lambda b, t: (0, b, 0, 0, 0))
    gamma = jnp.asarray(np.broadcast_to(np.asarray(GAMMA)[:, None, None], (HEADS, 1, HD)), F32)
    plain = [x, win, wout, *rope, qdec, kdec, ws, gb, lng, lnb, fg, gamma, q, k, v, states]
    plain += list(next_weights) if convert_next else []
    bf16_rows = lambda width: pltpu.VMEM((tok, width), BF16)
    per_chunk_head = lambda dtype: pltpu.VMEM((chunks_per_step, HEADS, CHUNK, CHUNK), dtype)

    return pl.pallas_call(
        functools.partial(_prompt_kernel, layer=layer, final_norm=final_norm, aliased=aliased,
                          convert_next=convert_next),
        grid=(batch, steps_per_row),
        in_specs=[
            row_spec,
            _resident_spec((D_MODEL, N_SEG * WIDTH)),
            _resident_spec((MIX, D_MODEL)),
            pos_spec, pos_spec,
            _const_spec((CHUNK, WIDTH)), _const_spec((CHUNK, WIDTH)),
            _layer_spec((HEADS, CHUNK, CHUNK), layer),
            _const_spec((CHUNK, WIDTH)),
            _const_spec((depth, WIDTH)), _const_spec((depth, WIDTH)),
            _const_spec((1, D_MODEL)),
            _const_spec((HEADS, 1, HD)),
            tile_spec, tile_spec, tile_spec,
            state_spec,
            *([_const_spec((depth, D_MODEL)),
               pl.BlockSpec((None, rows_next, N_SEG * WIDTH), lambda b, t: (layer + 1, step(b, t), 0))]
              if convert_next else []),
            *[pl.BlockSpec(memory_space=pl.ANY)] * (N_SHARED * aliased),
        ],
        out_specs=[row_spec, pstate_spec, tile_spec, state_spec if aliased else all_layers_spec,
                   *([pl.BlockSpec((rows_next, N_SEG * WIDTH), lambda b, t: (step(b, t), 0))] if convert_next else [])],
        out_shape=[jax.ShapeDtypeStruct(x.shape, F32),
                   jax.ShapeDtypeStruct((depth, batch, HEADS, HD, HD), F32),
                   jax.ShapeDtypeStruct((n_rows, WIDTH), F32),
                   jax.ShapeDtypeStruct(states.shape, F32),
                   *([jax.ShapeDtypeStruct((D_MODEL, N_SEG * WIDTH), BF16)] if convert_next else [])],
        input_output_aliases=({len(plain): OUT_PROMPT_STATES, len(plain) + 1: OUT_SAMPLE_STATES}
                              if aliased else {}),
        scratch_shapes=[
            bf16_rows(D_MODEL),
            bf16_rows(WIDTH),
            per_chunk_head(BF16),
            bf16_rows(WIDTH),
            pltpu.VMEM((tok, WIDTH), F32),
            pltpu.VMEM((tok, WIDTH), F32),
            bf16_rows(WIDTH),
            bf16_rows(MIX),
            pltpu.VMEM((HEADS, CHUNK, CHUNK), BF16),
            per_chunk_head(BF16),
            per_chunk_head(F32),
            per_chunk_head(BF16),
            *([pltpu.VMEM((LANES, LANES), F32)] if convert_next else []),
        ],
        compiler_params=pltpu.CompilerParams(
            dimension_semantics=("arbitrary", "arbitrary"),
            vmem_limit_bytes=VMEM_LIMIT_BYTES,
        ),
        name="prompt_layer",
    )(*plain, *(shared if aliased else ()))


N_WOUT_BLOCKS = MIX // WIDTH


def _finish_layer(h, scv_ref, opart_ref, gr_ref, m_ref, wout_ref):
    parts = []
    for hd in range(HEADS):
        hs = _head(hd)
        o = scv_ref[:, hs] + GAMMA[hd] * opart_ref[:, hs]
        parts.append((_rms_scale(o) * gr_ref[:, hs]).astype(BF16))
    om = jnp.concatenate(parts + [m_ref[...]], axis=1)
    return h + _dot(om, wout_ref[...])


def _sample_start_kernel(*refs, post, layer, convert_in):
    h_ref, refs = refs[0], refs[1:]
    if post:
        post_refs, refs = refs[:5], refs[5:]
    ng_ref, win_ref, wout_ref, cos_ref, sin_ref, ws_ref, gmb_ref, lng_ref, lnb_ref = refs[:9]
    refs = refs[9:]
    if post:
        hout_ref, refs = refs[0], refs[1:]
    q_ref, k_ref, v_ref, scv_ref, gr_ref, m_ref, vn_ref = refs[:7]
    winb_ref = refs[7] if convert_in else win_ref
    woutb_ref, xb, u_keep = refs[7 + convert_in:]
    j = pl.program_id(0)

    @pl.when(j == 0)
    def _tokens():
        h = h_ref[...]
        if post:
            h = _finish_layer(h, *post_refs)
            hout_ref[...] = h
        xb[...] = _rms_scale(h).astype(BF16)

    @pl.when(j < N_SEG)
    def _segment():
        if convert_in:
            gain_cols = _gain_columns(ng_ref[layer:layer + 1, :])
            for i in range(D_MODEL // LANES):
                rows = slice(i * LANES, (i + 1) * LANES)
                winb_ref[rows, :] = (win_ref[rows, :] * gain_cols[:, i:i + 1]).astype(BF16)
        acc = _dot(xb[...], winb_ref[...])

        @pl.when(j == SEG_Q)
        def _():
            for hd in range(HEADS):
                q_ref[:, _head(hd)] = _rope(acc[:, _head(hd)], cos_ref[...], sin_ref[...])

        @pl.when(j == SEG_K)
        def _():
            for hd in range(HEADS):
                hs = _head(hd)
                k = _rope(acc[:, hs], cos_ref[...], sin_ref[...]) * K_SCALE
                k_ref[:, hs] = k
                qk = jnp.sum(q_ref[:, hs] * k, axis=-1, keepdims=True)
                scv_ref[:, hs] = jnp.broadcast_to(qk, k.shape)

        @pl.when(j == SEG_V)
        def _():
            v_ref[...] = acc
            scv_ref[...] = scv_ref[...] * acc

        @pl.when(j == SEG_GR)
        def _():
            gr_ref[...] = _silu(acc)

        @pl.when(j == SEG_U)
        def _():
            u_keep[...] = acc

        @pl.when(j == SEG_VG)
        def _():
            vn_ref[...] = _layernorm(acc, lng_ref[layer:layer + 1, :], lnb_ref[layer:layer + 1, :])

        @pl.when(j == SEG_GG)
        def _():
            for hd in range(HEADS):
                hs = _head(hd)
                s = vn_ref[:, hs] * ws_ref[hd, 0:1, 0:1] + gmb_ref[layer, hd:hd + 1, 0:1]
                m_ref[:, hs] = (u_keep[:, hs] * _silu(acc[:, hs]) * s).astype(BF16)

    @pl.when(j >= N_SEG)
    def _out_weight():
        woutb_ref[...] = wout_ref[...].astype(BF16)


def _sample_final_kernel(h_ref, scv_ref, opart_ref, gr_ref, m_ref, wout_ref, fg_ref, y_ref):
    y_ref[...] = _rms_scale(_finish_layer(h_ref[...], scv_ref, opart_ref, gr_ref, m_ref, wout_ref)) * fg_ref[...]


def _token_spec(n_rows):
    return pl.BlockSpec((n_rows, None, D_MODEL), lambda *_: (0, 0, 0))


def _sample_start(h, post_args, win_b, ng, w_in, w_out, rope, ws, gmb, lng, lnb, *, layer):
    n_rows, depth = h.shape[0], w_in.shape[0]
    post = post_args is not None
    convert_in = win_b is None
    full = _const_spec((n_rows, WIDTH))
    vecs = _const_spec((depth, WIDTH))
    seg = lambda j: jnp.minimum(j, N_SEG - 1)
    out_block = lambda j: jnp.clip(j - N_SEG, 0, N_WOUT_BLOCKS - 1)
    f32_rows = jax.ShapeDtypeStruct((n_rows, WIDTH), F32)
    args = [h] + (list(post_args) if post else [])
    args += [ng, w_in if convert_in else win_b, w_out, *rope, ws, gmb, lng, lnb]
    in_specs = [_token_spec(n_rows) if h.ndim == 3 else full]
    if post:
        in_specs += [full, full, full, full, _resident_spec((MIX, D_MODEL))]
    in_specs += [vecs,
                 (pl.BlockSpec((None, D_MODEL, WIDTH), lambda j: (layer, 0, seg(j))) if convert_in else
                  pl.BlockSpec((D_MODEL, WIDTH), lambda j: (0, seg(j)))),
                 pl.BlockSpec((None, WIDTH, D_MODEL), lambda j: (layer, out_block(j), 0)),
                 _const_spec((1, HD)), _const_spec((1, HD)),
                 _layer_spec((HEADS, CHUNK, CHUNK), layer), _const_spec((depth, HEADS, CHUNK)), vecs, vecs]
    out_shape = (([f32_rows] if post else []) + [f32_rows] * 5
                 + [jax.ShapeDtypeStruct((n_rows, WIDTH), BF16), f32_rows]
                 + ([jax.ShapeDtypeStruct((D_MODEL, N_SEG * WIDTH), BF16)] if convert_in else [])
                 + [jax.ShapeDtypeStruct((MIX, D_MODEL), BF16)])
    out_specs = ([full] * (len(out_shape) - 1 - convert_in)
                 + ([pl.BlockSpec((D_MODEL, WIDTH), lambda j: (0, seg(j)))] if convert_in else [])
                 + [pl.BlockSpec((WIDTH, D_MODEL), lambda j: (out_block(j), 0))])
    return pl.pallas_call(
        functools.partial(_sample_start_kernel, post=post, layer=layer, convert_in=convert_in),
        grid=(N_SEG + N_WOUT_BLOCKS,),
        in_specs=in_specs,
        out_specs=out_specs,
        out_shape=out_shape,
        scratch_shapes=[pltpu.VMEM((n_rows, D_MODEL), BF16),
                        pltpu.VMEM((n_rows, WIDTH), F32)],
        compiler_params=pltpu.CompilerParams(
            dimension_semantics=("arbitrary",),
            vmem_limit_bytes=VMEM_LIMIT_BYTES,
        ),
        name="sample_start",
    )(*args)


def _sample_final(h, post_args, fg):
    n_rows = h.shape[0]
    full = _const_spec((n_rows, WIDTH))
    return pl.pallas_call(
        _sample_final_kernel,
        grid=(1,),
        in_specs=[full, full, full, full, full, _const_spec((MIX, D_MODEL)), _const_spec((1, D_MODEL))],
        out_specs=_token_spec(n_rows),
        out_shape=jax.ShapeDtypeStruct((n_rows, 1, D_MODEL), F32),
        compiler_params=pltpu.CompilerParams(
            dimension_semantics=("arbitrary",),
            vmem_limit_bytes=VMEM_LIMIT_BYTES,
        ),
        name="sample_final",
    )(h, *post_args, fg)


def _rope_tables(pos):
    inv = ROPE_BASE ** (-np.arange(0, HD, 2, dtype=np.float64) / HD)
    ang = np.asarray(pos, np.float64)[:, None] * inv[None, :]
    c, s = np.cos(ang), np.sin(ang)
    return (jnp.asarray(np.concatenate([c, c], axis=-1), F32),
            jnp.asarray(np.concatenate([-s, s], axis=-1), F32))


def _retention_tables():
    lg = np.log(np.asarray(GAMMA, np.float64))
    idx = np.arange(CHUNK, dtype=np.float64)
    q_dec = np.exp(lg[None, :] * (idx[:, None] + 1.0))
    k_dec = np.exp(lg[None, :] * (CHUNK - 1.0 - idx[:, None])) * K_SCALE
    per_lane = lambda a: np.repeat(a, HD, axis=1)
    return jnp.asarray(per_lane(q_dec), F32), jnp.asarray(per_lane(k_dec), F32)


def kernel(x_prompt, x_sample, state_ret, norm_g, w_in, w_out, gm_ws, gm_b, gm_ln_g, gm_ln_b, final_g):
    depth = w_in.shape[0]
    batch, seq, _ = x_prompt.shape
    n_rows = x_sample.shape[0]
    assert x_sample.shape[1] == 1

    rope_p = _rope_tables(np.arange(seq))
    rope_s = _rope_tables(PAST_LEN + np.arange(1))
    q_dec, k_dec = _retention_tables()
    fg = final_g.reshape(1, D_MODEL)
    start = functools.partial(_sample_start, ng=norm_g, w_in=w_in, w_out=w_out, rope=rope_s, ws=gm_ws, gmb=gm_b,
                              lng=gm_ln_g, lnb=gm_ln_b)

    h_p, h_s, shared, vns = x_prompt, x_sample, None, []
    q, k, v, scv, gr, m, vn, win_b, wout_b = start(h_s, None, None, layer=0)
    for l in range(depth):
        last = l == depth - 1
        vns.append(vn)
        gb = jnp.repeat(gm_b[l].T, HD, axis=1)
        h_p, states_p, opart, states_s, *win_next = _prompt_layer(
            h_p, win_b, wout_b, rope_p, q_dec, k_dec, gm_ws, gb, gm_ln_g, gm_ln_b, fg,
            q, k, v, state_ret, None if last else (norm_g, w_in), shared,
            layer=l, chunks_per_step=CHUNKS_PER_STEP, final_norm=last)
        shared = (states_p, states_s)
        post_args = (scv, opart, gr, m, wout_b)
        if last:
            y_s = _sample_final(h_s, post_args, fg)
        else:
            win_b, = win_next
            h_s, q, k, v, scv, gr, m, vn, wout_b = start(h_s, post_args, win_b, layer=l + 1)

    return h_p, y_s, states_p, states_s, jnp.stack(vns).reshape(depth, n_rows, 1, HEADS, HD)
```

```python
import functools

import jax
import jax.numpy as jnp
import numpy as np
from jax import lax
from jax.experimental import pallas as pl
from jax.experimental.pallas import tpu as pltpu

F32 = jnp.float32
BF16 = jnp.bfloat16

D_MODEL = 1024
HEADS = 8
HD = 128
LANES, SUBLANES = 128, 8
WIDTH = HEADS * HD
N_SEG = 7
MIX = 2 * WIDTH
CHUNK = 128
PAST_LEN = 16384
ROPE_BASE = 10000.0
EPS = 1e-6
SEG_Q, SEG_K, SEG_V, SEG_GR, SEG_U, SEG_VG, SEG_GG = range(N_SEG)

GAMMA = tuple(1.0 - 2.0 ** (-5.0 - h) for h in range(HEADS))
CHUNK_DECAY = tuple(g ** CHUNK for g in GAMMA)
K_SCALE = HD ** -0.5

CHUNKS_PER_STEP = 4
V7X_VMEM_BYTES = 64 * 1024 * 1024
VMEM_LIMIT_BYTES = V7X_VMEM_BYTES - 2 * 1024 * 1024


def _silu(x):
    return x * (1.0 / (1.0 + jnp.exp(-x)))


def _rms_scale(x):
    return x * lax.rsqrt(jnp.mean(x * x, axis=-1, keepdims=True) + EPS)


def _layernorm(x, g, b):
    mu = jnp.mean(x, axis=-1, keepdims=True)
    xc = x - mu
    var = jnp.mean(xc * xc, axis=-1, keepdims=True)
    return xc * lax.rsqrt(var + EPS) * g + b


def _rope(x, cos, sin):
    return x * cos + pltpu.roll(x, HD // 2, 1) * sin


def _head(h):
    return slice(h * HD, (h + 1) * HD)


def _dot(a, b):
    return jnp.dot(a, b, preferred_element_type=F32)


def _const_spec(shape):
    zeros = (0,) * len(shape)
    return pl.BlockSpec(shape, lambda *_: zeros)


def _layer_spec(shape, layer, **kwargs):
    index = (layer,) + (0,) * len(shape)
    return pl.BlockSpec((None,) + shape, lambda *_: index, **kwargs)


def _resident_spec(shape):
    zeros = (0,) * len(shape)
    return pl.BlockSpec(shape, lambda *_: zeros, pipeline_mode=pl.Buffered(1))


def _chunk_rows(c):
    return slice(c * CHUNK, (c + 1) * CHUNK)


def _sample_state_step(q_ref, k_ref, v_ref, gamma_ref, s0_ref, write_new, opart_ref, head0):
    heads = opart_ref.shape[1] // HD
    tiles = [r[:, _head(hh)] for r in (k_ref, q_ref) for hh in range(heads)]
    pad = jnp.zeros((LANES - len(tiles) * SUBLANES, HD), F32)
    cols = jnp.concatenate(tiles + [pad], axis=0).T
    for hh in range(heads):
        for j in range(SUBLANES):
            s_old = s0_ref[j, hh]
            k_lane = hh * SUBLANES + j
            q_lane = heads * SUBLANES + k_lane
            k_col = jnp.broadcast_to(cols[:, k_lane:k_lane + 1], (HD, HD))
            q_col = jnp.broadcast_to(cols[:, q_lane:q_lane + 1], (HD, HD))
            write_new(j, hh, s_old * gamma_ref[head0 + hh] + k_col * v_ref[j:j + 1, _head(hh)])
            opart_ref[j:j + 1, _head(hh)] = jnp.sum(q_col * s_old, axis=0, keepdims=True)


def _prompt_kernel(x_ref, win_ref, wout_ref, cos_ref, sin_ref, qdec_ref, kdec_ref, mask_ref,
                   ws_ref, gb_ref, lng_ref, lnb_ref, fg_ref, gamma_ref, q_ref, k_ref, v_ref, s0_ref, *refs,
                   layer, final_norm, aliased):
    refs = refs[N_SHARED * aliased:]
    y_ref, sall_ref, opart_ref, snew_ref = refs[:4]
    xb, qdb, kdt, vb, gr, ug, vnb, om, wtril, sb, kvs, scb = refs[4:]
    s_ref = sall_ref if aliased else sall_ref.at[layer]
    lng, lnb = (r[layer:layer + 1, :] for r in (lng_ref, lnb_ref))

    t = pl.program_id(1)
    n_chunks = x_ref.shape[0] // CHUNK

    @pl.when(t == 0)
    def _start_of_row():
        sall_ref[...] = jnp.zeros(sall_ref.shape, F32)
        row = lax.broadcasted_iota(jnp.int32, (CHUNK, CHUNK), 0)
        col = lax.broadcasted_iota(jnp.int32, (CHUNK, CHUNK), 1)
        for h in range(HEADS):
            wtril[h] = jnp.where(row >= col, ws_ref[h], 0.0).astype(BF16)

    def proj(seg, pair=None):
        lo, width = (seg * WIDTH, WIDTH) if pair is None else (seg * WIDTH + pair * 2 * HD, 2 * HD)
        return _dot(xb[...], win_ref[:, lo:lo + width])

    pairs = range(HEADS // 2)
    pair_cols = lambda pair: slice(pair * 2 * HD, (pair + 1) * 2 * HD)

    xb[...] = _rms_scale(x_ref[...]).astype(BF16)

    head_blocks = WIDTH // opart_ref.shape[1]
    head0 = lax.rem(pl.program_id(0) * pl.num_programs(1) + t, head_blocks) * (HEADS // head_blocks)
    if aliased:
        def write_new(j, hh, s_new):
            snew_ref[j, hh] = s_new
    else:
        def write_new(j, hh, s_new):
            for l in range(snew_ref.shape[0]):
                snew_ref[l, j, hh] = s_new if l == layer else jnp.zeros_like(s_new)
    _sample_state_step(q_ref, k_ref, v_ref, gamma_ref, s0_ref, write_new, opart_ref, head0)

    cos, sin = cos_ref[...], sin_ref[...]
    for pair in pairs:
        acc = proj(SEG_Q, pair)
        for i in range(2):
            h = 2 * pair + i
            r = _rope(acc[:, _head(i)], cos, sin)
            for c in range(n_chunks):
                rows = _chunk_rows(c)
                qdb[rows, _head(h)] = (r[rows] * qdec_ref[:, _head(h)]).astype(BF16)

    for pair in pairs:
        acc = proj(SEG_K, pair)
        for i in range(2):
            h = 2 * pair + i
            r = _rope(acc[:, _head(i)], cos, sin)
            for c in range(n_chunks):
                kd = r[_chunk_rows(c)] * kdec_ref[:, _head(h)]
                kdt[c, h] = kd.T.astype(BF16)

    for pair in pairs:
        vb[:, pair_cols(pair)] = proj(SEG_V, pair).astype(BF16)

    for c in range(n_chunks):
        rows = _chunk_rows(c)
        for h in range(HEADS):
            hs = _head(h)
            scb[c, h] = (_dot(qdb[rows, hs], kdt[c, h]) * mask_ref[h]).astype(BF16)
            kvs[c, h] = _dot(kdt[c, h], vb[rows, hs])

    for pair in pairs:
        gr[:, pair_cols(pair)] = _silu(proj(SEG_GR, pair))
    vnb[...] = _layernorm(proj(SEG_VG), lng, lnb).astype(BF16)

    for h in range(HEADS):
        s = s_ref[h]
        for c in range(n_chunks):
            sb[c, h] = s.astype(BF16)
            s = s * CHUNK_DECAY[h] + kvs[c, h]
        s_ref[h] = s

    for c in range(n_chunks):
        rows = _chunk_rows(c)
        for h in range(HEADS):
            hs = _head(h)
            lhs = jnp.concatenate([scb[c, h], qdb[rows, hs]], axis=1)
            rhs = jnp.concatenate([vb[rows, hs], sb[c, h]], axis=0)
            om[rows, hs] = (_rms_scale(_dot(lhs, rhs)) * gr[rows, hs]).astype(BF16)

    for pair in pairs:
        ug[:, pair_cols(pair)] = proj(SEG_U, pair) * _silu(proj(SEG_GG, pair))

    for c in range(n_chunks):
        rows = _chunk_rows(c)
        for h in range(HEADS):
            hs = _head(h)
            s = _dot(wtril[h], vnb[rows, hs]) + gb_ref[:, hs]
            om[rows, WIDTH + h * HD:WIDTH + (h + 1) * HD] = (ug[rows, hs] * s).astype(BF16)

    half = x_ref.shape[0] // 2
    for rows in (slice(0, half), slice(half, 2 * half)):
        y = x_ref[rows, :] + _dot(om[rows, :], wout_ref[...])
        if final_norm:
            y = _rms_scale(y) * fg_ref[...]
        y_ref[rows, :] = y


N_PLAIN_OPERANDS = 18
N_SHARED = 2
OUT_PROMPT_STATES, OUT_SAMPLE_STATES = 1, 3


def _prompt_layer(x, win, wout, rope, qdec, kdec, mask, ws, gb, lng, lnb, fg, q, k, v, states, shared, *,
                  layer, chunks_per_step, final_norm):
    batch, seq, _ = x.shape
    depth, n_rows = states.shape[:2]
    aliased = shared is not None
    tok = chunks_per_step * CHUNK
    steps_per_row = seq // tok
    heads = n_rows * HEADS // (batch * steps_per_row * SUBLANES)
    head_blocks = HEADS // heads
    assert heads * head_blocks == HEADS and batch * steps_per_row == (n_rows // SUBLANES) * head_blocks
    tile = lambda b, t: ((b * steps_per_row + t) // head_blocks, (b * steps_per_row + t) % head_blocks)

    row_spec = pl.BlockSpec((None, tok, D_MODEL), lambda b, t: (b, t, 0))
    pos_spec = pl.BlockSpec((tok, HD), lambda b, t: (t, 0))
    tile_spec = pl.BlockSpec((SUBLANES, heads * HD), tile)
    state_spec = pl.BlockSpec((None, SUBLANES, heads, HD, HD), lambda b, t: (layer, *tile(b, t), 0, 0))
    all_layers_spec = pl.BlockSpec((depth, SUBLANES, heads, HD, HD), lambda b, t: (0, *tile(b, t), 0, 0))
    if aliased:
        pstate_spec = pl.BlockSpec((None, None, HEADS, HD, HD), lambda b, t: (layer, b, 0, 0, 0))
    else:
        pstate_spec = pl.BlockSpec((depth, None, HEADS, HD, HD), lambda b, t: (0, b, 0, 0, 0))
    gamma = jnp.asarray(np.broadcast_to(np.asarray(GAMMA)[:, None, None], (HEADS, 1, HD)), F32)
    bf16_rows = lambda width: pltpu.VMEM((tok, width), BF16)
    per_chunk_head = lambda dtype: pltpu.VMEM((chunks_per_step, HEADS, CHUNK, CHUNK), dtype)

    return pl.pallas_call(
        functools.partial(_prompt_kernel, layer=layer, final_norm=final_norm, aliased=aliased),
        grid=(batch, steps_per_row),
        in_specs=[
            row_spec,
            _resident_spec((D_MODEL, N_SEG * WIDTH)),
            _resident_spec((MIX, D_MODEL)),
            pos_spec, pos_spec,
            _const_spec((CHUNK, WIDTH)), _const_spec((CHUNK, WIDTH)),
            _const_spec((HEADS, CHUNK, CHUNK)),
            _layer_spec((HEADS, CHUNK, CHUNK), layer),
            _const_spec((CHUNK, WIDTH)),
            _const_spec((depth, WIDTH)), _const_spec((depth, WIDTH)),
            _const_spec((1, D_MODEL)),
            _const_spec((HEADS, 1, HD)),
            tile_spec, tile_spec, tile_spec,
            state_spec,
            *[pl.BlockSpec(memory_space=pl.ANY)] * (N_SHARED * aliased),
        ],
        out_specs=[row_spec, pstate_spec, tile_spec, state_spec if aliased else all_layers_spec],
        out_shape=[jax.ShapeDtypeStruct(x.shape, F32),
                   jax.ShapeDtypeStruct((depth, batch, HEADS, HD, HD), F32),
                   jax.ShapeDtypeStruct((n_rows, WIDTH), F32),
                   jax.ShapeDtypeStruct(states.shape, F32)],
        input_output_aliases=({N_PLAIN_OPERANDS: OUT_PROMPT_STATES, N_PLAIN_OPERANDS + 1: OUT_SAMPLE_STATES}
                              if aliased else {}),
        scratch_shapes=[
            bf16_rows(D_MODEL),
            bf16_rows(WIDTH),
            per_chunk_head(BF16),
            bf16_rows(WIDTH),
            pltpu.VMEM((tok, WIDTH), F32),
            pltpu.VMEM((tok, WIDTH), F32),
            bf16_rows(WIDTH),
            bf16_rows(MIX),
            pltpu.VMEM((HEADS, CHUNK, CHUNK), BF16),
            per_chunk_head(BF16),
            per_chunk_head(F32),
            per_chunk_head(BF16),
        ],
        compiler_params=pltpu.CompilerParams(
            dimension_semantics=("arbitrary", "arbitrary"),
            vmem_limit_bytes=VMEM_LIMIT_BYTES,
        ),
        name="prompt_layer",
    )(x, win, wout, *rope, qdec, kdec, mask, ws, gb, lng, lnb, fg, gamma, q, k, v, states,
      *(shared if aliased else ()))


N_WOUT_BLOCKS = 8
N_START_STEPS = max(N_SEG, N_WOUT_BLOCKS)


def _finish_layer(h, scv_ref, opart_ref, gr_ref, m_ref, wout_ref):
    parts = []
    for hd in range(HEADS):
        hs = _head(hd)
        o = scv_ref[:, hs] + GAMMA[hd] * opart_ref[:, hs]
        parts.append((_rms_scale(o) * gr_ref[:, hs]).astype(BF16))
    om = jnp.concatenate(parts + [m_ref[...]], axis=1)
    return h + _dot(om, wout_ref[...])


def _sample_start_kernel(*refs, post, layer):
    h_ref, refs = refs[0], refs[1:]
    if post:
        post_refs, refs = refs[:5], refs[5:]
    ng_ref, win_ref, wout_ref, cos_ref, sin_ref, ws_ref, gmb_ref, lng_ref, lnb_ref = refs[:9]
    refs = refs[9:]
    if post:
        hout_ref, refs = refs[0], refs[1:]
    q_ref, k_ref, v_ref, scv_ref, gr_ref, m_ref, vn_ref, winb_ref, woutb_ref, xb, u_keep = refs
    j = pl.program_id(0)

    @pl.when(j == 0)
    def _tokens():
        h = h_ref[...]
        if post:
            h = _finish_layer(h, *post_refs)
            hout_ref[...] = h
        xb[...] = _rms_scale(h).astype(BF16)

    @pl.when(j < N_SEG)
    def _segment():
        gain = ng_ref[layer:layer + 1, :]
        tiles = [gain[:, i * LANES:(i + 1) * LANES] for i in range(D_MODEL // LANES)]
        pad = jnp.zeros((LANES - len(tiles), LANES), F32)
        gain_cols = jnp.concatenate(tiles + [pad], axis=0).T
        for i in range(D_MODEL // LANES):
            rows = slice(i * LANES, (i + 1) * LANES)
            winb_ref[rows, :] = (win_ref[rows, :] * gain_cols[:, i:i + 1]).astype(BF16)
        acc = _dot(xb[...], winb_ref[...])

        @pl.when(j == SEG_Q)
        def _():
            for hd in range(HEADS):
                q_ref[:, _head(hd)] = _rope(acc[:, _head(hd)], cos_ref[...], sin_ref[...])

        @pl.when(j == SEG_K)
        def _():
            for hd in range(HEADS):
                hs = _head(hd)
                k = _rope(acc[:, hs], cos_ref[...], sin_ref[...]) * K_SCALE
                k_ref[:, hs] = k
                qk = jnp.sum(q_ref[:, hs] * k, axis=-1, keepdims=True)
                scv_ref[:, hs] = jnp.broadcast_to(qk, k.shape)

        @pl.when(j == SEG_V)
        def _():
            v_ref[...] = acc
            scv_ref[...] = scv_ref[...] * acc

        @pl.when(j == SEG_GR)
        def _():
            gr_ref[...] = _silu(acc)

        @pl.when(j == SEG_U)
        def _():
            u_keep[...] = acc

        @pl.when(j == SEG_VG)
        def _():
            vn_ref[...] = _layernorm(acc, lng_ref[layer:layer + 1, :], lnb_ref[layer:layer + 1, :])

        @pl.when(j == SEG_GG)
        def _():
            for hd in range(HEADS):
                hs = _head(hd)
                s = vn_ref[:, hs] * ws_ref[hd, 0:1, 0:1] + gmb_ref[layer, hd:hd + 1, 0:1]
                m_ref[:, hs] = (u_keep[:, hs] * _silu(acc[:, hs]) * s).astype(BF16)

    woutb_ref[...] = wout_ref[...].astype(BF16)


def _sample_final_kernel(h_ref, scv_ref, opart_ref, gr_ref, m_ref, wout_ref, fg_ref, y_ref):
    y_ref[...] = _rms_scale(_finish_layer(h_ref[...], scv_ref, opart_ref, gr_ref, m_ref, wout_ref)) * fg_ref[...]


def _token_spec(n_rows):
    return pl.BlockSpec((n_rows, None, D_MODEL), lambda *_: (0, 0, 0))


def _sample_start(h, post_args, ng, w_in, w_out, rope, ws, gmb, lng, lnb, *, layer):
    n_rows, depth = h.shape[0], w_in.shape[0]
    post = post_args is not None
    full = _const_spec((n_rows, WIDTH))
    vecs = _const_spec((depth, WIDTH))
    seg = lambda j: jnp.minimum(j, N_SEG - 1)
    out_block = lambda j: jnp.minimum(j, N_WOUT_BLOCKS - 1)
    f32_rows = jax.ShapeDtypeStruct((n_rows, WIDTH), F32)
    args = [h] + (list(post_args) if post else []) + [ng, w_in, w_out, *rope, ws, gmb, lng, lnb]
    in_specs = [_token_spec(n_rows) if h.ndim == 3 else full]
    if post:
        in_specs += [full, full, full, full, _resident_spec((MIX, D_MODEL))]
    in_specs += [vecs,
                 pl.BlockSpec((None, D_MODEL, WIDTH), lambda j: (layer, 0, seg(j))),
                 pl.BlockSpec((None, MIX // N_WOUT_BLOCKS, D_MODEL), lambda j: (layer, out_block(j), 0)),
                 _const_spec((1, HD)), _const_spec((1, HD)),
                 _layer_spec((HEADS, CHUNK, CHUNK), layer), _const_spec((depth, HEADS, CHUNK)), vecs, vecs]
    out_shape = (([f32_rows] if post else []) + [f32_rows] * 5
                 + [jax.ShapeDtypeStruct((n_rows, WIDTH), BF16), f32_rows,
                    jax.ShapeDtypeStruct((D_MODEL, N_SEG * WIDTH), BF16), jax.ShapeDtypeStruct((MIX, D_MODEL), BF16)])
    out_specs = [full] * (len(out_shape) - 2) + [pl.BlockSpec((D_MODEL, WIDTH), lambda j: (0, seg(j))),
                                                 pl.BlockSpec((MIX // N_WOUT_BLOCKS, D_MODEL), lambda j: (out_block(j), 0))]
    return pl.pallas_call(
        functools.partial(_sample_start_kernel, post=post, layer=layer),
        grid=(N_START_STEPS,),
        in_specs=in_specs,
        out_specs=out_specs,
        out_shape=out_shape,
        scratch_shapes=[pltpu.VMEM((n_rows, D_MODEL), BF16),
                        pltpu.VMEM((n_rows, WIDTH), F32)],
        compiler_params=pltpu.CompilerParams(
            dimension_semantics=("arbitrary",),
            vmem_limit_bytes=VMEM_LIMIT_BYTES,
        ),
        name="sample_start",
    )(*args)


def _sample_final(h, post_args, fg):
    n_rows = h.shape[0]
    full = _const_spec((n_rows, WIDTH))
    return pl.pallas_call(
        _sample_final_kernel,
        grid=(1,),
        in_specs=[full, full, full, full, full, _const_spec((MIX, D_MODEL)), _const_spec((1, D_MODEL))],
        out_specs=_token_spec(n_rows),
        out_shape=jax.ShapeDtypeStruct((n_rows, 1, D_MODEL), F32),
        compiler_params=pltpu.CompilerParams(
            dimension_semantics=("arbitrary",),
            vmem_limit_bytes=VMEM_LIMIT_BYTES,
        ),
        name="sample_final",
    )(h, *post_args, fg)


def _rope_tables(pos):
    inv = ROPE_BASE ** (-np.arange(0, HD, 2, dtype=np.float64) / HD)
    ang = np.asarray(pos, np.float64)[:, None] * inv[None, :]
    c, s = np.cos(ang), np.sin(ang)
    return (jnp.asarray(np.concatenate([c, c], axis=-1), F32),
            jnp.asarray(np.concatenate([-s, s], axis=-1), F32))


def _retention_tables():
    lg = np.log(np.asarray(GAMMA, np.float64))
    idx = np.arange(CHUNK, dtype=np.float64)
    causal = idx[:, None] >= idx[None, :]
    mask = np.where(causal[None], np.exp(-lg * CHUNK)[:, None, None], 0.0)
    q_dec = np.exp(lg[None, :] * (idx[:, None] + 1.0))
    k_dec = np.exp(lg[None, :] * (CHUNK - 1.0 - idx[:, None])) * K_SCALE
    per_lane = lambda a: np.repeat(a, HD, axis=1)
    return jnp.asarray(mask, F32), jnp.asarray(per_lane(q_dec), F32), jnp.asarray(per_lane(k_dec), F32)


def kernel(x_prompt, x_sample, state_ret, norm_g, w_in, w_out, gm_ws, gm_b, gm_ln_g, gm_ln_b, final_g):
    depth = w_in.shape[0]
    batch, seq, _ = x_prompt.shape
    n_rows = x_sample.shape[0]
    assert x_sample.shape[1] == 1

    rope_p = _rope_tables(np.arange(seq))
    rope_s = _rope_tables(PAST_LEN + np.arange(1))
    mask, q_dec, k_dec = _retention_tables()
    fg = final_g.reshape(1, D_MODEL)
    start = functools.partial(_sample_start, ng=norm_g, w_in=w_in, w_out=w_out, rope=rope_s, ws=gm_ws, gmb=gm_b,
                              lng=gm_ln_g, lnb=gm_ln_b)

    h_p, h_s, shared, vns = x_prompt, x_sample, None, []
    q, k, v, scv, gr, m, vn, win_b, wout_b = start(h_s, None, layer=0)
    for l in range(depth):
        vns.append(vn)
        gb = jnp.repeat(gm_b[l].T, HD, axis=1)
        h_p, states_p, opart, states_s = _prompt_layer(
            h_p, win_b, wout_b, rope_p, q_dec, k_dec, mask, gm_ws, gb, gm_ln_g, gm_ln_b, fg,
            q, k, v, state_ret, shared, layer=l, chunks_per_step=CHUNKS_PER_STEP, final_norm=l == depth - 1)
        shared = (states_p, states_s)
        post_args = (scv, opart, gr, m, wout_b)
        if l == depth - 1:
            y_s = _sample_final(h_s, post_args, fg)
        else:
            h_s, q, k, v, scv, gr, m, vn, win_b, wout_b = start(h_s, post_args, layer=l + 1)

    return h_p, y_s, states_p, states_s, jnp.stack(vns).reshape(depth, n_rows, 1, HEADS, HD)
```

```python
import functools

import jax
import jax.numpy as jnp
import numpy as np
from jax import lax
from jax.experimental import pallas as pl
from jax.experimental.pallas import tpu as pltpu

F32 = jnp.float32
BF16 = jnp.bfloat16

D_MODEL = 1024
HEADS = 8
HD = 128
LANES, SUBLANES = 128, 8
WIDTH = HEADS * HD
N_SEG = 7
MIX = 2 * WIDTH
CHUNK = 128
PAST_LEN = 16384
ROPE_BASE = 10000.0
EPS = 1e-6
SEG_Q, SEG_K, SEG_V, SEG_GR, SEG_U, SEG_VG, SEG_GG = range(N_SEG)

GAMMA = tuple(1.0 - 2.0 ** (-5.0 - h) for h in range(HEADS))
CHUNK_DECAY = tuple(g ** CHUNK for g in GAMMA)
K_SCALE = HD ** -0.5

CHUNKS_PER_STEP = 4
V7X_VMEM_BYTES = 64 * 1024 * 1024
VMEM_LIMIT_BYTES = V7X_VMEM_BYTES - 2 * 1024 * 1024


def _silu(x):
    return x * (1.0 / (1.0 + jnp.exp(-x)))


def _rms_scale(x):
    return x * lax.rsqrt(jnp.mean(x * x, axis=-1, keepdims=True) + EPS)


def _layernorm(x, g, b):
    mu = jnp.mean(x, axis=-1, keepdims=True)
    xc = x - mu
    var = jnp.mean(xc * xc, axis=-1, keepdims=True)
    return xc * lax.rsqrt(var + EPS) * g + b


def _rope(x, cos, sin):
    return x * cos + pltpu.roll(x, HD // 2, 1) * sin


def _head(h):
    return slice(h * HD, (h + 1) * HD)


def _dot(a, b):
    return jnp.dot(a, b, preferred_element_type=F32)


def _const_spec(shape):
    zeros = (0,) * len(shape)
    return pl.BlockSpec(shape, lambda *_: zeros)


def _layer_spec(shape, layer, **kwargs):
    index = (layer,) + (0,) * len(shape)
    return pl.BlockSpec((None,) + shape, lambda *_: index, **kwargs)


def _resident_spec(shape):
    zeros = (0,) * len(shape)
    return pl.BlockSpec(shape, lambda *_: zeros, pipeline_mode=pl.Buffered(1))


def _chunk_rows(c):
    return slice(c * CHUNK, (c + 1) * CHUNK)


def _sample_state_step(q_ref, k_ref, v_ref, gamma_ref, s0_ref, write_new, opart_ref, head0):
    heads = opart_ref.shape[1] // HD
    tiles = [r[:, _head(hh)] for r in (k_ref, q_ref) for hh in range(heads)]
    pad = jnp.zeros((LANES - len(tiles) * SUBLANES, HD), F32)
    cols = jnp.concatenate(tiles + [pad], axis=0).T
    for hh in range(heads):
        for j in range(SUBLANES):
            s_old = s0_ref[j, hh]
            k_lane = hh * SUBLANES + j
            q_lane = heads * SUBLANES + k_lane
            k_col = jnp.broadcast_to(cols[:, k_lane:k_lane + 1], (HD, HD))
            q_col = jnp.broadcast_to(cols[:, q_lane:q_lane + 1], (HD, HD))
            write_new(j, hh, s_old * gamma_ref[head0 + hh] + k_col * v_ref[j:j + 1, _head(hh)])
            opart_ref[j:j + 1, _head(hh)] = jnp.sum(q_col * s_old, axis=0, keepdims=True)


def _prompt_kernel(x_ref, win_ref, wout_ref, cos_ref, sin_ref, qdec_ref, kdec_ref, mask_ref,
                   ws_ref, gb_ref, lng_ref, lnb_ref, fg_ref, gamma_ref, q_ref, k_ref, v_ref, s0_ref, *refs,
                   layer, final_norm, aliased):
    refs = refs[N_SHARED * aliased:]
    y_ref, sall_ref, opart_ref, snew_ref = refs[:4]
    xb, qdb, kdt, vb, gr, ug, vnb, om, wtril, sb, kvs, scb = refs[4:]
    s_ref = sall_ref if aliased else sall_ref.at[layer]
    lng, lnb = (r[layer:layer + 1, :] for r in (lng_ref, lnb_ref))

    t = pl.program_id(1)
    n_chunks = x_ref.shape[0] // CHUNK

    @pl.when(t == 0)
    def _start_of_row():
        sall_ref[...] = jnp.zeros(sall_ref.shape, F32)
        row = lax.broadcasted_iota(jnp.int32, (CHUNK, CHUNK), 0)
        col = lax.broadcasted_iota(jnp.int32, (CHUNK, CHUNK), 1)
        for h in range(HEADS):
            wtril[h] = jnp.where(row >= col, ws_ref[h], 0.0).astype(BF16)

    def proj(seg, pair=None):
        lo, width = (seg * WIDTH, WIDTH) if pair is None else (seg * WIDTH + pair * 2 * HD, 2 * HD)
        return _dot(xb[...], win_ref[:, lo:lo + width])

    pairs = range(HEADS // 2)
    pair_cols = lambda pair: slice(pair * 2 * HD, (pair + 1) * 2 * HD)

    xb[...] = _rms_scale(x_ref[...]).astype(BF16)

    head_blocks = WIDTH // opart_ref.shape[1]
    head0 = lax.rem(pl.program_id(0) * pl.num_programs(1) + t, head_blocks) * (HEADS // head_blocks)
    if aliased:
        def write_new(j, hh, s_new):
            snew_ref[j, hh] = s_new
    else:
        def write_new(j, hh, s_new):
            for l in range(snew_ref.shape[0]):
                snew_ref[l, j, hh] = s_new if l == layer else jnp.zeros_like(s_new)
    _sample_state_step(q_ref, k_ref, v_ref, gamma_ref, s0_ref, write_new, opart_ref, head0)

    cos, sin = cos_ref[...], sin_ref[...]
    for pair in pairs:
        acc = proj(SEG_Q, pair)
        for i in range(2):
            h = 2 * pair + i
            r = _rope(acc[:, _head(i)], cos, sin)
            for c in range(n_chunks):
                rows = _chunk_rows(c)
                qdb[rows, _head(h)] = (r[rows] * qdec_ref[:, _head(h)]).astype(BF16)

    for pair in pairs:
        acc = proj(SEG_K, pair)
        for i in range(2):
            h = 2 * pair + i
            r = _rope(acc[:, _head(i)], cos, sin)
            for c in range(n_chunks):
                kd = r[_chunk_rows(c)] * kdec_ref[:, _head(h)]
                kdt[c, h] = kd.T.astype(BF16)

    for pair in pairs:
        vb[:, pair_cols(pair)] = proj(SEG_V, pair).astype(BF16)

    for c in range(n_chunks):
        rows = _chunk_rows(c)
        for h in range(HEADS):
            hs = _head(h)
            scb[c, h] = (_dot(qdb[rows, hs], kdt[c, h]) * mask_ref[h]).astype(BF16)
            kvs[c, h] = _dot(kdt[c, h], vb[rows, hs])

    for pair in pairs:
        gr[:, pair_cols(pair)] = _silu(proj(SEG_GR, pair))
    vnb[...] = _layernorm(proj(SEG_VG), lng, lnb).astype(BF16)

    for h in range(HEADS):
        s = s_ref[h]
        for c in range(n_chunks):
            sb[c, h] = s.astype(BF16)
            s = s * CHUNK_DECAY[h] + kvs[c, h]
        s_ref[h] = s

    for c in range(n_chunks):
        rows = _chunk_rows(c)
        for h in range(HEADS):
            hs = _head(h)
            lhs = jnp.concatenate([scb[c, h], qdb[rows, hs]], axis=1)
            rhs = jnp.concatenate([vb[rows, hs], sb[c, h]], axis=0)
            om[rows, hs] = (_rms_scale(_dot(lhs, rhs)) * gr[rows, hs]).astype(BF16)

    for pair in pairs:
        ug[:, pair_cols(pair)] = proj(SEG_U, pair) * _silu(proj(SEG_GG, pair))

    for c in range(n_chunks):
        rows = _chunk_rows(c)
        for h in range(HEADS):
            hs = _head(h)
            s = _dot(wtril[h], vnb[rows, hs]) + gb_ref[:, hs]
            om[rows, WIDTH + h * HD:WIDTH + (h + 1) * HD] = (ug[rows, hs] * s).astype(BF16)

    half = x_ref.shape[0] // 2
    for rows in (slice(0, half), slice(half, 2 * half)):
        y = x_ref[rows, :] + _dot(om[rows, :], wout_ref[...])
        if final_norm:
            y = _rms_scale(y) * fg_ref[...]
        y_ref[rows, :] = y


N_PLAIN_OPERANDS = 18
N_SHARED = 2
OUT_PROMPT_STATES, OUT_SAMPLE_STATES = 1, 3


def _prompt_layer(x, win, wout, rope, qdec, kdec, mask, ws, gb, lng, lnb, fg, q, k, v, states, shared, *,
                  layer, chunks_per_step, final_norm):
    batch, seq, _ = x.shape
    depth, n_rows = states.shape[:2]
    aliased = shared is not None
    tok = chunks_per_step * CHUNK
    steps_per_row = seq // tok
    heads = n_rows * HEADS // (batch * steps_per_row * SUBLANES)
    head_blocks = HEADS // heads
    assert heads * head_blocks == HEADS and batch * steps_per_row == (n_rows // SUBLANES) * head_blocks
    tile = lambda b, t: ((b * steps_per_row + t) // head_blocks, (b * steps_per_row + t) % head_blocks)

    row_spec = pl.BlockSpec((None, tok, D_MODEL), lambda b, t: (b, t, 0))
    pos_spec = pl.BlockSpec((tok, HD), lambda b, t: (t, 0))
    tile_spec = pl.BlockSpec((SUBLANES, heads * HD), tile)
    state_spec = pl.BlockSpec((None, SUBLANES, heads, HD, HD), lambda b, t: (layer, *tile(b, t), 0, 0))
    all_layers_spec = pl.BlockSpec((depth, SUBLANES, heads, HD, HD), lambda b, t: (0, *tile(b, t), 0, 0))
    if aliased:
        pstate_spec = pl.BlockSpec((None, None, HEADS, HD, HD), lambda b, t: (layer, b, 0, 0, 0))
    else:
        pstate_spec = pl.BlockSpec((depth, None, HEADS, HD, HD), lambda b, t: (0, b, 0, 0, 0))
    gamma = jnp.asarray(np.broadcast_to(np.asarray(GAMMA)[:, None, None], (HEADS, 1, HD)), F32)
    bf16_rows = lambda width: pltpu.VMEM((tok, width), BF16)
    per_chunk_head = lambda dtype: pltpu.VMEM((chunks_per_step, HEADS, CHUNK, CHUNK), dtype)

    return pl.pallas_call(
        functools.partial(_prompt_kernel, layer=layer, final_norm=final_norm, aliased=aliased),
        grid=(batch, steps_per_row),
        in_specs=[
            row_spec,
            _resident_spec((D_MODEL, N_SEG * WIDTH)),
            _resident_spec((MIX, D_MODEL)),
            pos_spec, pos_spec,
            _const_spec((CHUNK, WIDTH)), _const_spec((CHUNK, WIDTH)),
            _const_spec((HEADS, CHUNK, CHUNK)),
            _layer_spec((HEADS, CHUNK, CHUNK), layer),
            _const_spec((CHUNK, WIDTH)),
            _const_spec((depth, WIDTH)), _const_spec((depth, WIDTH)),
            _const_spec((1, D_MODEL)),
            _const_spec((HEADS, 1, HD)),
            tile_spec, tile_spec, tile_spec,
            state_spec,
            *[pl.BlockSpec(memory_space=pl.ANY)] * (N_SHARED * aliased),
        ],
        out_specs=[row_spec, pstate_spec, tile_spec, state_spec if aliased else all_layers_spec],
        out_shape=[jax.ShapeDtypeStruct(x.shape, F32),
                   jax.ShapeDtypeStruct((depth, batch, HEADS, HD, HD), F32),
                   jax.ShapeDtypeStruct((n_rows, WIDTH), F32),
                   jax.ShapeDtypeStruct(states.shape, F32)],
        input_output_aliases=({N_PLAIN_OPERANDS: OUT_PROMPT_STATES, N_PLAIN_OPERANDS + 1: OUT_SAMPLE_STATES}
                              if aliased else {}),
        scratch_shapes=[
            bf16_rows(D_MODEL),
            bf16_rows(WIDTH),
            per_chunk_head(BF16),
            bf16_rows(WIDTH),
            pltpu.VMEM((tok, WIDTH), F32),
            pltpu.VMEM((tok, WIDTH), F32),
            bf16_rows(WIDTH),
            bf16_rows(MIX),
            pltpu.VMEM((HEADS, CHUNK, CHUNK), BF16),
            per_chunk_head(BF16),
            per_chunk_head(F32),
            per_chunk_head(BF16),
        ],
        compiler_params=pltpu.CompilerParams(
            dimension_semantics=("arbitrary", "arbitrary"),
            vmem_limit_bytes=VMEM_LIMIT_BYTES,
        ),
        name="prompt_layer",
    )(x, win, wout, *rope, qdec, kdec, mask, ws, gb, lng, lnb, fg, gamma, q, k, v, states,
      *(shared if aliased else ()))


N_WOUT_BLOCKS = 8
N_START_STEPS = max(N_SEG, N_WOUT_BLOCKS)


def _finish_layer(h, scv_ref, opart_ref, gr_ref, m_ref, wout_ref):
    parts = []
    for hd in range(HEADS):
        hs = _head(hd)
        o = scv_ref[:, hs] + GAMMA[hd] * opart_ref[:, hs]
        parts.append((_rms_scale(o) * gr_ref[:, hs]).astype(BF16))
    om = jnp.concatenate(parts + [m_ref[...]], axis=1)
    return h + _dot(om, wout_ref[...])


def _sample_start_kernel(*refs, post, layer):
    h_ref, refs = refs[0], refs[1:]
    if post:
        post_refs, vn_prev_ref, refs = refs[:5], refs[5], refs[6:]
    ng_ref, win_ref, wout_ref, cos_ref, sin_ref, ws_ref, gmb_ref, lng_ref, lnb_ref = refs[:9]
    refs = refs[9:]
    if post:
        hout_ref, refs = refs[0], refs[1:]
    q_ref, k_ref, v_ref, scv_ref, gr_ref, m_ref, vn_ref, gb_ref, winb_ref, woutb_ref, xb, u_keep, vn_keep = refs
    j = pl.program_id(0)

    @pl.when(j == 0)
    def _tokens():
        h = h_ref[...]
        if post:
            h = _finish_layer(h, *post_refs)
            hout_ref[...] = h
        xb[...] = _rms_scale(h).astype(BF16)

    @pl.when(j < N_SEG)
    def _segment():
        gain = ng_ref[layer:layer + 1, :]
        tiles = [gain[:, i * LANES:(i + 1) * LANES] for i in range(D_MODEL // LANES)]
        pad = jnp.zeros((LANES - len(tiles), LANES), F32)
        gain_cols = jnp.concatenate(tiles + [pad], axis=0).T
        for i in range(D_MODEL // LANES):
            rows = slice(i * LANES, (i + 1) * LANES)
            winb_ref[rows, :] = (win_ref[rows, :] * gain_cols[:, i:i + 1]).astype(BF16)
        acc = _dot(xb[...], winb_ref[...])

        @pl.when(j == SEG_Q)
        def _():
            for hd in range(HEADS):
                q_ref[:, _head(hd)] = _rope(acc[:, _head(hd)], cos_ref[...], sin_ref[...])

        @pl.when(j == SEG_K)
        def _():
            for hd in range(HEADS):
                hs = _head(hd)
                k = _rope(acc[:, hs], cos_ref[...], sin_ref[...]) * K_SCALE
                k_ref[:, hs] = k
                qk = jnp.sum(q_ref[:, hs] * k, axis=-1, keepdims=True)
                scv_ref[:, hs] = jnp.broadcast_to(qk, k.shape)

        @pl.when(j == SEG_V)
        def _():
            v_ref[...] = acc
            scv_ref[...] = scv_ref[...] * acc

        @pl.when(j == SEG_GR)
        def _():
            gr_ref[...] = _silu(acc)

        @pl.when(j == SEG_U)
        def _():
            u_keep[...] = acc

        @pl.when(j == SEG_VG)
        def _():
            vn = _layernorm(acc, lng_ref[layer:layer + 1, :], lnb_ref[layer:layer + 1, :])
            vn_keep[...] = vn
            if post:
                vn_ref[0:layer] = vn_prev_ref[...]
            vn_ref[layer] = jnp.swapaxes(jnp.stack([vn[:, _head(hd)] for hd in range(HEADS)], axis=0), 0, 1)

        @pl.when(j == SEG_GG)
        def _():
            for hd in range(HEADS):
                hs = _head(hd)
                s = vn_keep[:, hs] * ws_ref[hd, 0:1, 0:1] + gmb_ref[layer, hd:hd + 1, 0:1]
                m_ref[:, hs] = (u_keep[:, hs] * _silu(acc[:, hs]) * s).astype(BF16)

    woutb_ref[...] = wout_ref[...].astype(BF16)

    @pl.when(j == pl.num_programs(0) - 1)
    def _bias_table():
        b = gmb_ref[layer]
        cols = jnp.concatenate([b, jnp.zeros((LANES - HEADS, CHUNK), F32)], axis=0).T
        for hd in range(HEADS):
            gb_ref[:, _head(hd)] = jnp.broadcast_to(cols[:, hd:hd + 1], (CHUNK, HD))


def _sample_final_kernel(h_ref, scv_ref, opart_ref, gr_ref, m_ref, wout_ref, fg_ref, y_ref):
    y_ref[...] = _rms_scale(_finish_layer(h_ref[...], scv_ref, opart_ref, gr_ref, m_ref, wout_ref)) * fg_ref[...]


def _token_spec(n_rows):
    return pl.BlockSpec((n_rows, None, D_MODEL), lambda *_: (0, 0, 0))


def _sample_start(h, post_args, ng, w_in, w_out, rope, ws, gmb, lng, lnb, *, layer):
    n_rows, depth = h.shape[0], w_in.shape[0]
    post = post_args is not None
    assert post == (layer > 0)
    vn_stack = (layer + 1, n_rows, HEADS, HD)
    full = _const_spec((n_rows, WIDTH))
    vecs = _const_spec((depth, WIDTH))
    seg = lambda j: jnp.minimum(j, N_SEG - 1)
    out_block = lambda j: jnp.minimum(j, N_WOUT_BLOCKS - 1)
    f32_rows = jax.ShapeDtypeStruct((n_rows, WIDTH), F32)
    args = [h] + (list(post_args) if post else []) + [ng, w_in, w_out, *rope, ws, gmb, lng, lnb]
    in_specs = [_token_spec(n_rows) if h.ndim == 3 else full]
    if post:
        in_specs += [full, full, full, full, _resident_spec((MIX, D_MODEL)), _const_spec((layer, *vn_stack[1:]))]
    in_specs += [vecs,
                 pl.BlockSpec((None, D_MODEL, WIDTH), lambda j: (layer, 0, seg(j))),
                 pl.BlockSpec((None, MIX // N_WOUT_BLOCKS, D_MODEL), lambda j: (layer, out_block(j), 0)),
                 _const_spec((1, HD)), _const_spec((1, HD)),
                 _layer_spec((HEADS, CHUNK, CHUNK), layer), _const_spec((depth, HEADS, CHUNK)), vecs, vecs]
    out_shape = (([f32_rows] if post else []) + [f32_rows] * 5
                 + [jax.ShapeDtypeStruct((n_rows, WIDTH), BF16), jax.ShapeDtypeStruct(vn_stack, F32),
                    jax.ShapeDtypeStruct((CHUNK, WIDTH), F32),
                    jax.ShapeDtypeStruct((D_MODEL, N_SEG * WIDTH), BF16), jax.ShapeDtypeStruct((MIX, D_MODEL), BF16)])
    out_specs = [full] * (len(out_shape) - 4) + [_const_spec(vn_stack), _const_spec((CHUNK, WIDTH)),
                                                 pl.BlockSpec((D_MODEL, WIDTH), lambda j: (0, seg(j))),
                                                 pl.BlockSpec((MIX // N_WOUT_BLOCKS, D_MODEL), lambda j: (out_block(j), 0))]
    return pl.pallas_call(
        functools.partial(_sample_start_kernel, post=post, layer=layer),
        grid=(N_START_STEPS,),
        in_specs=in_specs,
        out_specs=out_specs,
        out_shape=out_shape,
        scratch_shapes=[pltpu.VMEM((n_rows, D_MODEL), BF16),
                        pltpu.VMEM((n_rows, WIDTH), F32),
                        pltpu.VMEM((n_rows, WIDTH), F32)],
        compiler_params=pltpu.CompilerParams(
            dimension_semantics=("arbitrary",),
            vmem_limit_bytes=VMEM_LIMIT_BYTES,
        ),
        name="sample_start",
    )(*args)


def _sample_final(h, post_args, fg):
    n_rows = h.shape[0]
    full = _const_spec((n_rows, WIDTH))
    return pl.pallas_call(
        _sample_final_kernel,
        grid=(1,),
        in_specs=[full, full, full, full, full, _const_spec((MIX, D_MODEL)), _const_spec((1, D_MODEL))],
        out_specs=_token_spec(n_rows),
        out_shape=jax.ShapeDtypeStruct((n_rows, 1, D_MODEL), F32),
        compiler_params=pltpu.CompilerParams(
            dimension_semantics=("arbitrary",),
            vmem_limit_bytes=VMEM_LIMIT_BYTES,
        ),
        name="sample_final",
    )(h, *post_args, fg)


def _rope_tables(pos):
    inv = ROPE_BASE ** (-np.arange(0, HD, 2, dtype=np.float64) / HD)
    ang = np.asarray(pos, np.float64)[:, None] * inv[None, :]
    c, s = np.cos(ang), np.sin(ang)
    return (jnp.asarray(np.concatenate([c, c], axis=-1), F32),
            jnp.asarray(np.concatenate([-s, s], axis=-1), F32))


def _retention_tables():
    lg = np.log(np.asarray(GAMMA, np.float64))
    idx = np.arange(CHUNK, dtype=np.float64)
    causal = idx[:, None] >= idx[None, :]
    mask = np.where(causal[None], np.exp(-lg * CHUNK)[:, None, None], 0.0)
    q_dec = np.exp(lg[None, :] * (idx[:, None] + 1.0))
    k_dec = np.exp(lg[None, :] * (CHUNK - 1.0 - idx[:, None])) * K_SCALE
    per_lane = lambda a: np.repeat(a, HD, axis=1)
    return jnp.asarray(mask, F32), jnp.asarray(per_lane(q_dec), F32), jnp.asarray(per_lane(k_dec), F32)


def kernel(x_prompt, x_sample, state_ret, norm_g, w_in, w_out, gm_ws, gm_b, gm_ln_g, gm_ln_b, final_g):
    depth = w_in.shape[0]
    batch, seq, _ = x_prompt.shape
    n_rows = x_sample.shape[0]
    assert x_sample.shape[1] == 1

    rope_p = _rope_tables(np.arange(seq))
    rope_s = _rope_tables(PAST_LEN + np.arange(1))
    mask, q_dec, k_dec = _retention_tables()
    fg = final_g.reshape(1, D_MODEL)
    start = functools.partial(_sample_start, ng=norm_g, w_in=w_in, w_out=w_out, rope=rope_s, ws=gm_ws, gmb=gm_b,
                              lng=gm_ln_g, lnb=gm_ln_b)

    h_p, h_s, shared = x_prompt, x_sample, None
    q, k, v, scv, gr, m, vn, gb, win_b, wout_b = start(h_s, None, layer=0)
    for l in range(depth):
        h_p, states_p, opart, states_s = _prompt_layer(
            h_p, win_b, wout_b, rope_p, q_dec, k_dec, mask, gm_ws, gb, gm_ln_g, gm_ln_b, fg,
            q, k, v, state_ret, shared, layer=l, chunks_per_step=CHUNKS_PER_STEP, final_norm=l == depth - 1)
        shared = (states_p, states_s)
        post_args = (scv, opart, gr, m, wout_b)
        if l == depth - 1:
            y_s = _sample_final(h_s, post_args, fg)
        else:
            h_s, q, k, v, scv, gr, m, vn, gb, win_b, wout_b = start(h_s, (*post_args, vn), layer=l + 1)

    return h_p, y_s, states_p, states_s, vn.reshape(depth, n_rows, 1, HEADS, HD)
```

```python
import functools

import jax
import jax.numpy as jnp
import numpy as np
from jax import lax
from jax.experimental import pallas as pl
from jax.experimental.pallas import tpu as pltpu

F32 = jnp.float32
BF16 = jnp.bfloat16

D_MODEL = 1024
HEADS = 8
HD = 128
LANES, SUBLANES = 128, 8
WIDTH = HEADS * HD
N_SEG = 7
MIX = 2 * WIDTH
CHUNK = 128
PAST_LEN = 16384
ROPE_BASE = 10000.0
EPS = 1e-6
SEG_Q, SEG_K, SEG_V, SEG_GR, SEG_U, SEG_VG, SEG_GG = range(N_SEG)

GAMMA = tuple(1.0 - 2.0 ** (-5.0 - h) for h in range(HEADS))
CHUNK_DECAY = tuple(g ** CHUNK for g in GAMMA)
K_SCALE = HD ** -0.5

CHUNKS_PER_STEP = 4
V7X_VMEM_BYTES = 64 * 1024 * 1024
VMEM_LIMIT_BYTES = V7X_VMEM_BYTES - 2 * 1024 * 1024


def _silu(x):
    return x * (1.0 / (1.0 + jnp.exp(-x)))


def _rms_scale(x):
    return x * lax.rsqrt(jnp.mean(x * x, axis=-1, keepdims=True) + EPS)


def _layernorm(x, g, b):
    mu = jnp.mean(x, axis=-1, keepdims=True)
    xc = x - mu
    var = jnp.mean(xc * xc, axis=-1, keepdims=True)
    return xc * lax.rsqrt(var + EPS) * g + b


def _rope(x, cos, sin):
    return x * cos + pltpu.roll(x, HD // 2, 1) * sin


def _head(h):
    return slice(h * HD, (h + 1) * HD)


def _dot(a, b):
    return jnp.dot(a, b, preferred_element_type=F32)


def _pack(w):
    return pltpu.bitcast(w, jnp.uint32)


def _unpack(words):
    return pltpu.bitcast(words, BF16)


def _const_spec(shape):
    zeros = (0,) * len(shape)
    return pl.BlockSpec(shape, lambda *_: zeros)


def _layer_spec(shape, layer, **kwargs):
    index = (layer,) + (0,) * len(shape)
    return pl.BlockSpec((None,) + shape, lambda *_: index, **kwargs)


def _resident_spec(shape):
    zeros = (0,) * len(shape)
    return pl.BlockSpec(shape, lambda *_: zeros, pipeline_mode=pl.Buffered(1))


def _chunk_rows(c):
    return slice(c * CHUNK, (c + 1) * CHUNK)


def _sample_state_step(q_ref, k_ref, v_ref, gamma_ref, s0_ref, write_new, opart_ref, head0):
    heads = opart_ref.shape[1] // HD
    tiles = [r[:, _head(hh)] for r in (k_ref, q_ref) for hh in range(heads)]
    pad = jnp.zeros((LANES - len(tiles) * SUBLANES, HD), F32)
    cols = jnp.concatenate(tiles + [pad], axis=0).T
    for hh in range(heads):
        for j in range(SUBLANES):
            s_old = s0_ref[j, hh]
            k_lane = hh * SUBLANES + j
            q_lane = heads * SUBLANES + k_lane
            k_col = jnp.broadcast_to(cols[:, k_lane:k_lane + 1], (HD, HD))
            q_col = jnp.broadcast_to(cols[:, q_lane:q_lane + 1], (HD, HD))
            write_new(j, hh, s_old * gamma_ref[head0 + hh] + k_col * v_ref[j:j + 1, _head(hh)])
            opart_ref[j:j + 1, _head(hh)] = jnp.sum(q_col * s_old, axis=0, keepdims=True)


def _prompt_kernel(x_ref, win_ref, wout_ref, cos_ref, sin_ref, qdec_ref, kdec_ref, mask_ref,
                   ws_ref, gb_ref, lng_ref, lnb_ref, fg_ref, gamma_ref, q_ref, k_ref, v_ref, s0_ref, *refs,
                   layer, final_norm, aliased):
    refs = refs[N_SHARED * aliased:]
    y_ref, sall_ref, opart_ref, snew_ref = refs[:4]
    xb, qdb, kdt, vb, gr, ug, vnb, om, wtril, sb, kvs, scb = refs[4:]
    s_ref = sall_ref if aliased else sall_ref.at[layer]
    lng, lnb = (r[layer:layer + 1, :] for r in (lng_ref, lnb_ref))

    t = pl.program_id(1)
    n_chunks = x_ref.shape[0] // CHUNK

    @pl.when(t == 0)
    def _start_of_row():
        sall_ref[...] = jnp.zeros(sall_ref.shape, F32)
        row = lax.broadcasted_iota(jnp.int32, (CHUNK, CHUNK), 0)
        col = lax.broadcasted_iota(jnp.int32, (CHUNK, CHUNK), 1)
        for h in range(HEADS):
            wtril[h] = jnp.where(row >= col, ws_ref[h], 0.0).astype(BF16)

    def proj(seg, pair=None):
        lo, width = (seg * WIDTH, WIDTH) if pair is None else (seg * WIDTH + pair * 2 * HD, 2 * HD)
        return _dot(xb[...], _unpack(win_ref[:, lo:lo + width]))

    pairs = range(HEADS // 2)
    pair_cols = lambda pair: slice(pair * 2 * HD, (pair + 1) * 2 * HD)

    xb[...] = _rms_scale(x_ref[...]).astype(BF16)

    head_blocks = WIDTH // opart_ref.shape[1]
    head0 = lax.rem(pl.program_id(0) * pl.num_programs(1) + t, head_blocks) * (HEADS // head_blocks)
    if aliased:
        def write_new(j, hh, s_new):
            snew_ref[j, hh] = s_new
    else:
        def write_new(j, hh, s_new):
            for l in range(snew_ref.shape[0]):
                snew_ref[l, j, hh] = s_new if l == layer else jnp.zeros_like(s_new)
    _sample_state_step(q_ref, k_ref, v_ref, gamma_ref, s0_ref, write_new, opart_ref, head0)

    cos, sin = cos_ref[...], sin_ref[...]
    for pair in pairs:
        acc = proj(SEG_Q, pair)
        for i in range(2):
            h = 2 * pair + i
            r = _rope(acc[:, _head(i)], cos, sin)
            for c in range(n_chunks):
                rows = _chunk_rows(c)
                qdb[rows, _head(h)] = (r[rows] * qdec_ref[:, _head(h)]).astype(BF16)

    for pair in pairs:
        acc = proj(SEG_K, pair)
        for i in range(2):
            h = 2 * pair + i
            r = _rope(acc[:, _head(i)], cos, sin)
            for c in range(n_chunks):
                kd = r[_chunk_rows(c)] * kdec_ref[:, _head(h)]
                kdt[c, h] = kd.T.astype(BF16)

    for pair in pairs:
        vb[:, pair_cols(pair)] = proj(SEG_V, pair).astype(BF16)

    for c in range(n_chunks):
        rows = _chunk_rows(c)
        for h in range(HEADS):
            hs = _head(h)
            scb[c, h] = (_dot(qdb[rows, hs], kdt[c, h]) * mask_ref[h]).astype(BF16)
            kvs[c, h] = _dot(kdt[c, h], vb[rows, hs])

    for pair in pairs:
        gr[:, pair_cols(pair)] = _silu(proj(SEG_GR, pair))
    vnb[...] = _layernorm(proj(SEG_VG), lng, lnb).astype(BF16)

    for h in range(HEADS):
        s = s_ref[h]
        for c in range(n_chunks):
            sb[c, h] = s.astype(BF16)
            s = s * CHUNK_DECAY[h] + kvs[c, h]
        s_ref[h] = s

    for c in range(n_chunks):
        rows = _chunk_rows(c)
        for h in range(HEADS):
            hs = _head(h)
            lhs = jnp.concatenate([scb[c, h], qdb[rows, hs]], axis=1)
            rhs = jnp.concatenate([vb[rows, hs], sb[c, h]], axis=0)
            om[rows, hs] = (_rms_scale(_dot(lhs, rhs)) * gr[rows, hs]).astype(BF16)

    for pair in pairs:
        ug[:, pair_cols(pair)] = proj(SEG_U, pair) * _silu(proj(SEG_GG, pair))

    for c in range(n_chunks):
        rows = _chunk_rows(c)
        for h in range(HEADS):
            hs = _head(h)
            s = _dot(wtril[h], vnb[rows, hs]) + gb_ref[:, hs]
            om[rows, WIDTH + h * HD:WIDTH + (h + 1) * HD] = (ug[rows, hs] * s).astype(BF16)

    half = x_ref.shape[0] // 2
    for rows in (slice(0, half), slice(half, 2 * half)):
        y = x_ref[rows, :] + _dot(om[rows, :], _unpack(wout_ref[...]))
        if final_norm:
            y = _rms_scale(y) * fg_ref[...]
        y_ref[rows, :] = y


N_PLAIN_OPERANDS = 18
N_SHARED = 2
OUT_PROMPT_STATES, OUT_SAMPLE_STATES = 1, 3


def _prompt_layer(x, win, wout, rope, qdec, kdec, mask, ws, gb, lng, lnb, fg, q, k, v, states, shared, *,
                  layer, chunks_per_step, final_norm):
    batch, seq, _ = x.shape
    depth, n_rows = states.shape[:2]
    aliased = shared is not None
    tok = chunks_per_step * CHUNK
    steps_per_row = seq // tok
    heads = n_rows * HEADS // (batch * steps_per_row * SUBLANES)
    head_blocks = HEADS // heads
    assert heads * head_blocks == HEADS and batch * steps_per_row == (n_rows // SUBLANES) * head_blocks
    tile = lambda b, t: ((b * steps_per_row + t) // head_blocks, (b * steps_per_row + t) % head_blocks)

    row_spec = pl.BlockSpec((None, tok, D_MODEL), lambda b, t: (b, t, 0))
    pos_spec = pl.BlockSpec((tok, HD), lambda b, t: (t, 0))
    tile_spec = pl.BlockSpec((SUBLANES, heads * HD), tile)
    state_spec = pl.BlockSpec((None, SUBLANES, heads, HD, HD), lambda b, t: (layer, *tile(b, t), 0, 0))
    all_layers_spec = pl.BlockSpec((depth, SUBLANES, heads, HD, HD), lambda b, t: (0, *tile(b, t), 0, 0))
    if aliased:
        pstate_spec = pl.BlockSpec((None, None, HEADS, HD, HD), lambda b, t: (layer, b, 0, 0, 0))
    else:
        pstate_spec = pl.BlockSpec((depth, None, HEADS, HD, HD), lambda b, t: (0, b, 0, 0, 0))
    gamma = jnp.asarray(np.broadcast_to(np.asarray(GAMMA)[:, None, None], (HEADS, 1, HD)), F32)
    bf16_rows = lambda width: pltpu.VMEM((tok, width), BF16)
    per_chunk_head = lambda dtype: pltpu.VMEM((chunks_per_step, HEADS, CHUNK, CHUNK), dtype)

    return pl.pallas_call(
        functools.partial(_prompt_kernel, layer=layer, final_norm=final_norm, aliased=aliased),
        grid=(batch, steps_per_row),
        in_specs=[
            row_spec,
            _resident_spec((D_MODEL // 2, N_SEG * WIDTH)),
            _resident_spec((MIX // 2, D_MODEL)),
            pos_spec, pos_spec,
            _const_spec((CHUNK, WIDTH)), _const_spec((CHUNK, WIDTH)),
            _const_spec((HEADS, CHUNK, CHUNK)),
            _layer_spec((HEADS, CHUNK, CHUNK), layer),
            _const_spec((CHUNK, WIDTH)),
            _const_spec((depth, WIDTH)), _const_spec((depth, WIDTH)),
            _const_spec((1, D_MODEL)),
            _const_spec((HEADS, 1, HD)),
            tile_spec, tile_spec, tile_spec,
            state_spec,
            *[pl.BlockSpec(memory_space=pl.ANY)] * (N_SHARED * aliased),
        ],
        out_specs=[row_spec, pstate_spec, tile_spec, state_spec if aliased else all_layers_spec],
        out_shape=[jax.ShapeDtypeStruct(x.shape, F32),
                   jax.ShapeDtypeStruct((depth, batch, HEADS, HD, HD), F32),
                   jax.ShapeDtypeStruct((n_rows, WIDTH), F32),
                   jax.ShapeDtypeStruct(states.shape, F32)],
        input_output_aliases=({N_PLAIN_OPERANDS: OUT_PROMPT_STATES, N_PLAIN_OPERANDS + 1: OUT_SAMPLE_STATES}
                              if aliased else {}),
        scratch_shapes=[
            bf16_rows(D_MODEL),
            bf16_rows(WIDTH),
            per_chunk_head(BF16),
            bf16_rows(WIDTH),
            pltpu.VMEM((tok, WIDTH), F32),
            pltpu.VMEM((tok, WIDTH), F32),
            bf16_rows(WIDTH),
            bf16_rows(MIX),
            pltpu.VMEM((HEADS, CHUNK, CHUNK), BF16),
            per_chunk_head(BF16),
            per_chunk_head(F32),
            per_chunk_head(BF16),
        ],
        compiler_params=pltpu.CompilerParams(
            dimension_semantics=("arbitrary", "arbitrary"),
            vmem_limit_bytes=VMEM_LIMIT_BYTES,
        ),
        name="prompt_layer",
    )(x, win, wout, *rope, qdec, kdec, mask, ws, gb, lng, lnb, fg, gamma, q, k, v, states,
      *(shared if aliased else ()))


N_WOUT_BLOCKS = 8
N_START_STEPS = max(N_SEG, N_WOUT_BLOCKS)


def _finish_layer(h, scv_ref, opart_ref, gr_ref, m_ref, wout_ref):
    parts = []
    for hd in range(HEADS):
        hs = _head(hd)
        o = scv_ref[:, hs] + GAMMA[hd] * opart_ref[:, hs]
        parts.append((_rms_scale(o) * gr_ref[:, hs]).astype(BF16))
    om = jnp.concatenate(parts + [m_ref[...]], axis=1)
    return h + _dot(om, _unpack(wout_ref[...]))


def _sample_start_kernel(*refs, post, layer):
    h_ref, refs = refs[0], refs[1:]
    if post:
        post_refs, vn_prev_ref, refs = refs[:5], refs[5], refs[6:]
    ng_ref, win_ref, wout_ref, cos_ref, sin_ref, ws_ref, gmb_ref, lng_ref, lnb_ref = refs[:9]
    refs = refs[9:]
    if post:
        hout_ref, refs = refs[0], refs[1:]
    q_ref, k_ref, v_ref, scv_ref, gr_ref, m_ref, vn_ref, gb_ref, winb_ref, woutb_ref, xb, u_keep, vn_keep = refs
    j = pl.program_id(0)

    @pl.when(j == 0)
    def _tokens():
        h = h_ref[...]
        if post:
            h = _finish_layer(h, *post_refs)
            hout_ref[...] = h
        xb[...] = _rms_scale(h).astype(BF16)

    @pl.when(j < N_SEG)
    def _segment():
        gain = ng_ref[layer:layer + 1, :]
        tiles = [gain[:, i * LANES:(i + 1) * LANES] for i in range(D_MODEL // LANES)]
        pad = jnp.zeros((LANES - len(tiles), LANES), F32)
        gain_cols = jnp.concatenate(tiles + [pad], axis=0).T
        for i in range(D_MODEL // LANES):
            rows = slice(i * LANES, (i + 1) * LANES)
            packed_rows = slice(i * LANES // 2, (i + 1) * LANES // 2)
            winb_ref[packed_rows, :] = _pack((win_ref[rows, :] * gain_cols[:, i:i + 1]).astype(BF16))
        acc = _dot(xb[...], _unpack(winb_ref[...]))

        @pl.when(j == SEG_Q)
        def _():
            for hd in range(HEADS):
                q_ref[:, _head(hd)] = _rope(acc[:, _head(hd)], cos_ref[...], sin_ref[...])

        @pl.when(j == SEG_K)
        def _():
            for hd in range(HEADS):
                hs = _head(hd)
                k = _rope(acc[:, hs], cos_ref[...], sin_ref[...]) * K_SCALE
                k_ref[:, hs] = k
                qk = jnp.sum(q_ref[:, hs] * k, axis=-1, keepdims=True)
                scv_ref[:, hs] = jnp.broadcast_to(qk, k.shape)

        @pl.when(j == SEG_V)
        def _():
            v_ref[...] = acc
            scv_ref[...] = scv_ref[...] * acc

        @pl.when(j == SEG_GR)
        def _():
            gr_ref[...] = _silu(acc)

        @pl.when(j == SEG_U)
        def _():
            u_keep[...] = acc

        @pl.when(j == SEG_VG)
        def _():
            vn = _layernorm(acc, lng_ref[layer:layer + 1, :], lnb_ref[layer:layer + 1, :])
            vn_keep[...] = vn
            if post:
                vn_ref[0:layer] = vn_prev_ref[...]
            vn_ref[layer] = jnp.swapaxes(jnp.stack([vn[:, _head(hd)] for hd in range(HEADS)], axis=0), 0, 1)

        @pl.when(j == SEG_GG)
        def _():
            for hd in range(HEADS):
                hs = _head(hd)
                s = vn_keep[:, hs] * ws_ref[hd, 0:1, 0:1] + gmb_ref[layer, hd:hd + 1, 0:1]
                m_ref[:, hs] = (u_keep[:, hs] * _silu(acc[:, hs]) * s).astype(BF16)

    woutb_ref[...] = _pack(wout_ref[...].astype(BF16))

    @pl.when(j == pl.num_programs(0) - 1)
    def _bias_table():
        b = gmb_ref[layer]
        cols = jnp.concatenate([b, jnp.zeros((LANES - HEADS, CHUNK), F32)], axis=0).T
        for hd in range(HEADS):
            gb_ref[:, _head(hd)] = jnp.broadcast_to(cols[:, hd:hd + 1], (CHUNK, HD))


def _sample_final_kernel(h_ref, scv_ref, opart_ref, gr_ref, m_ref, wout_ref, fg_ref, y_ref):
    y_ref[...] = _rms_scale(_finish_layer(h_ref[...], scv_ref, opart_ref, gr_ref, m_ref, wout_ref)) * fg_ref[...]


def _token_spec(n_rows):
    return pl.BlockSpec((n_rows, None, D_MODEL), lambda *_: (0, 0, 0))


def _sample_start(h, post_args, ng, w_in, w_out, rope, ws, gmb, lng, lnb, *, layer):
    n_rows, depth = h.shape[0], w_in.shape[0]
    post = post_args is not None
    assert post == (layer > 0)
    vn_stack = (layer + 1, n_rows, HEADS, HD)
    full = _const_spec((n_rows, WIDTH))
    vecs = _const_spec((depth, WIDTH))
    seg = lambda j: jnp.minimum(j, N_SEG - 1)
    out_block = lambda j: jnp.minimum(j, N_WOUT_BLOCKS - 1)
    f32_rows = jax.ShapeDtypeStruct((n_rows, WIDTH), F32)
    args = [h] + (list(post_args) if post else []) + [ng, w_in, w_out, *rope, ws, gmb, lng, lnb]
    in_specs = [_token_spec(n_rows) if h.ndim == 3 else full]
    if post:
        in_specs += [full, full, full, full, _resident_spec((MIX // 2, D_MODEL)), _const_spec((layer, *vn_stack[1:]))]
    in_specs += [vecs,
                 pl.BlockSpec((None, D_MODEL, WIDTH), lambda j: (layer, 0, seg(j))),
                 pl.BlockSpec((None, MIX // N_WOUT_BLOCKS, D_MODEL), lambda j: (layer, out_block(j), 0)),
                 _const_spec((1, HD)), _const_spec((1, HD)),
                 _layer_spec((HEADS, CHUNK, CHUNK), layer), _const_spec((depth, HEADS, CHUNK)), vecs, vecs]
    out_shape = (([f32_rows] if post else []) + [f32_rows] * 5
                 + [jax.ShapeDtypeStruct((n_rows, WIDTH), BF16), jax.ShapeDtypeStruct(vn_stack, F32),
                    jax.ShapeDtypeStruct((CHUNK, WIDTH), F32),
                    jax.ShapeDtypeStruct((D_MODEL // 2, N_SEG * WIDTH), jnp.uint32),
                    jax.ShapeDtypeStruct((MIX // 2, D_MODEL), jnp.uint32)])
    out_specs = [full] * (len(out_shape) - 4) + [_const_spec(vn_stack), _const_spec((CHUNK, WIDTH)),
                                                 pl.BlockSpec((D_MODEL // 2, WIDTH), lambda j: (0, seg(j))),
                                                 pl.BlockSpec((MIX // N_WOUT_BLOCKS // 2, D_MODEL), lambda j: (out_block(j), 0))]
    return pl.pallas_call(
        functools.partial(_sample_start_kernel, post=post, layer=layer),
        grid=(N_START_STEPS,),
        in_specs=in_specs,
        out_specs=out_specs,
        out_shape=out_shape,
        scratch_shapes=[pltpu.VMEM((n_rows, D_MODEL), BF16),
                        pltpu.VMEM((n_rows, WIDTH), F32),
                        pltpu.VMEM((n_rows, WIDTH), F32)],
        compiler_params=pltpu.CompilerParams(
            dimension_semantics=("arbitrary",),
            vmem_limit_bytes=VMEM_LIMIT_BYTES,
        ),
        name="sample_start",
    )(*args)


def _sample_final(h, post_args, fg):
    n_rows = h.shape[0]
    full = _const_spec((n_rows, WIDTH))
    return pl.pallas_call(
        _sample_final_kernel,
        grid=(1,),
        in_specs=[full, full, full, full, full, _const_spec((MIX // 2, D_MODEL)), _const_spec((1, D_MODEL))],
        out_specs=_token_spec(n_rows),
        out_shape=jax.ShapeDtypeStruct((n_rows, 1, D_MODEL), F32),
        compiler_params=pltpu.CompilerParams(
            dimension_semantics=("arbitrary",),
            vmem_limit_bytes=VMEM_LIMIT_BYTES,
        ),
        name="sample_final",
    )(h, *post_args, fg)


def _rope_tables(pos):
    inv = ROPE_BASE ** (-np.arange(0, HD, 2, dtype=np.float64) / HD)
    ang = np.asarray(pos, np.float64)[:, None] * inv[None, :]
    c, s = np.cos(ang), np.sin(ang)
    return (jnp.asarray(np.concatenate([c, c], axis=-1), F32),
            jnp.asarray(np.concatenate([-s, s], axis=-1), F32))


def _retention_tables():
    lg = np.log(np.asarray(GAMMA, np.float64))
    idx = np.arange(CHUNK, dtype=np.float64)
    causal = idx[:, None] >= idx[None, :]
    mask = np.where(causal[None], np.exp(-lg * CHUNK)[:, None, None], 0.0)
    q_dec = np.exp(lg[None, :] * (idx[:, None] + 1.0))
    k_dec = np.exp(lg[None, :] * (CHUNK - 1.0 - idx[:, None])) * K_SCALE
    per_lane = lambda a: np.repeat(a, HD, axis=1)
    return jnp.asarray(mask, F32), jnp.asarray(per_lane(q_dec), F32), jnp.asarray(per_lane(k_dec), F32)


def kernel(x_prompt, x_sample, state_ret, norm_g, w_in, w_out, gm_ws, gm_b, gm_ln_g, gm_ln_b, final_g):
    depth = w_in.shape[0]
    batch, seq, _ = x_prompt.shape
    n_rows = x_sample.shape[0]
    assert x_sample.shape[1] == 1

    rope_p = _rope_tables(np.arange(seq))
    rope_s = _rope_tables(PAST_LEN + np.arange(1))
    mask, q_dec, k_dec = _retention_tables()
    fg = final_g.reshape(1, D_MODEL)
    start = functools.partial(_sample_start, ng=norm_g, w_in=w_in, w_out=w_out, rope=rope_s, ws=gm_ws, gmb=gm_b,
                              lng=gm_ln_g, lnb=gm_ln_b)

    h_p, h_s, shared = x_prompt, x_sample, None
    q, k, v, scv, gr, m, vn, gb, win_b, wout_b = start(h_s, None, layer=0)
    for l in range(depth):
        h_p, states_p, opart, states_s = _prompt_layer(
            h_p, win_b, wout_b, rope_p, q_dec, k_dec, mask, gm_ws, gb, gm_ln_g, gm_ln_b, fg,
            q, k, v, state_ret, shared, layer=l, chunks_per_step=CHUNKS_PER_STEP, final_norm=l == depth - 1)
        shared = (states_p, states_s)
        post_args = (scv, opart, gr, m, wout_b)
        if l == depth - 1:
            y_s = _sample_final(h_s, post_args, fg)
        else:
            h_s, q, k, v, scv, gr, m, vn, gb, win_b, wout_b = start(h_s, (*post_args, vn), layer=l + 1)

    return h_p, y_s, states_p, states_s, vn.reshape(depth, n_rows, 1, HEADS, HD)
```

```python
import functools

import jax
import jax.numpy as jnp
import numpy as np
from jax import lax
from jax.experimental import pallas as pl
from jax.experimental.pallas import tpu as pltpu

F32 = jnp.float32
BF16 = jnp.bfloat16

D_MODEL = 1024
HEADS = 8
HD = 128
LANES, SUBLANES = 128, 8
WIDTH = HEADS * HD
N_SEG = 7
MIX = 2 * WIDTH
CHUNK = 128
PAST_LEN = 16384
ROPE_BASE = 10000.0
EPS = 1e-6
SEG_Q, SEG_K, SEG_V, SEG_GR, SEG_U, SEG_VG, SEG_GG = range(N_SEG)

GAMMA = tuple(1.0 - 2.0 ** (-5.0 - h) for h in range(HEADS))
CHUNK_DECAY = tuple(g ** CHUNK for g in GAMMA)
K_SCALE = HD ** -0.5

CHUNKS_PER_STEP = 4
V7X_VMEM_BYTES = 64 * 1024 * 1024
VMEM_LIMIT_BYTES = V7X_VMEM_BYTES - 2 * 1024 * 1024


def _silu(x):
    return x * (1.0 / (1.0 + jnp.exp(-x)))


def _rms_scale(x):
    return x * lax.rsqrt(jnp.mean(x * x, axis=-1, keepdims=True) + EPS)


def _layernorm(x, g, b):
    mu = jnp.mean(x, axis=-1, keepdims=True)
    xc = x - mu
    var = jnp.mean(xc * xc, axis=-1, keepdims=True)
    return xc * lax.rsqrt(var + EPS) * g + b


def _rope(x, cos, sin):
    return x * cos + pltpu.roll(x, HD // 2, 1) * sin


def _head(h):
    return slice(h * HD, (h + 1) * HD)


def _dot(a, b):
    return jnp.dot(a, b, preferred_element_type=F32)


def _pack(w):
    return pltpu.bitcast(w, jnp.uint32)


def _unpack(words):
    return pltpu.bitcast(words, BF16)


def _const_spec(shape):
    zeros = (0,) * len(shape)
    return pl.BlockSpec(shape, lambda *_: zeros)


def _layer_spec(shape, layer, **kwargs):
    index = (layer,) + (0,) * len(shape)
    return pl.BlockSpec((None,) + shape, lambda *_: index, **kwargs)


def _resident_spec(shape):
    zeros = (0,) * len(shape)
    return pl.BlockSpec(shape, lambda *_: zeros, pipeline_mode=pl.Buffered(1))


def _chunk_rows(c):
    return slice(c * CHUNK, (c + 1) * CHUNK)


def _sample_state_step(q_ref, k_ref, v_ref, gamma_ref, s0_ref, write_new, opart_ref, head0):
    heads = opart_ref.shape[1] // HD
    tiles = [r[:, _head(hh)] for r in (k_ref, q_ref) for hh in range(heads)]
    pad = jnp.zeros((LANES - len(tiles) * SUBLANES, HD), F32)
    cols = jnp.concatenate(tiles + [pad], axis=0).T
    for hh in range(heads):
        for j in range(SUBLANES):
            s_old = s0_ref[j, hh]
            k_lane = hh * SUBLANES + j
            q_lane = heads * SUBLANES + k_lane
            k_col = jnp.broadcast_to(cols[:, k_lane:k_lane + 1], (HD, HD))
            q_col = jnp.broadcast_to(cols[:, q_lane:q_lane + 1], (HD, HD))
            write_new(j, hh, s_old * gamma_ref[head0 + hh] + k_col * v_ref[j:j + 1, _head(hh)])
            opart_ref[j:j + 1, _head(hh)] = jnp.sum(q_col * s_old, axis=0, keepdims=True)


def _prompt_kernel(x_ref, win_ref, wout_ref, rope_ref, qdec_ref, kdec_ref, mask_ref,
                   ws_ref, gb_ref, lng_ref, lnb_ref, fg_ref, gamma_ref, qkv_ref, s0_ref, *refs,
                   layer, final_norm, aliased):
    refs = refs[N_SHARED * aliased:]
    y_ref, sall_ref, opart_ref, snew_ref = refs[:4]
    xb, qdb, kdt, vb, gr, ug, vnb, om, wtril, sb, kvs, scb = refs[4:]
    s_ref = sall_ref if aliased else sall_ref.at[layer]
    lng, lnb = (r[layer:layer + 1, :] for r in (lng_ref, lnb_ref))

    t = pl.program_id(1)
    n_chunks = x_ref.shape[0] // CHUNK

    @pl.when(t == 0)
    def _start_of_row():
        sall_ref[...] = jnp.zeros(sall_ref.shape, F32)
        row = lax.broadcasted_iota(jnp.int32, (CHUNK, CHUNK), 0)
        col = lax.broadcasted_iota(jnp.int32, (CHUNK, CHUNK), 1)
        for h in range(HEADS):
            wtril[h] = jnp.where(row >= col, ws_ref[h], 0.0).astype(BF16)

    def proj(seg, pair=None):
        lo, width = (seg * WIDTH, WIDTH) if pair is None else (seg * WIDTH + pair * 2 * HD, 2 * HD)
        return _dot(xb[...], _unpack(win_ref[:, lo:lo + width]))

    pairs = range(HEADS // 2)
    pair_cols = lambda pair: slice(pair * 2 * HD, (pair + 1) * 2 * HD)

    xb[...] = _rms_scale(x_ref[...]).astype(BF16)

    head_blocks = WIDTH // opart_ref.shape[1]
    head0 = lax.rem(pl.program_id(0) * pl.num_programs(1) + t, head_blocks) * (HEADS // head_blocks)
    if aliased:
        def write_new(j, hh, s_new):
            snew_ref[j, hh] = s_new
    else:
        def write_new(j, hh, s_new):
            for l in range(snew_ref.shape[0]):
                snew_ref[l, j, hh] = s_new if l == layer else jnp.zeros_like(s_new)
    _sample_state_step(*(qkv_ref.at[i] for i in range(3)), gamma_ref, s0_ref, write_new, opart_ref, head0)

    cos, sin = rope_ref[0], rope_ref[1]
    for pair in pairs:
        acc = proj(SEG_Q, pair)
        for i in range(2):
            h = 2 * pair + i
            r = _rope(acc[:, _head(i)], cos, sin)
            for c in range(n_chunks):
                rows = _chunk_rows(c)
                qdb[rows, _head(h)] = (r[rows] * qdec_ref[:, _head(h)]).astype(BF16)

    for pair in pairs:
        acc = proj(SEG_K, pair)
        for i in range(2):
            h = 2 * pair + i
            r = _rope(acc[:, _head(i)], cos, sin)
            for c in range(n_chunks):
                kd = r[_chunk_rows(c)] * kdec_ref[:, _head(h)]
                kdt[c, h] = kd.T.astype(BF16)

    for pair in pairs:
        vb[:, pair_cols(pair)] = proj(SEG_V, pair).astype(BF16)

    for c in range(n_chunks):
        rows = _chunk_rows(c)
        for h in range(HEADS):
            hs = _head(h)
            scb[c, h] = (_dot(qdb[rows, hs], kdt[c, h]) * mask_ref[h]).astype(BF16)
            kvs[c, h] = _dot(kdt[c, h], vb[rows, hs])

    for pair in pairs:
        gr[:, pair_cols(pair)] = _silu(proj(SEG_GR, pair))
    vnb[...] = _layernorm(proj(SEG_VG), lng, lnb).astype(BF16)

    for h in range(HEADS):
        s = s_ref[h]
        for c in range(n_chunks):
            sb[c, h] = s.astype(BF16)
            s = s * CHUNK_DECAY[h] + kvs[c, h]
        s_ref[h] = s

    for c in range(n_chunks):
        rows = _chunk_rows(c)
        for h in range(HEADS):
            hs = _head(h)
            lhs = jnp.concatenate([scb[c, h], qdb[rows, hs]], axis=1)
            rhs = jnp.concatenate([vb[rows, hs], sb[c, h]], axis=0)
            om[rows, hs] = (_rms_scale(_dot(lhs, rhs)) * gr[rows, hs]).astype(BF16)

    for pair in pairs:
        ug[:, pair_cols(pair)] = proj(SEG_U, pair) * _silu(proj(SEG_GG, pair))

    for c in range(n_chunks):
        rows = _chunk_rows(c)
        for h in range(HEADS):
            hs = _head(h)
            s = _dot(wtril[h], vnb[rows, hs]) + gb_ref[:, hs]
            om[rows, WIDTH + h * HD:WIDTH + (h + 1) * HD] = (ug[rows, hs] * s).astype(BF16)

    half = x_ref.shape[0] // 2
    for rows in (slice(0, half), slice(half, 2 * half)):
        y = x_ref[rows, :] + _dot(om[rows, :], _unpack(wout_ref[...]))
        if final_norm:
            y = _rms_scale(y) * fg_ref[...]
        y_ref[rows, :] = y


N_PLAIN_OPERANDS = 15
N_SHARED = 2
OUT_PROMPT_STATES, OUT_SAMPLE_STATES = 1, 3


def _prompt_layer(x, win, wout, rope, qdec, kdec, mask, ws, gb, lng, lnb, fg, qkv, states, shared, *,
                  layer, chunks_per_step, final_norm):
    batch, seq, _ = x.shape
    depth, n_rows = states.shape[:2]
    aliased = shared is not None
    tok = chunks_per_step * CHUNK
    steps_per_row = seq // tok
    heads = n_rows * HEADS // (batch * steps_per_row * SUBLANES)
    head_blocks = HEADS // heads
    assert heads * head_blocks == HEADS and batch * steps_per_row == (n_rows // SUBLANES) * head_blocks
    tile = lambda b, t: ((b * steps_per_row + t) // head_blocks, (b * steps_per_row + t) % head_blocks)

    row_spec = pl.BlockSpec((None, tok, D_MODEL), lambda b, t: (b, t, 0))
    pos_spec = pl.BlockSpec((2, tok, HD), lambda b, t: (0, t, 0))
    tile_spec = pl.BlockSpec((SUBLANES, heads * HD), tile)
    qkv_spec = pl.BlockSpec((3, SUBLANES, heads * HD), lambda b, t: (0, *tile(b, t)))
    state_spec = pl.BlockSpec((None, SUBLANES, heads, HD, HD), lambda b, t: (layer, *tile(b, t), 0, 0))
    all_layers_spec = pl.BlockSpec((depth, SUBLANES, heads, HD, HD), lambda b, t: (0, *tile(b, t), 0, 0))
    if aliased:
        pstate_spec = pl.BlockSpec((None, None, HEADS, HD, HD), lambda b, t: (layer, b, 0, 0, 0))
    else:
        pstate_spec = pl.BlockSpec((depth, None, HEADS, HD, HD), lambda b, t: (0, b, 0, 0, 0))
    gamma = jnp.asarray(np.broadcast_to(np.asarray(GAMMA)[:, None, None], (HEADS, 1, HD)), F32)
    bf16_rows = lambda width: pltpu.VMEM((tok, width), BF16)
    per_chunk_head = lambda dtype: pltpu.VMEM((chunks_per_step, HEADS, CHUNK, CHUNK), dtype)

    return pl.pallas_call(
        functools.partial(_prompt_kernel, layer=layer, final_norm=final_norm, aliased=aliased),
        grid=(batch, steps_per_row),
        in_specs=[
            row_spec,
            _resident_spec((D_MODEL // 2, N_SEG * WIDTH)),
            _resident_spec((MIX // 2, D_MODEL)),
            pos_spec,
            _const_spec((CHUNK, WIDTH)), _const_spec((CHUNK, WIDTH)),
            _const_spec((HEADS, CHUNK, CHUNK)),
            _layer_spec((HEADS, CHUNK, CHUNK), layer),
            _const_spec((CHUNK, WIDTH)),
            _const_spec((depth, WIDTH)), _const_spec((depth, WIDTH)),
            _const_spec((1, D_MODEL)),
            _const_spec((HEADS, 1, HD)),
            qkv_spec,
            state_spec,
            *[pl.BlockSpec(memory_space=pl.ANY)] * (N_SHARED * aliased),
        ],
        out_specs=[row_spec, pstate_spec, tile_spec, state_spec if aliased else all_layers_spec],
        out_shape=[jax.ShapeDtypeStruct(x.shape, F32),
                   jax.ShapeDtypeStruct((depth, batch, HEADS, HD, HD), F32),
                   jax.ShapeDtypeStruct((n_rows, WIDTH), F32),
                   jax.ShapeDtypeStruct(states.shape, F32)],
        input_output_aliases=({N_PLAIN_OPERANDS: OUT_PROMPT_STATES, N_PLAIN_OPERANDS + 1: OUT_SAMPLE_STATES}
                              if aliased else {}),
        scratch_shapes=[
            bf16_rows(D_MODEL),
            bf16_rows(WIDTH),
            per_chunk_head(BF16),
            bf16_rows(WIDTH),
            pltpu.VMEM((tok, WIDTH), F32),
            pltpu.VMEM((tok, WIDTH), F32),
            bf16_rows(WIDTH),
            bf16_rows(MIX),
            pltpu.VMEM((HEADS, CHUNK, CHUNK), BF16),
            per_chunk_head(BF16),
            per_chunk_head(F32),
            per_chunk_head(BF16),
        ],
        compiler_params=pltpu.CompilerParams(
            dimension_semantics=("arbitrary", "arbitrary"),
            vmem_limit_bytes=VMEM_LIMIT_BYTES,
        ),
        name="prompt_layer",
    )(x, win, wout, rope, qdec, kdec, mask, ws, gb, lng, lnb, fg, gamma, qkv, states,
      *(shared if aliased else ()))


N_WOUT_BLOCKS = 8
N_START_STEPS = max(N_SEG, N_WOUT_BLOCKS)


def _finish_layer(h, scv_ref, opart_ref, gr_ref, m_ref, wout_ref):
    parts = []
    for hd in range(HEADS):
        hs = _head(hd)
        o = scv_ref[:, hs] + GAMMA[hd] * opart_ref[:, hs]
        parts.append((_rms_scale(o) * gr_ref[:, hs]).astype(BF16))
    om = jnp.concatenate(parts + [m_ref[...]], axis=1)
    return h + _dot(om, _unpack(wout_ref[...]))


def _sample_start_kernel(*refs, post, layer):
    h_ref, refs = refs[0], refs[1:]
    if post:
        post_refs, vn_prev_ref, refs = refs[:5], refs[5], refs[6:]
    ng_ref, win_ref, wout_ref, cos_ref, sin_ref, ws_ref, gmb_ref, lng_ref, lnb_ref = refs[:9]
    refs = refs[9:]
    if post:
        hout_ref, refs = refs[0], refs[1:]
    qkv_ref, scv_ref, gr_ref, m_ref, vn_ref, gb_ref, winb_ref, woutb_ref, xb, u_keep, vn_keep = refs
    q_ref, k_ref, v_ref = (qkv_ref.at[i] for i in range(3))
    j = pl.program_id(0)

    @pl.when(j == 0)
    def _tokens():
        h = h_ref[...]
        if post:
            h = _finish_layer(h, *post_refs)
            hout_ref[...] = h
        xb[...] = _rms_scale(h).astype(BF16)

    @pl.when(j < N_SEG)
    def _segment():
        gain = ng_ref[layer:layer + 1, :]
        tiles = [gain[:, i * LANES:(i + 1) * LANES] for i in range(D_MODEL // LANES)]
        pad = jnp.zeros((LANES - len(tiles), LANES), F32)
        gain_cols = jnp.concatenate(tiles + [pad], axis=0).T
        for i in range(D_MODEL // LANES):
            rows = slice(i * LANES, (i + 1) * LANES)
            packed_rows = slice(i * LANES // 2, (i + 1) * LANES // 2)
            winb_ref[packed_rows, :] = _pack((win_ref[rows, :] * gain_cols[:, i:i + 1]).astype(BF16))
        acc = _dot(xb[...], _unpack(winb_ref[...]))

        @pl.when(j == SEG_Q)
        def _():
            for hd in range(HEADS):
                q_ref[:, _head(hd)] = _rope(acc[:, _head(hd)], cos_ref[...], sin_ref[...])

        @pl.when(j == SEG_K)
        def _():
            for hd in range(HEADS):
                hs = _head(hd)
                k = _rope(acc[:, hs], cos_ref[...], sin_ref[...]) * K_SCALE
                k_ref[:, hs] = k
                qk = jnp.sum(q_ref[:, hs] * k, axis=-1, keepdims=True)
                scv_ref[:, hs] = jnp.broadcast_to(qk, k.shape)

        @pl.when(j == SEG_V)
        def _():
            v_ref[...] = acc
            scv_ref[...] = scv_ref[...] * acc

        @pl.when(j == SEG_GR)
        def _():
            gr_ref[...] = _silu(acc)

        @pl.when(j == SEG_U)
        def _():
            u_keep[...] = acc

        @pl.when(j == SEG_VG)
        def _():
            vn = _layernorm(acc, lng_ref[layer:layer + 1, :], lnb_ref[layer:layer + 1, :])
            vn_keep[...] = vn
            if post:
                vn_ref[0:layer] = vn_prev_ref[...]
            vn_ref[layer] = jnp.swapaxes(jnp.stack([vn[:, _head(hd)] for hd in range(HEADS)], axis=0), 0, 1)

        @pl.when(j == SEG_GG)
        def _():
            for hd in range(HEADS):
                hs = _head(hd)
                s = vn_keep[:, hs] * ws_ref[hd, 0:1, 0:1] + gmb_ref[layer, hd:hd + 1, 0:1]
                m_ref[:, hs] = (u_keep[:, hs] * _silu(acc[:, hs]) * s).astype(BF16)

    woutb_ref[...] = _pack(wout_ref[...].astype(BF16))

    @pl.when(j == pl.num_programs(0) - 1)
    def _bias_table():
        b = gmb_ref[layer]
        cols = jnp.concatenate([b, jnp.zeros((LANES - HEADS, CHUNK), F32)], axis=0).T
        for hd in range(HEADS):
            gb_ref[:, _head(hd)] = jnp.broadcast_to(cols[:, hd:hd + 1], (CHUNK, HD))


def _sample_final_kernel(h_ref, scv_ref, opart_ref, gr_ref, m_ref, wout_ref, fg_ref, y_ref):
    y_ref[...] = _rms_scale(_finish_layer(h_ref[...], scv_ref, opart_ref, gr_ref, m_ref, wout_ref)) * fg_ref[...]


def _token_spec(n_rows):
    return pl.BlockSpec((n_rows, None, D_MODEL), lambda *_: (0, 0, 0))


def _sample_start(h, post_args, ng, w_in, w_out, rope, ws, gmb, lng, lnb, *, layer):
    n_rows, depth = h.shape[0], w_in.shape[0]
    post = post_args is not None
    assert post == (layer > 0)
    vn_stack = (layer + 1, n_rows, HEADS, HD)
    full = _const_spec((n_rows, WIDTH))
    vecs = _const_spec((depth, WIDTH))
    seg = lambda j: jnp.minimum(j, N_SEG - 1)
    out_block = lambda j: jnp.minimum(j, N_WOUT_BLOCKS - 1)
    f32_rows = jax.ShapeDtypeStruct((n_rows, WIDTH), F32)
    args = [h] + (list(post_args) if post else []) + [ng, w_in, w_out, *rope, ws, gmb, lng, lnb]
    in_specs = [_token_spec(n_rows) if h.ndim == 3 else full]
    if post:
        in_specs += [full, full, full, full, _resident_spec((MIX // 2, D_MODEL)), _const_spec((layer, *vn_stack[1:]))]
    in_specs += [vecs,
                 pl.BlockSpec((None, D_MODEL, WIDTH), lambda j: (layer, 0, seg(j))),
                 pl.BlockSpec((None, MIX // N_WOUT_BLOCKS, D_MODEL), lambda j: (layer, out_block(j), 0)),
                 _const_spec((1, HD)), _const_spec((1, HD)),
                 _layer_spec((HEADS, CHUNK, CHUNK), layer), _const_spec((depth, HEADS, CHUNK)), vecs, vecs]
    out_shape = (([f32_rows] if post else []) + [jax.ShapeDtypeStruct((3, n_rows, WIDTH), F32)] + [f32_rows] * 2
                 + [jax.ShapeDtypeStruct((n_rows, WIDTH), BF16), jax.ShapeDtypeStruct(vn_stack, F32),
                    jax.ShapeDtypeStruct((CHUNK, WIDTH), F32),
                    jax.ShapeDtypeStruct((D_MODEL // 2, N_SEG * WIDTH), jnp.uint32),
                    jax.ShapeDtypeStruct((MIX // 2, D_MODEL), jnp.uint32)])
    out_specs = [full] * post + [_const_spec((3, n_rows, WIDTH))] + [full] * 3 + [_const_spec(vn_stack), _const_spec((CHUNK, WIDTH)),
                                                 pl.BlockSpec((D_MODEL // 2, WIDTH), lambda j: (0, seg(j))),
                                                 pl.BlockSpec((MIX // N_WOUT_BLOCKS // 2, D_MODEL), lambda j: (out_block(j), 0))]
    return pl.pallas_call(
        functools.partial(_sample_start_kernel, post=post, layer=layer),
        grid=(N_START_STEPS,),
        in_specs=in_specs,
        out_specs=out_specs,
        out_shape=out_shape,
        scratch_shapes=[pltpu.VMEM((n_rows, D_MODEL), BF16),
                        pltpu.VMEM((n_rows, WIDTH), F32),
                        pltpu.VMEM((n_rows, WIDTH), F32)],
        compiler_params=pltpu.CompilerParams(
            dimension_semantics=("arbitrary",),
            vmem_limit_bytes=VMEM_LIMIT_BYTES,
        ),
        name="sample_start",
    )(*args)


def _sample_final(h, post_args, fg):
    n_rows = h.shape[0]
    full = _const_spec((n_rows, WIDTH))
    return pl.pallas_call(
        _sample_final_kernel,
        grid=(1,),
        in_specs=[full, full, full, full, full, _const_spec((MIX // 2, D_MODEL)), _const_spec((1, D_MODEL))],
        out_specs=_token_spec(n_rows),
        out_shape=jax.ShapeDtypeStruct((n_rows, 1, D_MODEL), F32),
        compiler_params=pltpu.CompilerParams(
            dimension_semantics=("arbitrary",),
            vmem_limit_bytes=VMEM_LIMIT_BYTES,
        ),
        name="sample_final",
    )(h, *post_args, fg)


def _rope_tables(pos):
    inv = ROPE_BASE ** (-np.arange(0, HD, 2, dtype=np.float64) / HD)
    ang = np.asarray(pos, np.float64)[:, None] * inv[None, :]
    c, s = np.cos(ang), np.sin(ang)
    return (jnp.asarray(np.concatenate([c, c], axis=-1), F32),
            jnp.asarray(np.concatenate([-s, s], axis=-1), F32))


def _retention_tables():
    lg = np.log(np.asarray(GAMMA, np.float64))
    idx = np.arange(CHUNK, dtype=np.float64)
    causal = idx[:, None] >= idx[None, :]
    mask = np.where(causal[None], np.exp(-lg * CHUNK)[:, None, None], 0.0)
    q_dec = np.exp(lg[None, :] * (idx[:, None] + 1.0))
    k_dec = np.exp(lg[None, :] * (CHUNK - 1.0 - idx[:, None])) * K_SCALE
    per_lane = lambda a: np.repeat(a, HD, axis=1)
    return jnp.asarray(mask, F32), jnp.asarray(per_lane(q_dec), F32), jnp.asarray(per_lane(k_dec), F32)


def kernel(x_prompt, x_sample, state_ret, norm_g, w_in, w_out, gm_ws, gm_b, gm_ln_g, gm_ln_b, final_g):
    depth = w_in.shape[0]
    batch, seq, _ = x_prompt.shape
    n_rows = x_sample.shape[0]
    assert x_sample.shape[1] == 1

    rope_p = jnp.stack(_rope_tables(np.arange(seq)))
    rope_s = _rope_tables(PAST_LEN + np.arange(1))
    mask, q_dec, k_dec = _retention_tables()
    fg = final_g.reshape(1, D_MODEL)
    start = functools.partial(_sample_start, ng=norm_g, w_in=w_in, w_out=w_out, rope=rope_s, ws=gm_ws, gmb=gm_b,
                              lng=gm_ln_g, lnb=gm_ln_b)

    h_p, h_s, shared = x_prompt, x_sample, None
    qkv, scv, gr, m, vn, gb, win_b, wout_b = start(h_s, None, layer=0)
    for l in range(depth):
        h_p, states_p, opart, states_s = _prompt_layer(
            h_p, win_b, wout_b, rope_p, q_dec, k_dec, mask, gm_ws, gb, gm_ln_g, gm_ln_b, fg,
            qkv, state_ret, shared, layer=l, chunks_per_step=CHUNKS_PER_STEP, final_norm=l == depth - 1)
        shared = (states_p, states_s)
        post_args = (scv, opart, gr, m, wout_b)
        if l == depth - 1:
            y_s = _sample_final(h_s, post_args, fg)
        else:
            h_s, qkv, scv, gr, m, vn, gb, win_b, wout_b = start(h_s, (*post_args, vn), layer=l + 1)

    return h_p, y_s, states_p, states_s, vn.reshape(depth, n_rows, 1, HEADS, HD)
```

```python
import functools

import jax
import jax.numpy as jnp
import numpy as np
from jax import lax
from jax.experimental import pallas as pl
from jax.experimental.pallas import tpu as pltpu

F32 = jnp.float32
BF16 = jnp.bfloat16

D_MODEL = 1024
HEADS = 8
HD = 128
LANES, SUBLANES = 128, 8
WIDTH = HEADS * HD
N_SEG = 7
MIX = 2 * WIDTH
CHUNK = 128
PAST_LEN = 16384
ROPE_BASE = 10000.0
EPS = 1e-6
SEG_Q, SEG_K, SEG_V, SEG_GR, SEG_U, SEG_VG, SEG_GG = range(N_SEG)

GAMMA = tuple(1.0 - 2.0 ** (-5.0 - h) for h in range(HEADS))
CHUNK_DECAY = tuple(g ** CHUNK for g in GAMMA)
K_SCALE = HD ** -0.5

CHUNKS_PER_STEP = 4
V7X_VMEM_BYTES = 64 * 1024 * 1024
VMEM_LIMIT_BYTES = V7X_VMEM_BYTES - 2 * 1024 * 1024


def _silu(x):
    return x * (1.0 / (1.0 + jnp.exp(-x)))


def _rms_scale(x):
    return x * lax.rsqrt(jnp.mean(x * x, axis=-1, keepdims=True) + EPS)


def _layernorm(x, g, b):
    mu = jnp.mean(x, axis=-1, keepdims=True)
    xc = x - mu
    var = jnp.mean(xc * xc, axis=-1, keepdims=True)
    return xc * lax.rsqrt(var + EPS) * g + b


def _rope(x, cos, sin):
    return x * cos + pltpu.roll(x, HD // 2, 1) * sin


def _head(h):
    return slice(h * HD, (h + 1) * HD)


def _dot(a, b):
    return jnp.dot(a, b, preferred_element_type=F32)


def _pack(w):
    return pltpu.bitcast(w, jnp.uint32)


def _unpack(words):
    return pltpu.bitcast(words, BF16)


def _const_spec(shape):
    zeros = (0,) * len(shape)
    return pl.BlockSpec(shape, lambda *_: zeros)


def _layer_spec(shape, layer, **kwargs):
    index = (layer,) + (0,) * len(shape)
    return pl.BlockSpec((None,) + shape, lambda *_: index, **kwargs)


def _resident_spec(shape):
    zeros = (0,) * len(shape)
    return pl.BlockSpec(shape, lambda *_: zeros, pipeline_mode=pl.Buffered(1))


def _chunk_rows(c):
    return slice(c * CHUNK, (c + 1) * CHUNK)


def _sample_state_step(q_ref, k_ref, v_ref, gamma_ref, s0_ref, write_new, opart_ref, head0):
    heads = opart_ref.shape[1] // HD
    tiles = [r[:, _head(hh)] for r in (k_ref, q_ref) for hh in range(heads)]
    pad = jnp.zeros((LANES - len(tiles) * SUBLANES, HD), F32)
    cols = jnp.concatenate(tiles + [pad], axis=0).T
    for hh in range(heads):
        for j in range(SUBLANES):
            s_old = s0_ref[j, hh]
            k_lane = hh * SUBLANES + j
            q_lane = heads * SUBLANES + k_lane
            k_col = jnp.broadcast_to(cols[:, k_lane:k_lane + 1], (HD, HD))
            q_col = jnp.broadcast_to(cols[:, q_lane:q_lane + 1], (HD, HD))
            write_new(j, hh, s_old * gamma_ref[head0 + hh] + k_col * v_ref[j:j + 1, _head(hh)])
            opart_ref[j:j + 1, _head(hh)] = jnp.sum(q_col * s_old, axis=0, keepdims=True)


def _prompt_kernel(x_ref, win_ref, wout_ref, rope_ref, qdec_ref, kdec_ref, mask_ref,
                   ws_ref, gb_ref, lng_ref, lnb_ref, fg_ref, gamma_ref, qkv_ref, s0_ref, *refs,
                   layer, final_norm, aliased):
    refs = refs[N_SHARED * aliased:]
    y_ref, sall_ref, opart_ref, snew_ref = refs[:4]
    xb, qdb, kdt, vb, gr, ug, vnb, om, wtril, sb, scb = refs[4:]
    s_ref = sall_ref if aliased else sall_ref.at[layer]
    lng, lnb = (r[layer:layer + 1, :] for r in (lng_ref, lnb_ref))

    t = pl.program_id(1)
    n_chunks = x_ref.shape[0] // CHUNK

    @pl.when(t == 0)
    def _start_of_row():
        sall_ref[...] = jnp.zeros(sall_ref.shape, F32)
        row = lax.broadcasted_iota(jnp.int32, (CHUNK, CHUNK), 0)
        col = lax.broadcasted_iota(jnp.int32, (CHUNK, CHUNK), 1)
        for h in range(HEADS):
            wtril[h] = jnp.where(row >= col, ws_ref[h], 0.0).astype(BF16)

    def proj(seg, pair=None):
        lo, width = (seg * WIDTH, WIDTH) if pair is None else (seg * WIDTH + pair * 2 * HD, 2 * HD)
        return _dot(xb[...], _unpack(win_ref[:, lo:lo + width]))

    pairs = range(HEADS // 2)
    pair_cols = lambda pair: slice(pair * 2 * HD, (pair + 1) * 2 * HD)

    xb[...] = _rms_scale(x_ref[...]).astype(BF16)

    head_blocks = WIDTH // opart_ref.shape[1]
    head0 = lax.rem(pl.program_id(0) * pl.num_programs(1) + t, head_blocks) * (HEADS // head_blocks)
    if aliased:
        def write_new(j, hh, s_new):
            snew_ref[j, hh] = s_new
    else:
        def write_new(j, hh, s_new):
            for l in range(snew_ref.shape[0]):
                snew_ref[l, j, hh] = s_new if l == layer else jnp.zeros_like(s_new)
    _sample_state_step(*(qkv_ref.at[i] for i in range(3)), gamma_ref, s0_ref, write_new, opart_ref, head0)

    cos, sin = rope_ref[0], rope_ref[1]
    for pair in pairs:
        acc = proj(SEG_Q, pair)
        for i in range(2):
            h = 2 * pair + i
            r = _rope(acc[:, _head(i)], cos, sin)
            for c in range(n_chunks):
                rows = _chunk_rows(c)
                qdb[rows, _head(h)] = (r[rows] * qdec_ref[:, _head(h)]).astype(BF16)

    for pair in pairs:
        acc = proj(SEG_K, pair)
        for i in range(2):
            h = 2 * pair + i
            r = _rope(acc[:, _head(i)], cos, sin)
            for c in range(n_chunks):
                kd = r[_chunk_rows(c)] * kdec_ref[:, _head(h)]
                kdt[c, h] = kd.T.astype(BF16)

    for pair in pairs:
        vb[:, pair_cols(pair)] = proj(SEG_V, pair).astype(BF16)

    for h in range(HEADS):
        hs = _head(h)
        s = s_ref[h]
        for c in range(n_chunks):
            rows = _chunk_rows(c)
            scb[c, h] = (_dot(qdb[rows, hs], kdt[c, h]) * mask_ref[h]).astype(BF16)
            sb[c, h] = s.astype(BF16)
            s = s * CHUNK_DECAY[h] + _dot(kdt[c, h], vb[rows, hs])
        s_ref[h] = s

    for pair in pairs:
        gr[:, pair_cols(pair)] = _silu(proj(SEG_GR, pair))
    vnb[...] = _layernorm(proj(SEG_VG), lng, lnb).astype(BF16)

    for c in range(n_chunks):
        rows = _chunk_rows(c)
        for h in range(HEADS):
            hs = _head(h)
            lhs = jnp.concatenate([scb[c, h], qdb[rows, hs]], axis=1)
            rhs = jnp.concatenate([vb[rows, hs], sb[c, h]], axis=0)
            om[rows, hs] = (_rms_scale(_dot(lhs, rhs)) * gr[rows, hs]).astype(BF16)

    for pair in pairs:
        ug[:, pair_cols(pair)] = proj(SEG_U, pair) * _silu(proj(SEG_GG, pair))

    for c in range(n_chunks):
        rows = _chunk_rows(c)
        for h in range(HEADS):
            hs = _head(h)
            s = _dot(wtril[h], vnb[rows, hs]) + gb_ref[:, hs]
            om[rows, WIDTH + h * HD:WIDTH + (h + 1) * HD] = (ug[rows, hs] * s).astype(BF16)

    half = x_ref.shape[0] // 2
    for rows in (slice(0, half), slice(half, 2 * half)):
        y = x_ref[rows, :] + _dot(om[rows, :], _unpack(wout_ref[...]))
        if final_norm:
            y = _rms_scale(y) * fg_ref[...]
        y_ref[rows, :] = y


N_PLAIN_OPERANDS = 15
N_SHARED = 2
OUT_PROMPT_STATES, OUT_SAMPLE_STATES = 1, 3


def _prompt_layer(x, win, wout, rope, qdec, kdec, mask, ws, gb, lng, lnb, fg, qkv, states, shared, *,
                  layer, chunks_per_step, final_norm):
    batch, seq, _ = x.shape
    depth, n_rows = states.shape[:2]
    aliased = shared is not None
    tok = chunks_per_step * CHUNK
    steps_per_row = seq // tok
    heads = n_rows * HEADS // (batch * steps_per_row * SUBLANES)
    head_blocks = HEADS // heads
    assert heads * head_blocks == HEADS and batch * steps_per_row == (n_rows // SUBLANES) * head_blocks
    tile = lambda b, t: ((b * steps_per_row + t) // head_blocks, (b * steps_per_row + t) % head_blocks)

    row_spec = pl.BlockSpec((None, tok, D_MODEL), lambda b, t: (b, t, 0))
    pos_spec = pl.BlockSpec((2, tok, HD), lambda b, t: (0, t, 0))
    tile_spec = pl.BlockSpec((SUBLANES, heads * HD), tile)
    qkv_spec = pl.BlockSpec((3, SUBLANES, heads * HD), lambda b, t: (0, *tile(b, t)))
    state_spec = pl.BlockSpec((None, SUBLANES, heads, HD, HD), lambda b, t: (layer, *tile(b, t), 0, 0))
    all_layers_spec = pl.BlockSpec((depth, SUBLANES, heads, HD, HD), lambda b, t: (0, *tile(b, t), 0, 0))
    if aliased:
        pstate_spec = pl.BlockSpec((None, None, HEADS, HD, HD), lambda b, t: (layer, b, 0, 0, 0))
    else:
        pstate_spec = pl.BlockSpec((depth, None, HEADS, HD, HD), lambda b, t: (0, b, 0, 0, 0))
    gamma = jnp.asarray(np.broadcast_to(np.asarray(GAMMA)[:, None, None], (HEADS, 1, HD)), F32)
    bf16_rows = lambda width: pltpu.VMEM((tok, width), BF16)
    per_chunk_head = lambda dtype: pltpu.VMEM((chunks_per_step, HEADS, CHUNK, CHUNK), dtype)

    return pl.pallas_call(
        functools.partial(_prompt_kernel, layer=layer, final_norm=final_norm, aliased=aliased),
        grid=(batch, steps_per_row),
        in_specs=[
            row_spec,
            _resident_spec((D_MODEL // 2, N_SEG * WIDTH)),
            _resident_spec((MIX // 2, D_MODEL)),
            pos_spec,
            _const_spec((CHUNK, WIDTH)), _const_spec((CHUNK, WIDTH)),
            _const_spec((HEADS, CHUNK, CHUNK)),
            _layer_spec((HEADS, CHUNK, CHUNK), layer),
            _const_spec((CHUNK, WIDTH)),
            _const_spec((depth, WIDTH)), _const_spec((depth, WIDTH)),
            _const_spec((1, D_MODEL)),
            _const_spec((HEADS, 1, HD)),
            qkv_spec,
            state_spec,
            *[pl.BlockSpec(memory_space=pl.ANY)] * (N_SHARED * aliased),
        ],
        out_specs=[row_spec, pstate_spec, tile_spec, state_spec if aliased else all_layers_spec],
        out_shape=[jax.ShapeDtypeStruct(x.shape, F32),
                   jax.ShapeDtypeStruct((depth, batch, HEADS, HD, HD), F32),
                   jax.ShapeDtypeStruct((n_rows, WIDTH), F32),
                   jax.ShapeDtypeStruct(states.shape, F32)],
        input_output_aliases=({N_PLAIN_OPERANDS: OUT_PROMPT_STATES, N_PLAIN_OPERANDS + 1: OUT_SAMPLE_STATES}
                              if aliased else {}),
        scratch_shapes=[
            bf16_rows(D_MODEL),
            bf16_rows(WIDTH),
            per_chunk_head(BF16),
            bf16_rows(WIDTH),
            pltpu.VMEM((tok, WIDTH), F32),
            pltpu.VMEM((tok, WIDTH), F32),
            bf16_rows(WIDTH),
            bf16_rows(MIX),
            pltpu.VMEM((HEADS, CHUNK, CHUNK), BF16),
            per_chunk_head(BF16),
            per_chunk_head(BF16),
        ],
        compiler_params=pltpu.CompilerParams(
            dimension_semantics=("arbitrary", "arbitrary"),
            vmem_limit_bytes=VMEM_LIMIT_BYTES,
        ),
        name="prompt_layer",
    )(x, win, wout, rope, qdec, kdec, mask, ws, gb, lng, lnb, fg, gamma, qkv, states,
      *(shared if aliased else ()))


N_WOUT_BLOCKS = 8
N_START_STEPS = max(N_SEG, N_WOUT_BLOCKS)


def _finish_layer(h, scv_ref, opart_ref, gr_ref, m_ref, wout_ref):
    parts = []
    for hd in range(HEADS):
        hs = _head(hd)
        o = scv_ref[:, hs] + GAMMA[hd] * opart_ref[:, hs]
        parts.append((_rms_scale(o) * gr_ref[:, hs]).astype(BF16))
    om = jnp.concatenate(parts + [m_ref[...]], axis=1)
    return h + _dot(om, _unpack(wout_ref[...]))


def _sample_start_kernel(*refs, post, layer):
    h_ref, refs = refs[0], refs[1:]
    if post:
        post_refs, vn_prev_ref, refs = refs[:5], refs[5], refs[6:]
    ng_ref, win_ref, wout_ref, cos_ref, sin_ref, ws_ref, gmb_ref, lng_ref, lnb_ref = refs[:9]
    refs = refs[9:]
    if post:
        hout_ref, refs = refs[0], refs[1:]
    qkv_ref, scv_ref, gr_ref, m_ref, vn_ref, gb_ref, winb_ref, woutb_ref, xb, u_keep, vn_keep = refs
    q_ref, k_ref, v_ref = (qkv_ref.at[i] for i in range(3))
    j = pl.program_id(0)

    @pl.when(j == 0)
    def _tokens():
        h = h_ref[...]
        if post:
            h = _finish_layer(h, *post_refs)
            hout_ref[...] = h
        xb[...] = _rms_scale(h).astype(BF16)

    @pl.when(j < N_SEG)
    def _segment():
        gain = ng_ref[layer:layer + 1, :]
        tiles = [gain[:, i * LANES:(i + 1) * LANES] for i in range(D_MODEL // LANES)]
        pad = jnp.zeros((LANES - len(tiles), LANES), F32)
        gain_cols = jnp.concatenate(tiles + [pad], axis=0).T
        for i in range(D_MODEL // LANES):
            rows = slice(i * LANES, (i + 1) * LANES)
            packed_rows = slice(i * LANES // 2, (i + 1) * LANES // 2)
            winb_ref[packed_rows, :] = _pack((win_ref[rows, :] * gain_cols[:, i:i + 1]).astype(BF16))
        acc = _dot(xb[...], _unpack(winb_ref[...]))

        @pl.when(j == SEG_Q)
        def _():
            for hd in range(HEADS):
                q_ref[:, _head(hd)] = _rope(acc[:, _head(hd)], cos_ref[...], sin_ref[...])

        @pl.when(j == SEG_K)
        def _():
            for hd in range(HEADS):
                hs = _head(hd)
                k = _rope(acc[:, hs], cos_ref[...], sin_ref[...]) * K_SCALE
                k_ref[:, hs] = k
                qk = jnp.sum(q_ref[:, hs] * k, axis=-1, keepdims=True)
                scv_ref[:, hs] = jnp.broadcast_to(qk, k.shape)

        @pl.when(j == SEG_V)
        def _():
            v_ref[...] = acc
            scv_ref[...] = scv_ref[...] * acc

        @pl.when(j == SEG_GR)
        def _():
            gr_ref[...] = _silu(acc)

        @pl.when(j == SEG_U)
        def _():
            u_keep[...] = acc

        @pl.when(j == SEG_VG)
        def _():
            vn = _layernorm(acc, lng_ref[layer:layer + 1, :], lnb_ref[layer:layer + 1, :])
            vn_keep[...] = vn
            if post:
                vn_ref[0:layer] = vn_prev_ref[...]
            vn_ref[layer] = jnp.swapaxes(jnp.stack([vn[:, _head(hd)] for hd in range(HEADS)], axis=0), 0, 1)

        @pl.when(j == SEG_GG)
        def _():
            for hd in range(HEADS):
                hs = _head(hd)
                s = vn_keep[:, hs] * ws_ref[hd, 0:1, 0:1] + gmb_ref[layer, hd:hd + 1, 0:1]
                m_ref[:, hs] = (u_keep[:, hs] * _silu(acc[:, hs]) * s).astype(BF16)

    woutb_ref[...] = _pack(wout_ref[...].astype(BF16))

    @pl.when(j == pl.num_programs(0) - 1)
    def _bias_table():
        b = gmb_ref[layer]
        cols = jnp.concatenate([b, jnp.zeros((LANES - HEADS, CHUNK), F32)], axis=0).T
        for hd in range(HEADS):
            gb_ref[:, _head(hd)] = jnp.broadcast_to(cols[:, hd:hd + 1], (CHUNK, HD))


def _sample_final_kernel(h_ref, scv_ref, opart_ref, gr_ref, m_ref, wout_ref, fg_ref, y_ref):
    y_ref[...] = _rms_scale(_finish_layer(h_ref[...], scv_ref, opart_ref, gr_ref, m_ref, wout_ref)) * fg_ref[...]


def _token_spec(n_rows):
    return pl.BlockSpec((n_rows, None, D_MODEL), lambda *_: (0, 0, 0))


def _sample_start(h, post_args, ng, w_in, w_out, rope, ws, gmb, lng, lnb, *, layer):
    n_rows, depth = h.shape[0], w_in.shape[0]
    post = post_args is not None
    assert post == (layer > 0)
    vn_stack = (layer + 1, n_rows, HEADS, HD)
    full = _const_spec((n_rows, WIDTH))
    vecs = _const_spec((depth, WIDTH))
    seg = lambda j: jnp.minimum(j, N_SEG - 1)
    out_block = lambda j: jnp.minimum(j, N_WOUT_BLOCKS - 1)
    f32_rows = jax.ShapeDtypeStruct((n_rows, WIDTH), F32)
    args = [h] + (list(post_args) if post else []) + [ng, w_in, w_out, *rope, ws, gmb, lng, lnb]
    in_specs = [_token_spec(n_rows) if h.ndim == 3 else full]
    if post:
        in_specs += [full, full, full, full, _resident_spec((MIX // 2, D_MODEL)), _const_spec((layer, *vn_stack[1:]))]
    in_specs += [vecs,
                 pl.BlockSpec((None, D_MODEL, WIDTH), lambda j: (layer, 0, seg(j))),
                 pl.BlockSpec((None, MIX // N_WOUT_BLOCKS, D_MODEL), lambda j: (layer, out_block(j), 0)),
                 _const_spec((1, HD)), _const_spec((1, HD)),
                 _layer_spec((HEADS, CHUNK, CHUNK), layer), _const_spec((depth, HEADS, CHUNK)), vecs, vecs]
    out_shape = (([f32_rows] if post else []) + [jax.ShapeDtypeStruct((3, n_rows, WIDTH), F32)] + [f32_rows] * 2
                 + [jax.ShapeDtypeStruct((n_rows, WIDTH), BF16), jax.ShapeDtypeStruct(vn_stack, F32),
                    jax.ShapeDtypeStruct((CHUNK, WIDTH), F32),
                    jax.ShapeDtypeStruct((D_MODEL // 2, N_SEG * WIDTH), jnp.uint32),
                    jax.ShapeDtypeStruct((MIX // 2, D_MODEL), jnp.uint32)])
    out_specs = [full] * post + [_const_spec((3, n_rows, WIDTH))] + [full] * 3 + [_const_spec(vn_stack), _const_spec((CHUNK, WIDTH)),
                                                 pl.BlockSpec((D_MODEL // 2, WIDTH), lambda j: (0, seg(j))),
                                                 pl.BlockSpec((MIX // N_WOUT_BLOCKS // 2, D_MODEL), lambda j: (out_block(j), 0))]
    return pl.pallas_call(
        functools.partial(_sample_start_kernel, post=post, layer=layer),
        grid=(N_START_STEPS,),
        in_specs=in_specs,
        out_specs=out_specs,
        out_shape=out_shape,
        scratch_shapes=[pltpu.VMEM((n_rows, D_MODEL), BF16),
                        pltpu.VMEM((n_rows, WIDTH), F32),
                        pltpu.VMEM((n_rows, WIDTH), F32)],
        compiler_params=pltpu.CompilerParams(
            dimension_semantics=("arbitrary",),
            vmem_limit_bytes=VMEM_LIMIT_BYTES,
        ),
        name="sample_start",
    )(*args)


def _sample_final(h, post_args, fg):
    n_rows = h.shape[0]
    full = _const_spec((n_rows, WIDTH))
    return pl.pallas_call(
        _sample_final_kernel,
        grid=(1,),
        in_specs=[full, full, full, full, full, _const_spec((MIX // 2, D_MODEL)), _const_spec((1, D_MODEL))],
        out_specs=_token_spec(n_rows),
        out_shape=jax.ShapeDtypeStruct((n_rows, 1, D_MODEL), F32),
        compiler_params=pltpu.CompilerParams(
            dimension_semantics=("arbitrary",),
            vmem_limit_bytes=VMEM_LIMIT_BYTES,
        ),
        name="sample_final",
    )(h, *post_args, fg)


def _rope_tables(pos):
    inv = ROPE_BASE ** (-np.arange(0, HD, 2, dtype=np.float64) / HD)
    ang = np.asarray(pos, np.float64)[:, None] * inv[None, :]
    c, s = np.cos(ang), np.sin(ang)
    return (jnp.asarray(np.concatenate([c, c], axis=-1), F32),
            jnp.asarray(np.concatenate([-s, s], axis=-1), F32))


def _retention_tables():
    lg = np.log(np.asarray(GAMMA, np.float64))
    idx = np.arange(CHUNK, dtype=np.float64)
    causal = idx[:, None] >= idx[None, :]
    mask = np.where(causal[None], np.exp(-lg * CHUNK)[:, None, None], 0.0)
    q_dec = np.exp(lg[None, :] * (idx[:, None] + 1.0))
    k_dec = np.exp(lg[None, :] * (CHUNK - 1.0 - idx[:, None])) * K_SCALE
    per_lane = lambda a: np.repeat(a, HD, axis=1)
    return jnp.asarray(mask, F32), jnp.asarray(per_lane(q_dec), F32), jnp.asarray(per_lane(k_dec), F32)


def kernel(x_prompt, x_sample, state_ret, norm_g, w_in, w_out, gm_ws, gm_b, gm_ln_g, gm_ln_b, final_g):
    depth = w_in.shape[0]
    batch, seq, _ = x_prompt.shape
    n_rows = x_sample.shape[0]
    assert x_sample.shape[1] == 1

    rope_p = jnp.stack(_rope_tables(np.arange(seq)))
    rope_s = _rope_tables(PAST_LEN + np.arange(1))
    mask, q_dec, k_dec = _retention_tables()
    fg = final_g.reshape(1, D_MODEL)
    start = functools.partial(_sample_start, ng=norm_g, w_in=w_in, w_out=w_out, rope=rope_s, ws=gm_ws, gmb=gm_b,
                              lng=gm_ln_g, lnb=gm_ln_b)

    h_p, h_s, shared = x_prompt, x_sample, None
    qkv, scv, gr, m, vn, gb, win_b, wout_b = start(h_s, None, layer=0)
    for l in range(depth):
        h_p, states_p, opart, states_s = _prompt_layer(
            h_p, win_b, wout_b, rope_p, q_dec, k_dec, mask, gm_ws, gb, gm_ln_g, gm_ln_b, fg,
            qkv, state_ret, shared, layer=l, chunks_per_step=CHUNKS_PER_STEP, final_norm=l == depth - 1)
        shared = (states_p, states_s)
        post_args = (scv, opart, gr, m, wout_b)
        if l == depth - 1:
            y_s = _sample_final(h_s, post_args, fg)
        else:
            h_s, qkv, scv, gr, m, vn, gb, win_b, wout_b = start(h_s, (*post_args, vn), layer=l + 1)

    return h_p, y_s, states_p, states_s, vn.reshape(depth, n_rows, 1, HEADS, HD)
```

```python
import functools

import jax
import jax.numpy as jnp
import numpy as np
from jax import lax
from jax.experimental import pallas as pl
from jax.experimental.pallas import tpu as pltpu

F32 = jnp.float32
BF16 = jnp.bfloat16

D_MODEL = 1024
HEADS = 8
HD = 128
LANES, SUBLANES = 128, 8
WIDTH = HEADS * HD
N_SEG = 7
MIX = 2 * WIDTH
CHUNK = 128
PAST_LEN = 16384
ROPE_BASE = 10000.0
EPS = 1e-6
SEG_Q, SEG_K, SEG_V, SEG_GR, SEG_U, SEG_VG, SEG_GG = range(N_SEG)

GAMMA = tuple(1.0 - 2.0 ** (-5.0 - h) for h in range(HEADS))
CHUNK_DECAY = tuple(g ** CHUNK for g in GAMMA)
K_SCALE = HD ** -0.5

CHUNKS_PER_STEP = 4
V7X_VMEM_BYTES = 64 * 1024 * 1024
VMEM_LIMIT_BYTES = V7X_VMEM_BYTES - 2 * 1024 * 1024


def _silu(x):
    return x * (1.0 / (1.0 + jnp.exp(-x)))


def _rms_scale(x):
    return x * lax.rsqrt(jnp.mean(x * x, axis=-1, keepdims=True) + EPS)


def _layernorm(x, g, b):
    mu = jnp.mean(x, axis=-1, keepdims=True)
    xc = x - mu
    var = jnp.mean(xc * xc, axis=-1, keepdims=True)
    return xc * lax.rsqrt(var + EPS) * g + b


def _rope(x, cos, sin):
    return x * cos + pltpu.roll(x, HD // 2, 1) * sin


def _head(h):
    return slice(h * HD, (h + 1) * HD)


def _dot(a, b):
    return jnp.dot(a, b, preferred_element_type=F32)


def _pack(w):
    return pltpu.bitcast(w, jnp.uint32)


def _unpack(words):
    return pltpu.bitcast(words, BF16)


def _const_spec(shape):
    zeros = (0,) * len(shape)
    return pl.BlockSpec(shape, lambda *_: zeros)


def _layer_spec(shape, layer, **kwargs):
    index = (layer,) + (0,) * len(shape)
    return pl.BlockSpec((None,) + shape, lambda *_: index, **kwargs)


def _resident_spec(shape):
    zeros = (0,) * len(shape)
    return pl.BlockSpec(shape, lambda *_: zeros, pipeline_mode=pl.Buffered(1))


def _chunk_rows(c):
    return slice(c * CHUNK, (c + 1) * CHUNK)


def _sample_state_step(q_ref, k_ref, v_ref, gamma_ref, s0_ref, write_new, opart_ref, head0):
    heads = opart_ref.shape[1] // HD
    tiles = [r[:, _head(hh)] for r in (k_ref, q_ref) for hh in range(heads)]
    pad = jnp.zeros((LANES - len(tiles) * SUBLANES, HD), F32)
    cols = jnp.concatenate(tiles + [pad], axis=0).T
    for hh in range(heads):
        for j in range(SUBLANES):
            s_old = s0_ref[j, hh]
            k_lane = hh * SUBLANES + j
            q_lane = heads * SUBLANES + k_lane
            k_col = jnp.broadcast_to(cols[:, k_lane:k_lane + 1], (HD, HD))
            q_col = jnp.broadcast_to(cols[:, q_lane:q_lane + 1], (HD, HD))
            write_new(j, hh, s_old * gamma_ref[head0 + hh] + k_col * v_ref[j:j + 1, _head(hh)])
            opart_ref[j:j + 1, _head(hh)] = jnp.sum(q_col * s_old, axis=0, keepdims=True)


def _prompt_kernel(x_ref, win_ref, wout_ref, rope_ref, qdec_ref, kdec_ref, mask_ref,
                   ws_ref, gb_ref, lng_ref, lnb_ref, fg_ref, gamma_ref, qkv_ref, s0_ref, *refs,
                   layer, final_norm, aliased):
    refs = refs[N_SHARED * aliased:]
    y_ref, sall_ref, opart_ref, snew_ref = refs[:4]
    xb, qdb, kdt, vb, gr, ug, vnb, om, wtril, sb, kvs, scb = refs[4:]
    s_ref = sall_ref if aliased else sall_ref.at[layer]
    lng, lnb = (r[layer:layer + 1, :] for r in (lng_ref, lnb_ref))

    t = pl.program_id(1)
    n_chunks = x_ref.shape[0] // CHUNK

    @pl.when(t == 0)
    def _start_of_row():
        sall_ref[...] = jnp.zeros(sall_ref.shape, F32)
        row = lax.broadcasted_iota(jnp.int32, (CHUNK, CHUNK), 0)
        col = lax.broadcasted_iota(jnp.int32, (CHUNK, CHUNK), 1)
        for h in range(HEADS):
            wtril[h] = jnp.where(row >= col, ws_ref[h], 0.0).astype(BF16)

    def proj(seg, pair=None):
        lo, width = (seg * WIDTH, WIDTH) if pair is None else (seg * WIDTH + pair * 2 * HD, 2 * HD)
        return _dot(xb[...], _unpack(win_ref[:, lo:lo + width]))

    pairs = range(HEADS // 2)
    pair_cols = lambda pair: slice(pair * 2 * HD, (pair + 1) * 2 * HD)

    xb[...] = _rms_scale(x_ref[...]).astype(BF16)

    head_blocks = WIDTH // opart_ref.shape[1]
    head0 = lax.rem(pl.program_id(0) * pl.num_programs(1) + t, head_blocks) * (HEADS // head_blocks)
    if aliased:
        def write_new(j, hh, s_new):
            snew_ref[j, hh] = s_new
    else:
        def write_new(j, hh, s_new):
            for l in range(snew_ref.shape[0]):
                snew_ref[l, j, hh] = s_new if l == layer else jnp.zeros_like(s_new)
    _sample_state_step(*(qkv_ref.at[i] for i in range(3)), gamma_ref, s0_ref, write_new, opart_ref, head0)

    cos, sin = rope_ref[0], rope_ref[1]
    for pair in pairs:
        acc = proj(SEG_Q, pair)
        for i in range(2):
            h = 2 * pair + i
            r = _rope(acc[:, _head(i)], cos, sin)
            for c in range(n_chunks):
                rows = _chunk_rows(c)
                qdb[rows, _head(h)] = (r[rows] * qdec_ref[:, _head(h)]).astype(BF16)

    for pair in pairs:
        acc = proj(SEG_K, pair)
        for i in range(2):
            h = 2 * pair + i
            r = _rope(acc[:, _head(i)], cos, sin)
            for c in range(n_chunks):
                kd = r[_chunk_rows(c)] * kdec_ref[:, _head(h)]
                kdt[c, h] = kd.T.astype(BF16)

    for pair in pairs:
        vb[:, pair_cols(pair)] = proj(SEG_V, pair).astype(BF16)

    for c in range(n_chunks):
        rows = _chunk_rows(c)
        for h in range(HEADS):
            hs = _head(h)
            scb[c, h] = (_dot(qdb[rows, hs], kdt[c, h]) * mask_ref[h]).astype(BF16)
            kvs[c, h] = _dot(kdt[c, h], vb[rows, hs])

    for pair in pairs:
        gr[:, pair_cols(pair)] = _silu(proj(SEG_GR, pair))
    vnb[...] = _layernorm(proj(SEG_VG), lng, lnb).astype(BF16)

    for h in range(HEADS):
        s = s_ref[h]
        for c in range(n_chunks):
            sb[c, h] = s.astype(BF16)
            s = s * CHUNK_DECAY[h] + kvs[c, h]
        s_ref[h] = s

    for c in range(n_chunks):
        rows = _chunk_rows(c)
        for h in range(HEADS):
            hs = _head(h)
            lhs = jnp.concatenate([scb[c, h], qdb[rows, hs]], axis=1)
            rhs = jnp.concatenate([vb[rows, hs], sb[c, h]], axis=0)
            om[rows, hs] = (_rms_scale(_dot(lhs, rhs)) * gr[rows, hs]).astype(BF16)

    for pair in pairs:
        ug[:, pair_cols(pair)] = proj(SEG_U, pair) * _silu(proj(SEG_GG, pair))

    for c in range(n_chunks):
        rows = _chunk_rows(c)
        for h in range(HEADS):
            hs = _head(h)
            s = _dot(wtril[h], vnb[rows, hs]) + gb_ref[:, hs]
            om[rows, WIDTH + h * HD:WIDTH + (h + 1) * HD] = (ug[rows, hs] * s).astype(BF16)

    half = x_ref.shape[0] // 2
    for rows in (slice(0, half), slice(half, 2 * half)):
        y = x_ref[rows, :] + _dot(om[rows, :], _unpack(wout_ref[...]))
        if final_norm:
            y = _rms_scale(y) * fg_ref[...]
        y_ref[rows, :] = y


N_PLAIN_OPERANDS = 15
N_SHARED = 2
OUT_PROMPT_STATES, OUT_SAMPLE_STATES = 1, 3


def _prompt_layer(x, win, wout, rope, qdec, kdec, mask, ws, gb, lng, lnb, fg, qkv, states, shared, *,
                  layer, chunks_per_step, final_norm):
    batch, seq, _ = x.shape
    depth, n_rows = states.shape[:2]
    aliased = shared is not None
    tok = chunks_per_step * CHUNK
    steps_per_row = seq // tok
    heads = n_rows * HEADS // (batch * steps_per_row * SUBLANES)
    head_blocks = HEADS // heads
    assert heads * head_blocks == HEADS and batch * steps_per_row == (n_rows // SUBLANES) * head_blocks
    tile = lambda b, t: ((b * steps_per_row + t) // head_blocks, (b * steps_per_row + t) % head_blocks)

    row_spec = pl.BlockSpec((None, tok, D_MODEL), lambda b, t: (b, t, 0))
    pos_spec = pl.BlockSpec((2, tok, HD), lambda b, t: (0, t, 0))
    tile_spec = pl.BlockSpec((SUBLANES, heads * HD), tile)
    qkv_spec = pl.BlockSpec((3, SUBLANES, heads * HD), lambda b, t: (0, *tile(b, t)))
    state_spec = pl.BlockSpec((None, SUBLANES, heads, HD, HD), lambda b, t: (layer, *tile(b, t), 0, 0))
    all_layers_spec = pl.BlockSpec((depth, SUBLANES, heads, HD, HD), lambda b, t: (0, *tile(b, t), 0, 0))
    if aliased:
        pstate_spec = pl.BlockSpec((None, None, HEADS, HD, HD), lambda b, t: (layer, b, 0, 0, 0))
    else:
        pstate_spec = pl.BlockSpec((depth, None, HEADS, HD, HD), lambda b, t: (0, b, 0, 0, 0))
    gamma = jnp.asarray(np.broadcast_to(np.asarray(GAMMA)[:, None, None], (HEADS, 1, HD)), F32)
    bf16_rows = lambda width: pltpu.VMEM((tok, width), BF16)
    per_chunk_head = lambda dtype: pltpu.VMEM((chunks_per_step, HEADS, CHUNK, CHUNK), dtype)

    return pl.pallas_call(
        functools.partial(_prompt_kernel, layer=layer, final_norm=final_norm, aliased=aliased),
        grid=(batch, steps_per_row),
        in_specs=[
            row_spec,
            _resident_spec((D_MODEL // 2, N_SEG * WIDTH)),
            _resident_spec((MIX // 2, D_MODEL)),
            pos_spec,
            _const_spec((CHUNK, WIDTH)), _const_spec((CHUNK, WIDTH)),
            _const_spec((HEADS, CHUNK, CHUNK)),
            _layer_spec((HEADS, CHUNK, CHUNK), layer),
            _const_spec((CHUNK, WIDTH)),
            _const_spec((depth, WIDTH)), _const_spec((depth, WIDTH)),
            _const_spec((1, D_MODEL)),
            _const_spec((HEADS, 1, HD)),
            qkv_spec,
            state_spec,
            *[pl.BlockSpec(memory_space=pl.ANY)] * (N_SHARED * aliased),
        ],
        out_specs=[row_spec, pstate_spec, tile_spec, state_spec if aliased else all_layers_spec],
        out_shape=[jax.ShapeDtypeStruct(x.shape, F32),
                   jax.ShapeDtypeStruct((depth, batch, HEADS, HD, HD), F32),
                   jax.ShapeDtypeStruct((n_rows, WIDTH), F32),
                   jax.ShapeDtypeStruct(states.shape, F32)],
        input_output_aliases=({N_PLAIN_OPERANDS: OUT_PROMPT_STATES, N_PLAIN_OPERANDS + 1: OUT_SAMPLE_STATES}
                              if aliased else {}),
        scratch_shapes=[
            bf16_rows(D_MODEL),
            bf16_rows(WIDTH),
            per_chunk_head(BF16),
            bf16_rows(WIDTH),
            pltpu.VMEM((tok, WIDTH), F32),
            pltpu.VMEM((tok, WIDTH), F32),
            bf16_rows(WIDTH),
            bf16_rows(MIX),
            pltpu.VMEM((HEADS, CHUNK, CHUNK), BF16),
            per_chunk_head(BF16),
            per_chunk_head(F32),
            per_chunk_head(BF16),
        ],
        compiler_params=pltpu.CompilerParams(
            dimension_semantics=("arbitrary", "arbitrary"),
            vmem_limit_bytes=VMEM_LIMIT_BYTES,
        ),
        name="prompt_layer",
    )(x, win, wout, rope, qdec, kdec, mask, ws, gb, lng, lnb, fg, gamma, qkv, states,
      *(shared if aliased else ()))


N_WOUT_BLOCKS = 8
N_START_STEPS = max(N_SEG, N_WOUT_BLOCKS)
N_WIN_BUFFERS = 3


def _finish_layer(h, scv_ref, opart_ref, gr_ref, m_ref, wout_ref):
    parts = []
    for hd in range(HEADS):
        hs = _head(hd)
        o = scv_ref[:, hs] + GAMMA[hd] * opart_ref[:, hs]
        parts.append((_rms_scale(o) * gr_ref[:, hs]).astype(BF16))
    om = jnp.concatenate(parts + [m_ref[...]], axis=1)
    return h + _dot(om, _unpack(wout_ref[...]))


def _sample_start_kernel(*refs, post, layer):
    h_ref, refs = refs[0], refs[1:]
    if post:
        post_refs, vn_prev_ref, refs = refs[:5], refs[5], refs[6:]
    ng_ref, win_ref, wout_ref, cos_ref, sin_ref, ws_ref, gmb_ref, lng_ref, lnb_ref = refs[:9]
    refs = refs[9:]
    if post:
        hout_ref, refs = refs[0], refs[1:]
    qkv_ref, scv_ref, gr_ref, m_ref, vn_ref, gb_ref, winb_ref, woutb_ref, xb, u_keep, vn_keep, win_buf, win_sem = refs
    q_ref, k_ref, v_ref = (qkv_ref.at[i] for i in range(3))
    j = pl.program_id(0)

    def win_copy(segment, slot):
        cols = pl.ds(pl.multiple_of(segment * WIDTH, WIDTH), WIDTH)
        return pltpu.make_async_copy(win_ref.at[layer, :, cols], win_buf.at[slot], win_sem.at[slot])

    @pl.when(j == 0)
    def _prefill():
        for s in range(min(N_WIN_BUFFERS, N_SEG)):
            win_copy(s, s).start()

    @pl.when(j == 0)
    def _tokens():
        h = h_ref[...]
        if post:
            h = _finish_layer(h, *post_refs)
            hout_ref[...] = h
        xb[...] = _rms_scale(h).astype(BF16)

    @pl.when(j < N_SEG)
    def _segment():
        gain = ng_ref[layer:layer + 1, :]
        tiles = [gain[:, i * LANES:(i + 1) * LANES] for i in range(D_MODEL // LANES)]
        pad = jnp.zeros((LANES - len(tiles), LANES), F32)
        gain_cols = jnp.concatenate(tiles + [pad], axis=0).T
        slot = lax.rem(j, N_WIN_BUFFERS)
        win_copy(j, slot).wait()
        for i in range(D_MODEL // LANES):
            rows = slice(i * LANES, (i + 1) * LANES)
            packed_rows = slice(i * LANES // 2, (i + 1) * LANES // 2)
            winb_ref[packed_rows, :] = _pack((win_buf[slot, rows, :] * gain_cols[:, i:i + 1]).astype(BF16))

        @pl.when(j + N_WIN_BUFFERS < N_SEG)
        def _refill():
            win_copy(j + N_WIN_BUFFERS, slot).start()

        acc = _dot(xb[...], _unpack(winb_ref[...]))

        @pl.when(j == SEG_Q)
        def _():
            for hd in range(HEADS):
                q_ref[:, _head(hd)] = _rope(acc[:, _head(hd)], cos_ref[...], sin_ref[...])

        @pl.when(j == SEG_K)
        def _():
            for hd in range(HEADS):
                hs = _head(hd)
                k = _rope(acc[:, hs], cos_ref[...], sin_ref[...]) * K_SCALE
                k_ref[:, hs] = k
                qk = jnp.sum(q_ref[:, hs] * k, axis=-1, keepdims=True)
                scv_ref[:, hs] = jnp.broadcast_to(qk, k.shape)

        @pl.when(j == SEG_V)
        def _():
            v_ref[...] = acc
            scv_ref[...] = scv_ref[...] * acc

        @pl.when(j == SEG_GR)
        def _():
            gr_ref[...] = _silu(acc)

        @pl.when(j == SEG_U)
        def _():
            u_keep[...] = acc

        @pl.when(j == SEG_VG)
        def _():
            vn = _layernorm(acc, lng_ref[layer:layer + 1, :], lnb_ref[layer:layer + 1, :])
            vn_keep[...] = vn
            if post:
                vn_ref[0:layer] = vn_prev_ref[...]
            vn_ref[layer] = jnp.swapaxes(jnp.stack([vn[:, _head(hd)] for hd in range(HEADS)], axis=0), 0, 1)

        @pl.when(j == SEG_GG)
        def _():
            for hd in range(HEADS):
                hs = _head(hd)
                s = vn_keep[:, hs] * ws_ref[hd, 0:1, 0:1] + gmb_ref[layer, hd:hd + 1, 0:1]
                m_ref[:, hs] = (u_keep[:, hs] * _silu(acc[:, hs]) * s).astype(BF16)

    woutb_ref[...] = _pack(wout_ref[...].astype(BF16))

    @pl.when(j == pl.num_programs(0) - 1)
    def _bias_table():
        b = gmb_ref[layer]
        cols = jnp.concatenate([b, jnp.zeros((LANES - HEADS, CHUNK), F32)], axis=0).T
        for hd in range(HEADS):
            gb_ref[:, _head(hd)] = jnp.broadcast_to(cols[:, hd:hd + 1], (CHUNK, HD))


def _sample_final_kernel(h_ref, scv_ref, opart_ref, gr_ref, m_ref, wout_ref, fg_ref, y_ref):
    y_ref[...] = _rms_scale(_finish_layer(h_ref[...], scv_ref, opart_ref, gr_ref, m_ref, wout_ref)) * fg_ref[...]


def _token_spec(n_rows):
    return pl.BlockSpec((n_rows, None, D_MODEL), lambda *_: (0, 0, 0))


def _sample_start(h, post_args, ng, w_in, w_out, rope, ws, gmb, lng, lnb, *, layer):
    n_rows, depth = h.shape[0], w_in.shape[0]
    post = post_args is not None
    assert post == (layer > 0)
    vn_stack = (layer + 1, n_rows, HEADS, HD)
    full = _const_spec((n_rows, WIDTH))
    vecs = _const_spec((depth, WIDTH))
    seg = lambda j: jnp.minimum(j, N_SEG - 1)
    out_block = lambda j: jnp.minimum(j, N_WOUT_BLOCKS - 1)
    f32_rows = jax.ShapeDtypeStruct((n_rows, WIDTH), F32)
    args = [h] + (list(post_args) if post else []) + [ng, w_in, w_out, *rope, ws, gmb, lng, lnb]
    in_specs = [_token_spec(n_rows) if h.ndim == 3 else full]
    if post:
        in_specs += [full, full, full, full, _resident_spec((MIX // 2, D_MODEL)), _const_spec((layer, *vn_stack[1:]))]
    in_specs += [vecs,
                 pl.BlockSpec(memory_space=pl.ANY),
                 pl.BlockSpec((None, MIX // N_WOUT_BLOCKS, D_MODEL), lambda j: (layer, out_block(j), 0)),
                 _const_spec((1, HD)), _const_spec((1, HD)),
                 _layer_spec((HEADS, CHUNK, CHUNK), layer), _const_spec((depth, HEADS, CHUNK)), vecs, vecs]
    out_shape = (([f32_rows] if post else []) + [jax.ShapeDtypeStruct((3, n_rows, WIDTH), F32)] + [f32_rows] * 2
                 + [jax.ShapeDtypeStruct((n_rows, WIDTH), BF16), jax.ShapeDtypeStruct(vn_stack, F32),
                    jax.ShapeDtypeStruct((CHUNK, WIDTH), F32),
                    jax.ShapeDtypeStruct((D_MODEL // 2, N_SEG * WIDTH), jnp.uint32),
                    jax.ShapeDtypeStruct((MIX // 2, D_MODEL), jnp.uint32)])
    out_specs = [full] * post + [_const_spec((3, n_rows, WIDTH))] + [full] * 3 + [_const_spec(vn_stack), _const_spec((CHUNK, WIDTH)),
                                                 pl.BlockSpec((D_MODEL // 2, WIDTH), lambda j: (0, seg(j))),
                                                 pl.BlockSpec((MIX // N_WOUT_BLOCKS // 2, D_MODEL), lambda j: (out_block(j), 0))]
    return pl.pallas_call(
        functools.partial(_sample_start_kernel, post=post, layer=layer),
        grid=(N_START_STEPS,),
        in_specs=in_specs,
        out_specs=out_specs,
        out_shape=out_shape,
        scratch_shapes=[pltpu.VMEM((n_rows, D_MODEL), BF16),
                        pltpu.VMEM((n_rows, WIDTH), F32),
                        pltpu.VMEM((n_rows, WIDTH), F32),
                        pltpu.VMEM((N_WIN_BUFFERS, D_MODEL, WIDTH), F32),
                        pltpu.SemaphoreType.DMA((N_WIN_BUFFERS,))],
        compiler_params=pltpu.CompilerParams(
            dimension_semantics=("arbitrary",),
            vmem_limit_bytes=VMEM_LIMIT_BYTES,
        ),
        name="sample_start",
    )(*args)


def _sample_final(h, post_args, fg):
    n_rows = h.shape[0]
    full = _const_spec((n_rows, WIDTH))
    return pl.pallas_call(
        _sample_final_kernel,
        grid=(1,),
        in_specs=[full, full, full, full, full, _const_spec((MIX // 2, D_MODEL)), _const_spec((1, D_MODEL))],
        out_specs=_token_spec(n_rows),
        out_shape=jax.ShapeDtypeStruct((n_rows, 1, D_MODEL), F32),
        compiler_params=pltpu.CompilerParams(
            dimension_semantics=("arbitrary",),
            vmem_limit_bytes=VMEM_LIMIT_BYTES,
        ),
        name="sample_final",
    )(h, *post_args, fg)


def _rope_tables(pos):
    inv = ROPE_BASE ** (-np.arange(0, HD, 2, dtype=np.float64) / HD)
    ang = np.asarray(pos, np.float64)[:, None] * inv[None, :]
    c, s = np.cos(ang), np.sin(ang)
    return (jnp.asarray(np.concatenate([c, c], axis=-1), F32),
            jnp.asarray(np.concatenate([-s, s], axis=-1), F32))


def _retention_tables():
    lg = np.log(np.asarray(GAMMA, np.float64))
    idx = np.arange(CHUNK, dtype=np.float64)
    causal = idx[:, None] >= idx[None, :]
    mask = np.where(causal[None], np.exp(-lg * CHUNK)[:, None, None], 0.0)
    q_dec = np.exp(lg[None, :] * (idx[:, None] + 1.0))
    k_dec = np.exp(lg[None, :] * (CHUNK - 1.0 - idx[:, None])) * K_SCALE
    per_lane = lambda a: np.repeat(a, HD, axis=1)
    return jnp.asarray(mask, F32), jnp.asarray(per_lane(q_dec), F32), jnp.asarray(per_lane(k_dec), F32)


def kernel(x_prompt, x_sample, state_ret, norm_g, w_in, w_out, gm_ws, gm_b, gm_ln_g, gm_ln_b, final_g):
    depth = w_in.shape[0]
    batch, seq, _ = x_prompt.shape
    n_rows = x_sample.shape[0]
    assert x_sample.shape[1] == 1

    rope_p = jnp.stack(_rope_tables(np.arange(seq)))
    rope_s = _rope_tables(PAST_LEN + np.arange(1))
    mask, q_dec, k_dec = _retention_tables()
    fg = final_g.reshape(1, D_MODEL)
    start = functools.partial(_sample_start, ng=norm_g, w_in=w_in, w_out=w_out, rope=rope_s, ws=gm_ws, gmb=gm_b,
                              lng=gm_ln_g, lnb=gm_ln_b)

    h_p, h_s, shared = x_prompt, x_sample, None
    qkv, scv, gr, m, vn, gb, win_b, wout_b = start(h_s, None, layer=0)
    for l in range(depth):
        h_p, states_p, opart, states_s = _prompt_layer(
            h_p, win_b, wout_b, rope_p, q_dec, k_dec, mask, gm_ws, gb, gm_ln_g, gm_ln_b, fg,
            qkv, state_ret, shared, layer=l, chunks_per_step=CHUNKS_PER_STEP, final_norm=l == depth - 1)
        shared = (states_p, states_s)
        post_args = (scv, opart, gr, m, wout_b)
        if l == depth - 1:
            y_s = _sample_final(h_s, post_args, fg)
        else:
            h_s, qkv, scv, gr, m, vn, gb, win_b, wout_b = start(h_s, (*post_args, vn), layer=l + 1)

    return h_p, y_s, states_p, states_s, vn.reshape(depth, n_rows, 1, HEADS, HD)
```

```python
import functools

import jax
import jax.numpy as jnp
import numpy as np
from jax import lax
from jax.experimental import pallas as pl
from jax.experimental.pallas import tpu as pltpu

F32 = jnp.float32
BF16 = jnp.bfloat16

D_MODEL = 1024
HEADS = 8
HD = 128
LANES, SUBLANES = 128, 8
WIDTH = HEADS * HD
N_SEG = 7
MIX = 2 * WIDTH
CHUNK = 128
PAST_LEN = 16384
ROPE_BASE = 10000.0
EPS = 1e-6
SEG_Q, SEG_K, SEG_V, SEG_GR, SEG_U, SEG_VG, SEG_GG = range(N_SEG)

GAMMA = tuple(1.0 - 2.0 ** (-5.0 - h) for h in range(HEADS))
CHUNK_DECAY = tuple(g ** CHUNK for g in GAMMA)
K_SCALE = HD ** -0.5

CHUNKS_PER_STEP = 4
V7X_VMEM_BYTES = 64 * 1024 * 1024
VMEM_LIMIT_BYTES = V7X_VMEM_BYTES - 2 * 1024 * 1024


def _silu(x):
    return x * (1.0 / (1.0 + jnp.exp(-x)))


def _rms_scale(x):
    return x * lax.rsqrt(jnp.mean(x * x, axis=-1, keepdims=True) + EPS)


def _layernorm(x, g, b):
    mu = jnp.mean(x, axis=-1, keepdims=True)
    xc = x - mu
    var = jnp.mean(xc * xc, axis=-1, keepdims=True)
    return xc * lax.rsqrt(var + EPS) * g + b


def _rope(x, cos, sin):
    return x * cos + pltpu.roll(x, HD // 2, 1) * sin


def _head(h):
    return slice(h * HD, (h + 1) * HD)


def _dot(a, b):
    return jnp.dot(a, b, preferred_element_type=F32)


def _pack(w):
    return pltpu.bitcast(w, jnp.uint32)


def _unpack(words):
    return pltpu.bitcast(words, BF16)


def _const_spec(shape):
    zeros = (0,) * len(shape)
    return pl.BlockSpec(shape, lambda *_: zeros)


def _layer_spec(shape, layer, **kwargs):
    index = (layer,) + (0,) * len(shape)
    return pl.BlockSpec((None,) + shape, lambda *_: index, **kwargs)


def _resident_spec(shape):
    zeros = (0,) * len(shape)
    return pl.BlockSpec(shape, lambda *_: zeros, pipeline_mode=pl.Buffered(1))


def _chunk_rows(c):
    return slice(c * CHUNK, (c + 1) * CHUNK)


def _sample_state_step(q_ref, k_ref, v_ref, gamma_ref, s0_ref, write_new, opart_ref, head0):
    heads = opart_ref.shape[1] // HD
    tiles = [r[:, _head(hh)] for r in (k_ref, q_ref) for hh in range(heads)]
    pad = jnp.zeros((LANES - len(tiles) * SUBLANES, HD), F32)
    cols = jnp.concatenate(tiles + [pad], axis=0).T
    for hh in range(heads):
        for j in range(SUBLANES):
            s_old = s0_ref[j, hh]
            k_lane = hh * SUBLANES + j
            q_lane = heads * SUBLANES + k_lane
            k_col = jnp.broadcast_to(cols[:, k_lane:k_lane + 1], (HD, HD))
            q_col = jnp.broadcast_to(cols[:, q_lane:q_lane + 1], (HD, HD))
            write_new(j, hh, s_old * gamma_ref[head0 + hh] + k_col * v_ref[j:j + 1, _head(hh)])
            opart_ref[j:j + 1, _head(hh)] = jnp.sum(q_col * s_old, axis=0, keepdims=True)


def _prompt_kernel(x_ref, win_ref, wout_ref, rope_ref, qdec_ref, kdec_ref, mask_ref,
                   ws_ref, gb_ref, lng_ref, lnb_ref, fg_ref, gamma_ref, qkv_ref, s0_ref, *refs,
                   layer, final_norm, aliased):
    refs = refs[N_SHARED * aliased:]
    y_ref, sall_ref, opart_ref, snew_ref = refs[:4]
    xb, qdb, kdt, vb, gr, ug, vnb, om, wtril, sb, kvs, scb = refs[4:]
    s_ref = sall_ref if aliased else sall_ref.at[layer]
    lng, lnb = (r[layer:layer + 1, :] for r in (lng_ref, lnb_ref))

    t = pl.program_id(1)
    n_chunks = x_ref.shape[0] // CHUNK

    @pl.when(t == 0)
    def _start_of_row():
        sall_ref[...] = jnp.zeros(sall_ref.shape, F32)
        row = lax.broadcasted_iota(jnp.int32, (CHUNK, CHUNK), 0)
        col = lax.broadcasted_iota(jnp.int32, (CHUNK, CHUNK), 1)
        for h in range(HEADS):
            wtril[h] = jnp.where(row >= col, ws_ref[h], 0.0).astype(BF16)

    def proj(seg, pair=None):
        lo, width = (seg * WIDTH, WIDTH) if pair is None else (seg * WIDTH + pair * 2 * HD, 2 * HD)
        return _dot(xb[...], _unpack(win_ref[:, lo:lo + width]))

    pairs = range(HEADS // 2)
    pair_cols = lambda pair: slice(pair * 2 * HD, (pair + 1) * 2 * HD)

    xb[...] = _rms_scale(x_ref[...]).astype(BF16)

    head_blocks = WIDTH // opart_ref.shape[1]
    head0 = lax.rem(pl.program_id(0) * pl.num_programs(1) + t, head_blocks) * (HEADS // head_blocks)
    if aliased:
        def write_new(j, hh, s_new):
            snew_ref[j, hh] = s_new
    else:
        def write_new(j, hh, s_new):
            for l in range(snew_ref.shape[0]):
                snew_ref[l, j, hh] = s_new if l == layer else jnp.zeros_like(s_new)
    _sample_state_step(*(qkv_ref.at[i] for i in range(3)), gamma_ref, s0_ref, write_new, opart_ref, head0)

    cos, sin = rope_ref[0], rope_ref[1]
    for pair in pairs:
        acc = proj(SEG_Q, pair)
        for i in range(2):
            h = 2 * pair + i
            r = _rope(acc[:, _head(i)], cos, sin)
            for c in range(n_chunks):
                rows = _chunk_rows(c)
                qdb[rows, _head(h)] = (r[rows] * qdec_ref[:, _head(h)]).astype(BF16)

    for pair in pairs:
        acc = proj(SEG_K, pair)
        for i in range(2):
            h = 2 * pair + i
            r = _rope(acc[:, _head(i)], cos, sin)
            for c in range(n_chunks):
                kd = r[_chunk_rows(c)] * kdec_ref[:, _head(h)]
                kdt[c, h] = kd.T.astype(BF16)

    for pair in pairs:
        vb[:, pair_cols(pair)] = proj(SEG_V, pair).astype(BF16)

    for c in range(n_chunks):
        rows = _chunk_rows(c)
        for h in range(HEADS):
            hs = _head(h)
            scb[c, h] = (_dot(qdb[rows, hs], kdt[c, h]) * mask_ref[h]).astype(BF16)
            kvs[c, h] = _dot(kdt[c, h], vb[rows, hs])

    for pair in pairs:
        gr[:, pair_cols(pair)] = _silu(proj(SEG_GR, pair))
    vnb[...] = _layernorm(proj(SEG_VG), lng, lnb).astype(BF16)

    for h in range(HEADS):
        s = s_ref[h]
        for c in range(n_chunks):
            sb[c, h] = s.astype(BF16)
            s = s * CHUNK_DECAY[h] + kvs[c, h]
        s_ref[h] = s

    for c in range(n_chunks):
        rows = _chunk_rows(c)
        for h in range(HEADS):
            hs = _head(h)
            lhs = jnp.concatenate([scb[c, h], qdb[rows, hs]], axis=1)
            rhs = jnp.concatenate([vb[rows, hs], sb[c, h]], axis=0)
            om[rows, hs] = (_rms_scale(_dot(lhs, rhs)) * gr[rows, hs]).astype(BF16)

    for pair in pairs:
        ug[:, pair_cols(pair)] = proj(SEG_U, pair) * _silu(proj(SEG_GG, pair))

    for c in range(n_chunks):
        rows = _chunk_rows(c)
        for h in range(HEADS):
            hs = _head(h)
            s = _dot(wtril[h], vnb[rows, hs]) + gb_ref[:, hs]
            om[rows, WIDTH + h * HD:WIDTH + (h + 1) * HD] = (ug[rows, hs] * s).astype(BF16)

    half = x_ref.shape[0] // 2
    for rows in (slice(0, half), slice(half, 2 * half)):
        y = x_ref[rows, :] + _dot(om[rows, :], _unpack(wout_ref[...]))
        if final_norm:
            y = _rms_scale(y) * fg_ref[...]
        y_ref[rows, :] = y


N_PLAIN_OPERANDS = 15
N_SHARED = 2
OUT_PROMPT_STATES, OUT_SAMPLE_STATES = 1, 3


def _prompt_layer(x, win, wout, rope, qdec, kdec, mask, ws, gb, lng, lnb, fg, qkv, states, shared, *,
                  layer, chunks_per_step, final_norm):
    batch, seq, _ = x.shape
    depth, n_rows = states.shape[:2]
    aliased = shared is not None
    tok = chunks_per_step * CHUNK
    steps_per_row = seq // tok
    heads = n_rows * HEADS // (batch * steps_per_row * SUBLANES)
    head_blocks = HEADS // heads
    assert heads * head_blocks == HEADS and batch * steps_per_row == (n_rows // SUBLANES) * head_blocks
    tile = lambda b, t: ((b * steps_per_row + t) // head_blocks, (b * steps_per_row + t) % head_blocks)

    row_spec = pl.BlockSpec((None, tok, D_MODEL), lambda b, t: (b, t, 0))
    pos_spec = pl.BlockSpec((2, tok, HD), lambda b, t: (0, t, 0))
    tile_spec = pl.BlockSpec((SUBLANES, heads * HD), tile)
    qkv_spec = pl.BlockSpec((3, SUBLANES, heads * HD), lambda b, t: (0, *tile(b, t)))
    state_spec = pl.BlockSpec((None, SUBLANES, heads, HD, HD), lambda b, t: (layer, *tile(b, t), 0, 0))
    all_layers_spec = pl.BlockSpec((depth, SUBLANES, heads, HD, HD), lambda b, t: (0, *tile(b, t), 0, 0))
    if aliased:
        pstate_spec = pl.BlockSpec((None, None, HEADS, HD, HD), lambda b, t: (layer, b, 0, 0, 0))
    else:
        pstate_spec = pl.BlockSpec((depth, None, HEADS, HD, HD), lambda b, t: (0, b, 0, 0, 0))
    gamma = jnp.asarray(np.broadcast_to(np.asarray(GAMMA)[:, None, None], (HEADS, 1, HD)), F32)
    bf16_rows = lambda width: pltpu.VMEM((tok, width), BF16)
    per_chunk_head = lambda dtype: pltpu.VMEM((chunks_per_step, HEADS, CHUNK, CHUNK), dtype)

    return pl.pallas_call(
        functools.partial(_prompt_kernel, layer=layer, final_norm=final_norm, aliased=aliased),
        grid=(batch, steps_per_row),
        in_specs=[
            row_spec,
            _resident_spec((D_MODEL // 2, N_SEG * WIDTH)),
            _resident_spec((MIX // 2, D_MODEL)),
            pos_spec,
            _const_spec((CHUNK, WIDTH)), _const_spec((CHUNK, WIDTH)),
            _const_spec((HEADS, CHUNK, CHUNK)),
            _layer_spec((HEADS, CHUNK, CHUNK), layer),
            _const_spec((CHUNK, WIDTH)),
            _const_spec((depth, WIDTH)), _const_spec((depth, WIDTH)),
            _const_spec((1, D_MODEL)),
            _const_spec((HEADS, 1, HD)),
            qkv_spec,
            state_spec,
            *[pl.BlockSpec(memory_space=pl.ANY)] * (N_SHARED * aliased),
        ],
        out_specs=[row_spec, pstate_spec, tile_spec, state_spec if aliased else all_layers_spec],
        out_shape=[jax.ShapeDtypeStruct(x.shape, F32),
                   jax.ShapeDtypeStruct((depth, batch, HEADS, HD, HD), F32),
                   jax.ShapeDtypeStruct((n_rows, WIDTH), F32),
                   jax.ShapeDtypeStruct(states.shape, F32)],
        input_output_aliases=({N_PLAIN_OPERANDS: OUT_PROMPT_STATES, N_PLAIN_OPERANDS + 1: OUT_SAMPLE_STATES}
                              if aliased else {}),
        scratch_shapes=[
            bf16_rows(D_MODEL),
            bf16_rows(WIDTH),
            per_chunk_head(BF16),
            bf16_rows(WIDTH),
            pltpu.VMEM((tok, WIDTH), F32),
            pltpu.VMEM((tok, WIDTH), F32),
            bf16_rows(WIDTH),
            bf16_rows(MIX),
            pltpu.VMEM((HEADS, CHUNK, CHUNK), BF16),
            per_chunk_head(BF16),
            per_chunk_head(F32),
            per_chunk_head(BF16),
        ],
        compiler_params=pltpu.CompilerParams(
            dimension_semantics=("arbitrary", "arbitrary"),
            vmem_limit_bytes=VMEM_LIMIT_BYTES,
        ),
        name="prompt_layer",
    )(x, win, wout, rope, qdec, kdec, mask, ws, gb, lng, lnb, fg, gamma, qkv, states,
      *(shared if aliased else ()))


N_WOUT_BLOCKS = 8
N_START_STEPS = max(N_SEG, N_WOUT_BLOCKS)
N_WIN_BUFFERS = 4


def _finish_layer(h, scv_ref, opart_ref, gr_ref, m_ref, wout_copy, wout_buf):
    parts = []
    for hd in range(HEADS):
        hs = _head(hd)
        o = scv_ref[:, hs] + GAMMA[hd] * opart_ref[:, hs]
        parts.append((_rms_scale(o) * gr_ref[:, hs]).astype(BF16))
    om = jnp.concatenate(parts + [m_ref[...]], axis=1)
    wout_copy.wait()
    return h + _dot(om, _unpack(wout_buf[...]))


def _sample_start_kernel(*refs, post, layer):
    h_ref, refs = refs[0], refs[1:]
    if post:
        post_refs, vn_prev_ref, refs = refs[:5], refs[5], refs[6:]
    ng_ref, win_ref, wout_ref, cos_ref, sin_ref, ws_ref, gmb_ref, lng_ref, lnb_ref = refs[:9]
    refs = refs[9:]
    if post:
        hout_ref, refs = refs[0], refs[1:]
    (qkv_ref, scv_ref, gr_ref, m_ref, vn_ref, gb_ref, winb_ref, woutb_ref,
     xb, u_keep, vn_keep, win_buf, win_sem, *prev_wout) = refs
    if post:
        prev_wout_buf, prev_wout_sem = prev_wout
        prev_wout_copy = pltpu.make_async_copy(post_refs[4], prev_wout_buf, prev_wout_sem.at[0])
    q_ref, k_ref, v_ref = (qkv_ref.at[i] for i in range(3))
    j = pl.program_id(0)

    def win_copy(segment, slot):
        cols = pl.ds(pl.multiple_of(segment * WIDTH, WIDTH), WIDTH)
        return pltpu.make_async_copy(win_ref.at[layer, :, cols], win_buf.at[slot], win_sem.at[slot])

    @pl.when(j == 0)
    def _prefill():
        if post:
            prev_wout_copy.start()
        for s in range(min(N_WIN_BUFFERS, N_SEG)):
            win_copy(s, s).start()

    @pl.when(j == 0)
    def _tokens():
        h = h_ref[...]
        if post:
            h = _finish_layer(h, *post_refs[:4], prev_wout_copy, prev_wout_buf)
            hout_ref[...] = h
        xb[...] = _rms_scale(h).astype(BF16)

    @pl.when(j < N_SEG)
    def _segment():
        gain = ng_ref[layer:layer + 1, :]
        tiles = [gain[:, i * LANES:(i + 1) * LANES] for i in range(D_MODEL // LANES)]
        pad = jnp.zeros((LANES - len(tiles), LANES), F32)
        gain_cols = jnp.concatenate(tiles + [pad], axis=0).T
        slot = lax.rem(j, N_WIN_BUFFERS)
        win_copy(j, slot).wait()
        for i in range(D_MODEL // LANES):
            rows = slice(i * LANES, (i + 1) * LANES)
            packed_rows = slice(i * LANES // 2, (i + 1) * LANES // 2)
            winb_ref[packed_rows, :] = _pack((win_buf[slot, rows, :] * gain_cols[:, i:i + 1]).astype(BF16))

        @pl.when(j + N_WIN_BUFFERS < N_SEG)
        def _refill():
            win_copy(j + N_WIN_BUFFERS, slot).start()

        acc = _dot(xb[...], _unpack(winb_ref[...]))

        @pl.when(j == SEG_Q)
        def _():
            for hd in range(HEADS):
                q_ref[:, _head(hd)] = _rope(acc[:, _head(hd)], cos_ref[...], sin_ref[...])

        @pl.when(j == SEG_K)
        def _():
            for hd in range(HEADS):
                hs = _head(hd)
                k = _rope(acc[:, hs], cos_ref[...], sin_ref[...]) * K_SCALE
                k_ref[:, hs] = k
                qk = jnp.sum(q_ref[:, hs] * k, axis=-1, keepdims=True)
                scv_ref[:, hs] = jnp.broadcast_to(qk, k.shape)

        @pl.when(j == SEG_V)
        def _():
            v_ref[...] = acc
            scv_ref[...] = scv_ref[...] * acc

        @pl.when(j == SEG_GR)
        def _():
            gr_ref[...] = _silu(acc)

        @pl.when(j == SEG_U)
        def _():
            u_keep[...] = acc

        @pl.when(j == SEG_VG)
        def _():
            vn = _layernorm(acc, lng_ref[layer:layer + 1, :], lnb_ref[layer:layer + 1, :])
            vn_keep[...] = vn
            if post:
                vn_ref[0:layer] = vn_prev_ref[...]
            vn_ref[layer] = jnp.swapaxes(jnp.stack([vn[:, _head(hd)] for hd in range(HEADS)], axis=0), 0, 1)

        @pl.when(j == SEG_GG)
        def _():
            for hd in range(HEADS):
                hs = _head(hd)
                s = vn_keep[:, hs] * ws_ref[hd, 0:1, 0:1] + gmb_ref[layer, hd:hd + 1, 0:1]
                m_ref[:, hs] = (u_keep[:, hs] * _silu(acc[:, hs]) * s).astype(BF16)

    woutb_ref[...] = _pack(wout_ref[...].astype(BF16))

    @pl.when(j == pl.num_programs(0) - 1)
    def _bias_table():
        b = gmb_ref[layer]
        cols = jnp.concatenate([b, jnp.zeros((LANES - HEADS, CHUNK), F32)], axis=0).T
        for hd in range(HEADS):
            gb_ref[:, _head(hd)] = jnp.broadcast_to(cols[:, hd:hd + 1], (CHUNK, HD))


def _sample_final_kernel(h_ref, scv_ref, opart_ref, gr_ref, m_ref, wout_ref, fg_ref, y_ref, wout_buf, wout_sem):
    wout_copy = pltpu.make_async_copy(wout_ref, wout_buf, wout_sem.at[0])
    wout_copy.start()
    y = _finish_layer(h_ref[...], scv_ref, opart_ref, gr_ref, m_ref, wout_copy, wout_buf)
    y_ref[...] = _rms_scale(y) * fg_ref[...]


def _token_spec(n_rows):
    return pl.BlockSpec((n_rows, None, D_MODEL), lambda *_: (0, 0, 0))


def _sample_start(h, post_args, ng, w_in, w_out, rope, ws, gmb, lng, lnb, *, layer):
    n_rows, depth = h.shape[0], w_in.shape[0]
    post = post_args is not None
    assert post == (layer > 0)
    vn_stack = (layer + 1, n_rows, HEADS, HD)
    full = _const_spec((n_rows, WIDTH))
    vecs = _const_spec((depth, WIDTH))
    seg = lambda j: jnp.minimum(j, N_SEG - 1)
    out_block = lambda j: jnp.minimum(j, N_WOUT_BLOCKS - 1)
    f32_rows = jax.ShapeDtypeStruct((n_rows, WIDTH), F32)
    args = [h] + (list(post_args) if post else []) + [ng, w_in, w_out, *rope, ws, gmb, lng, lnb]
    in_specs = [_token_spec(n_rows) if h.ndim == 3 else full]
    if post:
        in_specs += [full, full, full, full, pl.BlockSpec(memory_space=pl.ANY), _const_spec((layer, *vn_stack[1:]))]
    in_specs += [vecs,
                 pl.BlockSpec(memory_space=pl.ANY),
                 pl.BlockSpec((None, MIX // N_WOUT_BLOCKS, D_MODEL), lambda j: (layer, out_block(j), 0)),
                 _const_spec((1, HD)), _const_spec((1, HD)),
                 _layer_spec((HEADS, CHUNK, CHUNK), layer), _const_spec((depth, HEADS, CHUNK)), vecs, vecs]
    out_shape = (([f32_rows] if post else []) + [jax.ShapeDtypeStruct((3, n_rows, WIDTH), F32)] + [f32_rows] * 2
                 + [jax.ShapeDtypeStruct((n_rows, WIDTH), BF16), jax.ShapeDtypeStruct(vn_stack, F32),
                    jax.ShapeDtypeStruct((CHUNK, WIDTH), F32),
                    jax.ShapeDtypeStruct((D_MODEL // 2, N_SEG * WIDTH), jnp.uint32),
                    jax.ShapeDtypeStruct((MIX // 2, D_MODEL), jnp.uint32)])
    out_specs = [full] * post + [_const_spec((3, n_rows, WIDTH))] + [full] * 3 + [_const_spec(vn_stack), _const_spec((CHUNK, WIDTH)),
                                                 pl.BlockSpec((D_MODEL // 2, WIDTH), lambda j: (0, seg(j))),
                                                 pl.BlockSpec((MIX // N_WOUT_BLOCKS // 2, D_MODEL), lambda j: (out_block(j), 0))]
    return pl.pallas_call(
        functools.partial(_sample_start_kernel, post=post, layer=layer),
        grid=(N_START_STEPS,),
        in_specs=in_specs,
        out_specs=out_specs,
        out_shape=out_shape,
        scratch_shapes=[pltpu.VMEM((n_rows, D_MODEL), BF16),
                        pltpu.VMEM((n_rows, WIDTH), F32),
                        pltpu.VMEM((n_rows, WIDTH), F32),
                        pltpu.VMEM((N_WIN_BUFFERS, D_MODEL, WIDTH), F32),
                        pltpu.SemaphoreType.DMA((N_WIN_BUFFERS,)),
                        *([pltpu.VMEM((MIX // 2, D_MODEL), jnp.uint32),
                           pltpu.SemaphoreType.DMA((1,))] if post else [])],
        compiler_params=pltpu.CompilerParams(
            dimension_semantics=("arbitrary",),
            vmem_limit_bytes=VMEM_LIMIT_BYTES,
        ),
        name="sample_start",
    )(*args)


def _sample_final(h, post_args, fg):
    n_rows = h.shape[0]
    full = _const_spec((n_rows, WIDTH))
    return pl.pallas_call(
        _sample_final_kernel,
        grid=(1,),
        in_specs=[full, full, full, full, full, pl.BlockSpec(memory_space=pl.ANY), _const_spec((1, D_MODEL))],
        out_specs=_token_spec(n_rows),
        out_shape=jax.ShapeDtypeStruct((n_rows, 1, D_MODEL), F32),
        scratch_shapes=[pltpu.VMEM((MIX // 2, D_MODEL), jnp.uint32), pltpu.SemaphoreType.DMA((1,))],
        compiler_params=pltpu.CompilerParams(
            dimension_semantics=("arbitrary",),
            vmem_limit_bytes=VMEM_LIMIT_BYTES,
        ),
        name="sample_final",
    )(h, *post_args, fg)


def _rope_tables(pos):
    inv = ROPE_BASE ** (-np.arange(0, HD, 2, dtype=np.float64) / HD)
    ang = np.asarray(pos, np.float64)[:, None] * inv[None, :]
    c, s = np.cos(ang), np.sin(ang)
    return (jnp.asarray(np.concatenate([c, c], axis=-1), F32),
            jnp.asarray(np.concatenate([-s, s], axis=-1), F32))


def _retention_tables():
    lg = np.log(np.asarray(GAMMA, np.float64))
    idx = np.arange(CHUNK, dtype=np.float64)
    causal = idx[:, None] >= idx[None, :]
    mask = np.where(causal[None], np.exp(-lg * CHUNK)[:, None, None], 0.0)
    q_dec = np.exp(lg[None, :] * (idx[:, None] + 1.0))
    k_dec = np.exp(lg[None, :] * (CHUNK - 1.0 - idx[:, None])) * K_SCALE
    per_lane = lambda a: np.repeat(a, HD, axis=1)
    return jnp.asarray(mask, F32), jnp.asarray(per_lane(q_dec), F32), jnp.asarray(per_lane(k_dec), F32)


def kernel(x_prompt, x_sample, state_ret, norm_g, w_in, w_out, gm_ws, gm_b, gm_ln_g, gm_ln_b, final_g):
    depth = w_in.shape[0]
    batch, seq, _ = x_prompt.shape
    n_rows = x_sample.shape[0]
    assert x_sample.shape[1] == 1

    rope_p = jnp.stack(_rope_tables(np.arange(seq)))
    rope_s = _rope_tables(PAST_LEN + np.arange(1))
    mask, q_dec, k_dec = _retention_tables()
    fg = final_g.reshape(1, D_MODEL)
    start = functools.partial(_sample_start, ng=norm_g, w_in=w_in, w_out=w_out, rope=rope_s, ws=gm_ws, gmb=gm_b,
                              lng=gm_ln_g, lnb=gm_ln_b)

    h_p, h_s, shared = x_prompt, x_sample, None
    qkv, scv, gr, m, vn, gb, win_b, wout_b = start(h_s, None, layer=0)
    for l in range(depth):
        h_p, states_p, opart, states_s = _prompt_layer(
            h_p, win_b, wout_b, rope_p, q_dec, k_dec, mask, gm_ws, gb, gm_ln_g, gm_ln_b, fg,
            qkv, state_ret, shared, layer=l, chunks_per_step=CHUNKS_PER_STEP, final_norm=l == depth - 1)
        shared = (states_p, states_s)
        post_args = (scv, opart, gr, m, wout_b)
        if l == depth - 1:
            y_s = _sample_final(h_s, post_args, fg)
        else:
            h_s, qkv, scv, gr, m, vn, gb, win_b, wout_b = start(h_s, (*post_args, vn), layer=l + 1)

    return h_p, y_s, states_p, states_s, vn.reshape(depth, n_rows, 1, HEADS, HD)
```

```python
import functools

import jax
import jax.numpy as jnp
import numpy as np
from jax import lax
from jax.experimental import pallas as pl
from jax.experimental.pallas import tpu as pltpu

F32 = jnp.float32
BF16 = jnp.bfloat16

D_MODEL = 1024
HEADS = 8
HD = 128
LANES, SUBLANES = 128, 8
WIDTH = HEADS * HD
N_SEG = 7
MIX = 2 * WIDTH
CHUNK = 128
PAST_LEN = 16384
ROPE_BASE = 10000.0
EPS = 1e-6
SEG_Q, SEG_K, SEG_V, SEG_GR, SEG_U, SEG_VG, SEG_GG = range(N_SEG)

GAMMA = tuple(1.0 - 2.0 ** (-5.0 - h) for h in range(HEADS))
CHUNK_DECAY = tuple(g ** CHUNK for g in GAMMA)
K_SCALE = HD ** -0.5

CHUNKS_PER_STEP = 4
V7X_VMEM_BYTES = 64 * 1024 * 1024
VMEM_LIMIT_BYTES = V7X_VMEM_BYTES - 2 * 1024 * 1024


def _silu(x):
    return x * (1.0 / (1.0 + jnp.exp(-x)))


def _rms_scale(x):
    return x * lax.rsqrt(jnp.mean(x * x, axis=-1, keepdims=True) + EPS)


def _layernorm(x, g, b):
    mu = jnp.mean(x, axis=-1, keepdims=True)
    xc = x - mu
    var = jnp.mean(xc * xc, axis=-1, keepdims=True)
    return xc * lax.rsqrt(var + EPS) * g + b


def _rope(x, cos, sin):
    return x * cos + pltpu.roll(x, HD // 2, 1) * sin


def _head(h):
    return slice(h * HD, (h + 1) * HD)


def _dot(a, b):
    return jnp.dot(a, b, preferred_element_type=F32)


def _pack(w):
    return pltpu.bitcast(w, jnp.uint32)


def _unpack(words):
    return pltpu.bitcast(words, BF16)


def _const_spec(shape):
    zeros = (0,) * len(shape)
    return pl.BlockSpec(shape, lambda *_: zeros)


def _layer_spec(shape, layer, **kwargs):
    index = (layer,) + (0,) * len(shape)
    return pl.BlockSpec((None,) + shape, lambda *_: index, **kwargs)


def _resident_spec(shape):
    zeros = (0,) * len(shape)
    return pl.BlockSpec(shape, lambda *_: zeros, pipeline_mode=pl.Buffered(1))


def _chunk_rows(c):
    return slice(c * CHUNK, (c + 1) * CHUNK)


def _sample_state_step(q_ref, k_ref, v_ref, gamma_ref, s0_ref, write_new, opart_ref, head0):
    heads = opart_ref.shape[1] // HD
    tiles = [r[:, _head(hh)] for r in (k_ref, q_ref) for hh in range(heads)]
    pad = jnp.zeros((LANES - len(tiles) * SUBLANES, HD), F32)
    cols = jnp.concatenate(tiles + [pad], axis=0).T
    for hh in range(heads):
        for j in range(SUBLANES):
            s_old = s0_ref[j, hh]
            k_lane = hh * SUBLANES + j
            q_lane = heads * SUBLANES + k_lane
            k_col = jnp.broadcast_to(cols[:, k_lane:k_lane + 1], (HD, HD))
            q_col = jnp.broadcast_to(cols[:, q_lane:q_lane + 1], (HD, HD))
            write_new(j, hh, s_old * gamma_ref[head0 + hh] + k_col * v_ref[j:j + 1, _head(hh)])
            opart_ref[j:j + 1, _head(hh)] = jnp.sum(q_col * s_old, axis=0, keepdims=True)


def _prompt_kernel(x_ref, win_ref, wout_ref, rope_ref, qdec_ref, kdec_ref, mask_ref,
                   ws_ref, gb_ref, lng_ref, lnb_ref, fg_ref, gamma_ref, qkv_ref, s0_ref, *refs,
                   layer, final_norm, aliased):
    refs = refs[N_SHARED * aliased:]
    y_ref, sall_ref, opart_ref, snew_ref = refs[:4]
    xb, qdb, kdt, vb, gr, ug, vnb, om, wtril, sb, kvs, scb = refs[4:]
    s_ref = sall_ref if aliased else sall_ref.at[layer]
    lng, lnb = (r[layer:layer + 1, :] for r in (lng_ref, lnb_ref))

    t = pl.program_id(1)
    n_chunks = x_ref.shape[0] // CHUNK

    @pl.when(t == 0)
    def _start_of_row():
        sall_ref[...] = jnp.zeros(sall_ref.shape, F32)
        row = lax.broadcasted_iota(jnp.int32, (CHUNK, CHUNK), 0)
        col = lax.broadcasted_iota(jnp.int32, (CHUNK, CHUNK), 1)
        for h in range(HEADS):
            wtril[h] = jnp.where(row >= col, ws_ref[h], 0.0).astype(BF16)

    def proj(seg, pair=None):
        lo, width = (seg * WIDTH, WIDTH) if pair is None else (seg * WIDTH + pair * 2 * HD, 2 * HD)
        return _dot(xb[...], _unpack(win_ref[:, lo:lo + width]))

    pairs = range(HEADS // 2)
    pair_cols = lambda pair: slice(pair * 2 * HD, (pair + 1) * 2 * HD)

    xb[...] = _rms_scale(x_ref[...]).astype(BF16)

    head_blocks = WIDTH // opart_ref.shape[1]
    head0 = lax.rem(pl.program_id(0) * pl.num_programs(1) + t, head_blocks) * (HEADS // head_blocks)
    if aliased:
        def write_new(j, hh, s_new):
            snew_ref[j, hh] = s_new
    else:
        def write_new(j, hh, s_new):
            for l in range(snew_ref.shape[0]):
                snew_ref[l, j, hh] = s_new if l == layer else jnp.zeros_like(s_new)
    _sample_state_step(*(qkv_ref.at[i] for i in range(3)), gamma_ref, s0_ref, write_new, opart_ref, head0)

    cos, sin = rope_ref[0], rope_ref[1]
    for pair in pairs:
        acc = proj(SEG_Q, pair)
        for i in range(2):
            h = 2 * pair + i
            r = _rope(acc[:, _head(i)], cos, sin)
            for c in range(n_chunks):
                rows = _chunk_rows(c)
                qdb[rows, _head(h)] = (r[rows] * qdec_ref[:, _head(h)]).astype(BF16)

    for pair in pairs:
        acc = proj(SEG_K, pair)
        for i in range(2):
            h = 2 * pair + i
            r = _rope(acc[:, _head(i)], cos, sin)
            for c in range(n_chunks):
                kd = r[_chunk_rows(c)] * kdec_ref[:, _head(h)]
                kdt[c, h] = kd.T.astype(BF16)

    for pair in pairs:
        vb[:, pair_cols(pair)] = proj(SEG_V, pair).astype(BF16)

    for c in range(n_chunks):
        rows = _chunk_rows(c)
        for h in range(HEADS):
            hs = _head(h)
            scb[c, h] = (_dot(qdb[rows, hs], kdt[c, h]) * mask_ref[h]).astype(BF16)
            kvs[c, h] = _dot(kdt[c, h], vb[rows, hs])

    for pair in pairs:
        gr[:, pair_cols(pair)] = _silu(proj(SEG_GR, pair))
    vnb[...] = _layernorm(proj(SEG_VG), lng, lnb).astype(BF16)

    for h in range(HEADS):
        s = s_ref[h]
        for c in range(n_chunks):
            sb[c, h] = s.astype(BF16)
            s = s * CHUNK_DECAY[h] + kvs[c, h]
        s_ref[h] = s

    for c in range(n_chunks):
        rows = _chunk_rows(c)
        for h in range(HEADS):
            hs = _head(h)
            lhs = jnp.concatenate([scb[c, h], qdb[rows, hs]], axis=1)
            rhs = jnp.concatenate([vb[rows, hs], sb[c, h]], axis=0)
            om[rows, hs] = (_rms_scale(_dot(lhs, rhs)) * gr[rows, hs]).astype(BF16)

    for pair in pairs:
        ug[:, pair_cols(pair)] = proj(SEG_U, pair) * _silu(proj(SEG_GG, pair))

    for c in range(n_chunks):
        rows = _chunk_rows(c)
        for h in range(HEADS):
            hs = _head(h)
            s = _dot(wtril[h], vnb[rows, hs]) + gb_ref[:, hs]
            om[rows, WIDTH + h * HD:WIDTH + (h + 1) * HD] = (ug[rows, hs] * s).astype(BF16)

    half = x_ref.shape[0] // 2
    for rows in (slice(0, half), slice(half, 2 * half)):
        y = x_ref[rows, :] + _dot(om[rows, :], _unpack(wout_ref[...]))
        if final_norm:
            y = _rms_scale(y) * fg_ref[...]
        y_ref[rows, :] = y


N_PLAIN_OPERANDS = 15
N_SHARED = 2
OUT_PROMPT_STATES, OUT_SAMPLE_STATES = 1, 3


def _prompt_layer(x, win, wout, rope, qdec, kdec, mask, ws, gb, lng, lnb, fg, qkv, states, shared, *,
                  layer, chunks_per_step, final_norm):
    batch, seq, _ = x.shape
    depth, n_rows = states.shape[:2]
    aliased = shared is not None
    tok = chunks_per_step * CHUNK
    steps_per_row = seq // tok
    heads = n_rows * HEADS // (batch * steps_per_row * SUBLANES)
    head_blocks = HEADS // heads
    assert heads * head_blocks == HEADS and batch * steps_per_row == (n_rows // SUBLANES) * head_blocks
    tile = lambda b, t: ((b * steps_per_row + t) // head_blocks, (b * steps_per_row + t) % head_blocks)

    row_spec = pl.BlockSpec((None, tok, D_MODEL), lambda b, t: (b, t, 0))
    pos_spec = pl.BlockSpec((2, tok, HD), lambda b, t: (0, t, 0))
    tile_spec = pl.BlockSpec((SUBLANES, heads * HD), tile)
    qkv_spec = pl.BlockSpec((3, SUBLANES, heads * HD), lambda b, t: (0, *tile(b, t)))
    state_spec = pl.BlockSpec((None, SUBLANES, heads, HD, HD), lambda b, t: (layer, *tile(b, t), 0, 0))
    all_layers_spec = pl.BlockSpec((depth, SUBLANES, heads, HD, HD), lambda b, t: (0, *tile(b, t), 0, 0))
    if aliased:
        pstate_spec = pl.BlockSpec((None, None, HEADS, HD, HD), lambda b, t: (layer, b, 0, 0, 0))
    else:
        pstate_spec = pl.BlockSpec((depth, None, HEADS, HD, HD), lambda b, t: (0, b, 0, 0, 0))
    gamma = jnp.asarray(np.broadcast_to(np.asarray(GAMMA)[:, None, None], (HEADS, 1, HD)), F32)
    bf16_rows = lambda width: pltpu.VMEM((tok, width), BF16)
    per_chunk_head = lambda dtype: pltpu.VMEM((chunks_per_step, HEADS, CHUNK, CHUNK), dtype)

    return pl.pallas_call(
        functools.partial(_prompt_kernel, layer=layer, final_norm=final_norm, aliased=aliased),
        grid=(batch, steps_per_row),
        in_specs=[
            row_spec,
            _resident_spec((D_MODEL // 2, N_SEG * WIDTH)),
            _resident_spec((MIX // 2, D_MODEL)),
            pos_spec,
            _const_spec((CHUNK, WIDTH)), _const_spec((CHUNK, WIDTH)),
            _const_spec((HEADS, CHUNK, CHUNK)),
            _layer_spec((HEADS, CHUNK, CHUNK), layer),
            _const_spec((CHUNK, WIDTH)),
            _const_spec((depth, WIDTH)), _const_spec((depth, WIDTH)),
            _const_spec((1, D_MODEL)),
            _const_spec((HEADS, 1, HD)),
            qkv_spec,
            state_spec,
            *[pl.BlockSpec(memory_space=pl.ANY)] * (N_SHARED * aliased),
        ],
        out_specs=[row_spec, pstate_spec, tile_spec, state_spec if aliased else all_layers_spec],
        out_shape=[jax.ShapeDtypeStruct(x.shape, F32),
                   jax.ShapeDtypeStruct((depth, batch, HEADS, HD, HD), F32),
                   jax.ShapeDtypeStruct((n_rows, WIDTH), F32),
                   jax.ShapeDtypeStruct(states.shape, F32)],
        input_output_aliases=({N_PLAIN_OPERANDS: OUT_PROMPT_STATES, N_PLAIN_OPERANDS + 1: OUT_SAMPLE_STATES}
                              if aliased else {}),
        scratch_shapes=[
            bf16_rows(D_MODEL),
            bf16_rows(WIDTH),
            per_chunk_head(BF16),
            bf16_rows(WIDTH),
            pltpu.VMEM((tok, WIDTH), F32),
            pltpu.VMEM((tok, WIDTH), F32),
            bf16_rows(WIDTH),
            bf16_rows(MIX),
            pltpu.VMEM((HEADS, CHUNK, CHUNK), BF16),
            per_chunk_head(BF16),
            per_chunk_head(F32),
            per_chunk_head(BF16),
        ],
        compiler_params=pltpu.CompilerParams(
            dimension_semantics=("arbitrary", "arbitrary"),
            vmem_limit_bytes=VMEM_LIMIT_BYTES,
        ),
        name="prompt_layer",
    )(x, win, wout, rope, qdec, kdec, mask, ws, gb, lng, lnb, fg, gamma, qkv, states,
      *(shared if aliased else ()))


N_WOUT_BLOCKS = 8
N_START_STEPS = max(N_SEG, N_WOUT_BLOCKS)
N_WIN_BUFFERS = 3
WIN_DMA_PRIORITY = 1


def _finish_layer(h, scv_ref, opart_ref, gr_ref, m_ref, wout_ref):
    parts = []
    for hd in range(HEADS):
        hs = _head(hd)
        o = scv_ref[:, hs] + GAMMA[hd] * opart_ref[:, hs]
        parts.append((_rms_scale(o) * gr_ref[:, hs]).astype(BF16))
    om = jnp.concatenate(parts + [m_ref[...]], axis=1)
    return h + _dot(om, _unpack(wout_ref[...]))


def _sample_start_kernel(*refs, post, layer):
    h_ref, refs = refs[0], refs[1:]
    if post:
        post_refs, vn_prev_ref, refs = refs[:5], refs[5], refs[6:]
    ng_ref, win_ref, wout_ref, cos_ref, sin_ref, ws_ref, gmb_ref, lng_ref, lnb_ref = refs[:9]
    refs = refs[9:]
    if post:
        hout_ref, refs = refs[0], refs[1:]
    qkv_ref, scv_ref, gr_ref, m_ref, vn_ref, gb_ref, winb_ref, woutb_ref, xb, u_keep, vn_keep, win_buf, win_sem = refs
    q_ref, k_ref, v_ref = (qkv_ref.at[i] for i in range(3))
    j = pl.program_id(0)

    def win_copy(segment, slot):
        cols = pl.ds(pl.multiple_of(segment * WIDTH, WIDTH), WIDTH)
        return pltpu.make_async_copy(win_ref.at[layer, :, cols], win_buf.at[slot], win_sem.at[slot])

    @pl.when(j == 0)
    def _prefill():
        for s in range(min(N_WIN_BUFFERS, N_SEG)):
            win_copy(s, s).start(priority=WIN_DMA_PRIORITY)

    @pl.when(j == 0)
    def _tokens():
        h = h_ref[...]
        if post:
            h = _finish_layer(h, *post_refs)
            hout_ref[...] = h
        xb[...] = _rms_scale(h).astype(BF16)

    @pl.when(j < N_SEG)
    def _segment():
        gain = ng_ref[layer:layer + 1, :]
        tiles = [gain[:, i * LANES:(i + 1) * LANES] for i in range(D_MODEL // LANES)]
        pad = jnp.zeros((LANES - len(tiles), LANES), F32)
        gain_cols = jnp.concatenate(tiles + [pad], axis=0).T
        slot = lax.rem(j, N_WIN_BUFFERS)
        win_copy(j, slot).wait()
        for i in range(D_MODEL // LANES):
            rows = slice(i * LANES, (i + 1) * LANES)
            packed_rows = slice(i * LANES // 2, (i + 1) * LANES // 2)
            winb_ref[packed_rows, :] = _pack((win_buf[slot, rows, :] * gain_cols[:, i:i + 1]).astype(BF16))

        @pl.when(j + N_WIN_BUFFERS < N_SEG)
        def _refill():
            win_copy(j + N_WIN_BUFFERS, slot).start(priority=WIN_DMA_PRIORITY)

        acc = _dot(xb[...], _unpack(winb_ref[...]))

        @pl.when(j == SEG_Q)
        def _():
            for hd in range(HEADS):
                q_ref[:, _head(hd)] = _rope(acc[:, _head(hd)], cos_ref[...], sin_ref[...])

        @pl.when(j == SEG_K)
        def _():
            for hd in range(HEADS):
                hs = _head(hd)
                k = _rope(acc[:, hs], cos_ref[...], sin_ref[...]) * K_SCALE
                k_ref[:, hs] = k
                qk = jnp.sum(q_ref[:, hs] * k, axis=-1, keepdims=True)
                scv_ref[:, hs] = jnp.broadcast_to(qk, k.shape)

        @pl.when(j == SEG_V)
        def _():
            v_ref[...] = acc
            scv_ref[...] = scv_ref[...] * acc

        @pl.when(j == SEG_GR)
        def _():
            gr_ref[...] = _silu(acc)

        @pl.when(j == SEG_U)
        def _():
            u_keep[...] = acc

        @pl.when(j == SEG_VG)
        def _():
            vn = _layernorm(acc, lng_ref[layer:layer + 1, :], lnb_ref[layer:layer + 1, :])
            vn_keep[...] = vn
            if post:
                vn_ref[0:layer] = vn_prev_ref[...]
            vn_ref[layer] = jnp.swapaxes(jnp.stack([vn[:, _head(hd)] for hd in range(HEADS)], axis=0), 0, 1)

        @pl.when(j == SEG_GG)
        def _():
            for hd in range(HEADS):
                hs = _head(hd)
                s = vn_keep[:, hs] * ws_ref[hd, 0:1, 0:1] + gmb_ref[layer, hd:hd + 1, 0:1]
                m_ref[:, hs] = (u_keep[:, hs] * _silu(acc[:, hs]) * s).astype(BF16)

    woutb_ref[...] = _pack(wout_ref[...].astype(BF16))

    @pl.when(j == pl.num_programs(0) - 1)
    def _bias_table():
        b = gmb_ref[layer]
        cols = jnp.concatenate([b, jnp.zeros((LANES - HEADS, CHUNK), F32)], axis=0).T
        for hd in range(HEADS):
            gb_ref[:, _head(hd)] = jnp.broadcast_to(cols[:, hd:hd + 1], (CHUNK, HD))


def _sample_final_kernel(h_ref, scv_ref, opart_ref, gr_ref, m_ref, wout_ref, fg_ref, y_ref):
    y_ref[...] = _rms_scale(_finish_layer(h_ref[...], scv_ref, opart_ref, gr_ref, m_ref, wout_ref)) * fg_ref[...]


def _token_spec(n_rows):
    return pl.BlockSpec((n_rows, None, D_MODEL), lambda *_: (0, 0, 0))


def _sample_start(h, post_args, ng, w_in, w_out, rope, ws, gmb, lng, lnb, *, layer):
    n_rows, depth = h.shape[0], w_in.shape[0]
    post = post_args is not None
    assert post == (layer > 0)
    vn_stack = (layer + 1, n_rows, HEADS, HD)
    full = _const_spec((n_rows, WIDTH))
    vecs = _const_spec((depth, WIDTH))
    seg = lambda j: jnp.minimum(j, N_SEG - 1)
    out_block = lambda j: jnp.minimum(j, N_WOUT_BLOCKS - 1)
    f32_rows = jax.ShapeDtypeStruct((n_rows, WIDTH), F32)
    args = [h] + (list(post_args) if post else []) + [ng, w_in, w_out, *rope, ws, gmb, lng, lnb]
    in_specs = [_token_spec(n_rows) if h.ndim == 3 else full]
    if post:
        in_specs += [full, full, full, full, _resident_spec((MIX // 2, D_MODEL)), _const_spec((layer, *vn_stack[1:]))]
    in_specs += [vecs,
                 pl.BlockSpec(memory_space=pl.ANY),
                 pl.BlockSpec((None, MIX // N_WOUT_BLOCKS, D_MODEL), lambda j: (layer, out_block(j), 0)),
                 _const_spec((1, HD)), _const_spec((1, HD)),
                 _layer_spec((HEADS, CHUNK, CHUNK), layer), _const_spec((depth, HEADS, CHUNK)), vecs, vecs]
    out_shape = (([f32_rows] if post else []) + [jax.ShapeDtypeStruct((3, n_rows, WIDTH), F32)] + [f32_rows] * 2
                 + [jax.ShapeDtypeStruct((n_rows, WIDTH), BF16), jax.ShapeDtypeStruct(vn_stack, F32),
                    jax.ShapeDtypeStruct((CHUNK, WIDTH), F32),
                    jax.ShapeDtypeStruct((D_MODEL // 2, N_SEG * WIDTH), jnp.uint32),
                    jax.ShapeDtypeStruct((MIX // 2, D_MODEL), jnp.uint32)])
    out_specs = [full] * post + [_const_spec((3, n_rows, WIDTH))] + [full] * 3 + [_const_spec(vn_stack), _const_spec((CHUNK, WIDTH)),
                                                 pl.BlockSpec((D_MODEL // 2, WIDTH), lambda j: (0, seg(j))),
                                                 pl.BlockSpec((MIX // N_WOUT_BLOCKS // 2, D_MODEL), lambda j: (out_block(j), 0))]
    return pl.pallas_call(
        functools.partial(_sample_start_kernel, post=post, layer=layer),
        grid=(N_START_STEPS,),
        in_specs=in_specs,
        out_specs=out_specs,
        out_shape=out_shape,
        scratch_shapes=[pltpu.VMEM((n_rows, D_MODEL), BF16),
                        pltpu.VMEM((n_rows, WIDTH), F32),
                        pltpu.VMEM((n_rows, WIDTH), F32),
                        pltpu.VMEM((N_WIN_BUFFERS, D_MODEL, WIDTH), F32),
                        pltpu.SemaphoreType.DMA((N_WIN_BUFFERS,))],
        compiler_params=pltpu.CompilerParams(
            dimension_semantics=("arbitrary",),
            vmem_limit_bytes=VMEM_LIMIT_BYTES,
        ),
        name="sample_start",
    )(*args)


def _sample_final(h, post_args, fg):
    n_rows = h.shape[0]
    full = _const_spec((n_rows, WIDTH))
    return pl.pallas_call(
        _sample_final_kernel,
        grid=(1,),
        in_specs=[full, full, full, full, full, _const_spec((MIX // 2, D_MODEL)), _const_spec((1, D_MODEL))],
        out_specs=_token_spec(n_rows),
        out_shape=jax.ShapeDtypeStruct((n_rows, 1, D_MODEL), F32),
        compiler_params=pltpu.CompilerParams(
            dimension_semantics=("arbitrary",),
            vmem_limit_bytes=VMEM_LIMIT_BYTES,
        ),
        name="sample_final",
    )(h, *post_args, fg)


def _rope_tables(pos):
    inv = ROPE_BASE ** (-np.arange(0, HD, 2, dtype=np.float64) / HD)
    ang = np.asarray(pos, np.float64)[:, None] * inv[None, :]
    c, s = np.cos(ang), np.sin(ang)
    return (jnp.asarray(np.concatenate([c, c], axis=-1), F32),
            jnp.asarray(np.concatenate([-s, s], axis=-1), F32))


def _retention_tables():
    lg = np.log(np.asarray(GAMMA, np.float64))
    idx = np.arange(CHUNK, dtype=np.float64)
    causal = idx[:, None] >= idx[None, :]
    mask = np.where(causal[None], np.exp(-lg * CHUNK)[:, None, None], 0.0)
    q_dec = np.exp(lg[None, :] * (idx[:, None] + 1.0))
    k_dec = np.exp(lg[None, :] * (CHUNK - 1.0 - idx[:, None])) * K_SCALE
    per_lane = lambda a: np.repeat(a, HD, axis=1)
    return jnp.asarray(mask, F32), jnp.asarray(per_lane(q_dec), F32), jnp.asarray(per_lane(k_dec), F32)


def kernel(x_prompt, x_sample, state_ret, norm_g, w_in, w_out, gm_ws, gm_b, gm_ln_g, gm_ln_b, final_g):
    depth = w_in.shape[0]
    batch, seq, _ = x_prompt.shape
    n_rows = x_sample.shape[0]
    assert x_sample.shape[1] == 1

    rope_p = jnp.stack(_rope_tables(np.arange(seq)))
    rope_s = _rope_tables(PAST_LEN + np.arange(1))
    mask, q_dec, k_dec = _retention_tables()
    fg = final_g.reshape(1, D_MODEL)
    start = functools.partial(_sample_start, ng=norm_g, w_in=w_in, w_out=w_out, rope=rope_s, ws=gm_ws, gmb=gm_b,
                              lng=gm_ln_g, lnb=gm_ln_b)

    h_p, h_s, shared = x_prompt, x_sample, None
    qkv, scv, gr, m, vn, gb, win_b, wout_b = start(h_s, None, layer=0)
    for l in range(depth):
        h_p, states_p, opart, states_s = _prompt_layer(
            h_p, win_b, wout_b, rope_p, q_dec, k_dec, mask, gm_ws, gb, gm_ln_g, gm_ln_b, fg,
            qkv, state_ret, shared, layer=l, chunks_per_step=CHUNKS_PER_STEP, final_norm=l == depth - 1)
        shared = (states_p, states_s)
        post_args = (scv, opart, gr, m, wout_b)
        if l == depth - 1:
            y_s = _sample_final(h_s, post_args, fg)
        else:
            h_s, qkv, scv, gr, m, vn, gb, win_b, wout_b = start(h_s, (*post_args, vn), layer=l + 1)

    return h_p, y_s, states_p, states_s, vn.reshape(depth, n_rows, 1, HEADS, HD)
```

```python
import functools

import jax
import jax.numpy as jnp
import numpy as np
from jax import lax
from jax.experimental import pallas as pl
from jax.experimental.pallas import tpu as pltpu

F32 = jnp.float32
BF16 = jnp.bfloat16

D_MODEL = 1024
HEADS = 8
HD = 128
LANES, SUBLANES = 128, 8
WIDTH = HEADS * HD
N_SEG = 7
MIX = 2 * WIDTH
CHUNK = 128
PAST_LEN = 16384
ROPE_BASE = 10000.0
EPS = 1e-6
SEG_Q, SEG_K, SEG_V, SEG_GR, SEG_U, SEG_VG, SEG_GG = range(N_SEG)

GAMMA = tuple(1.0 - 2.0 ** (-5.0 - h) for h in range(HEADS))
CHUNK_DECAY = tuple(g ** CHUNK for g in GAMMA)
K_SCALE = HD ** -0.5

CHUNKS_PER_STEP = 4
V7X_VMEM_BYTES = 64 * 1024 * 1024
VMEM_LIMIT_BYTES = V7X_VMEM_BYTES - 2 * 1024 * 1024


def _silu(x):
    return x * (1.0 / (1.0 + jnp.exp(-x)))


def _rms_scale(x):
    return x * lax.rsqrt(jnp.mean(x * x, axis=-1, keepdims=True) + EPS)


def _layernorm(x, g, b):
    mu = jnp.mean(x, axis=-1, keepdims=True)
    xc = x - mu
    var = jnp.mean(xc * xc, axis=-1, keepdims=True)
    return xc * lax.rsqrt(var + EPS) * g + b


def _rope(x, cos, sin):
    return x * cos + pltpu.roll(x, HD // 2, 1) * sin


def _head(h):
    return slice(h * HD, (h + 1) * HD)


def _dot(a, b):
    return jnp.dot(a, b, preferred_element_type=F32)


def _pack(w):
    return pltpu.bitcast(w, jnp.uint32)


def _unpack(words):
    return pltpu.bitcast(words, BF16)


def _const_spec(shape):
    zeros = (0,) * len(shape)
    return pl.BlockSpec(shape, lambda *_: zeros)


def _layer_spec(shape, layer, **kwargs):
    index = (layer,) + (0,) * len(shape)
    return pl.BlockSpec((None,) + shape, lambda *_: index, **kwargs)


def _resident_spec(shape):
    zeros = (0,) * len(shape)
    return pl.BlockSpec(shape, lambda *_: zeros, pipeline_mode=pl.Buffered(1))


def _chunk_rows(c):
    return slice(c * CHUNK, (c + 1) * CHUNK)


def _sample_state_step(q_ref, k_ref, v_ref, gamma_ref, s0_ref, write_new, opart_ref, head0):
    heads = opart_ref.shape[1] // HD
    tiles = [r[:, _head(hh)] for r in (k_ref, q_ref) for hh in range(heads)]
    pad = jnp.zeros((LANES - len(tiles) * SUBLANES, HD), F32)
    cols = jnp.concatenate(tiles + [pad], axis=0).T
    for hh in range(heads):
        for j in range(SUBLANES):
            s_old = s0_ref[j, hh]
            k_lane = hh * SUBLANES + j
            q_lane = heads * SUBLANES + k_lane
            k_col = jnp.broadcast_to(cols[:, k_lane:k_lane + 1], (HD, HD))
            q_col = jnp.broadcast_to(cols[:, q_lane:q_lane + 1], (HD, HD))
            write_new(j, hh, s_old * gamma_ref[head0 + hh] + k_col * v_ref[j:j + 1, _head(hh)])
            opart_ref[j:j + 1, _head(hh)] = jnp.sum(q_col * s_old, axis=0, keepdims=True)


def _prompt_kernel(x_ref, win_ref, wout_ref, rope_ref, qdec_ref, kdec_ref, mask_ref,
                   ws_ref, gb_ref, lng_ref, lnb_ref, fg_ref, gamma_ref, qkv_ref, s0_ref, *refs,
                   layer, final_norm, aliased):
    refs = refs[N_SHARED * aliased:]
    y_ref, sall_ref, opart_ref, snew_ref = refs[:4]
    xb, qdb, kdt, vb, gr, ug, vnb, om, wtril, sb, kvs, scb = refs[4:]
    s_ref = sall_ref if aliased else sall_ref.at[layer]
    lng, lnb = (r[layer:layer + 1, :] for r in (lng_ref, lnb_ref))

    t = pl.program_id(1)
    n_chunks = x_ref.shape[0] // CHUNK

    @pl.when(t == 0)
    def _start_of_row():
        sall_ref[...] = jnp.zeros(sall_ref.shape, F32)
        row = lax.broadcasted_iota(jnp.int32, (CHUNK, CHUNK), 0)
        col = lax.broadcasted_iota(jnp.int32, (CHUNK, CHUNK), 1)
        for h in range(HEADS):
            wtril[h] = jnp.where(row >= col, ws_ref[h], 0.0).astype(BF16)

    def proj(seg, pair=None):
        lo, width = (seg * WIDTH, WIDTH) if pair is None else (seg * WIDTH + pair * 2 * HD, 2 * HD)
        return _dot(xb[...], _unpack(win_ref[:, lo:lo + width]))

    pairs = range(HEADS // 2)
    pair_cols = lambda pair: slice(pair * 2 * HD, (pair + 1) * 2 * HD)

    xb[...] = _rms_scale(x_ref[...]).astype(BF16)

    head_blocks = WIDTH // opart_ref.shape[1]
    head0 = lax.rem(pl.program_id(0) * pl.num_programs(1) + t, head_blocks) * (HEADS // head_blocks)
    if aliased:
        def write_new(j, hh, s_new):
            snew_ref[j, hh] = s_new
    else:
        def write_new(j, hh, s_new):
            for l in range(snew_ref.shape[0]):
                snew_ref[l, j, hh] = s_new if l == layer else jnp.zeros_like(s_new)
    _sample_state_step(*(qkv_ref.at[i] for i in range(3)), gamma_ref, s0_ref, write_new, opart_ref, head0)

    cos, sin = rope_ref[0], rope_ref[1]
    for pair in pairs:
        acc = proj(SEG_Q, pair)
        for i in range(2):
            h = 2 * pair + i
            r = _rope(acc[:, _head(i)], cos, sin)
            for c in range(n_chunks):
                rows = _chunk_rows(c)
                qdb[rows, _head(h)] = (r[rows] * qdec_ref[:, _head(h)]).astype(BF16)

    for pair in pairs:
        acc = proj(SEG_K, pair)
        for i in range(2):
            h = 2 * pair + i
            r = _rope(acc[:, _head(i)], cos, sin)
            for c in range(n_chunks):
                kd = r[_chunk_rows(c)] * kdec_ref[:, _head(h)]
                kdt[c, h] = kd.T.astype(BF16)

    for pair in pairs:
        vb[:, pair_cols(pair)] = proj(SEG_V, pair).astype(BF16)

    for c in range(n_chunks):
        rows = _chunk_rows(c)
        for h in range(HEADS):
            hs = _head(h)
            scb[c, h] = (_dot(qdb[rows, hs], kdt[c, h]) * mask_ref[h]).astype(BF16)
            kvs[c, h] = _dot(kdt[c, h], vb[rows, hs])

    for pair in pairs:
        gr[:, pair_cols(pair)] = _silu(proj(SEG_GR, pair))
    vnb[...] = _layernorm(proj(SEG_VG), lng, lnb).astype(BF16)

    for h in range(HEADS):
        s = s_ref[h]
        for c in range(n_chunks):
            sb[c, h] = s.astype(BF16)
            s = s * CHUNK_DECAY[h] + kvs[c, h]
        s_ref[h] = s

    for c in range(n_chunks):
        rows = _chunk_rows(c)
        for h in range(HEADS):
            hs = _head(h)
            lhs = jnp.concatenate([scb[c, h], qdb[rows, hs]], axis=1)
            rhs = jnp.concatenate([vb[rows, hs], sb[c, h]], axis=0)
            om[rows, hs] = (_rms_scale(_dot(lhs, rhs)) * gr[rows, hs]).astype(BF16)

    for pair in pairs:
        ug[:, pair_cols(pair)] = proj(SEG_U, pair) * _silu(proj(SEG_GG, pair))

    for c in range(n_chunks):
        rows = _chunk_rows(c)
        for h in range(HEADS):
            hs = _head(h)
            s = _dot(wtril[h], vnb[rows, hs]) + gb_ref[:, hs]
            om[rows, WIDTH + h * HD:WIDTH + (h + 1) * HD] = (ug[rows, hs] * s).astype(BF16)

    half = x_ref.shape[0] // 2
    for rows in (slice(0, half), slice(half, 2 * half)):
        y = x_ref[rows, :] + _dot(om[rows, :], _unpack(wout_ref[...]))
        if final_norm:
            y = _rms_scale(y) * fg_ref[...]
        y_ref[rows, :] = y


N_PLAIN_OPERANDS = 15
N_SHARED = 2
OUT_PROMPT_STATES, OUT_SAMPLE_STATES = 1, 3


def _prompt_layer(x, win, wout, rope, qdec, kdec, mask, ws, gb, lng, lnb, fg, qkv, states, shared, *,
                  layer, chunks_per_step, final_norm):
    batch, seq, _ = x.shape
    depth, n_rows = states.shape[:2]
    aliased = shared is not None
    tok = chunks_per_step * CHUNK
    steps_per_row = seq // tok
    heads = n_rows * HEADS // (batch * steps_per_row * SUBLANES)
    head_blocks = HEADS // heads
    assert heads * head_blocks == HEADS and batch * steps_per_row == (n_rows // SUBLANES) * head_blocks
    tile = lambda b, t: ((b * steps_per_row + t) // head_blocks, (b * steps_per_row + t) % head_blocks)

    row_spec = pl.BlockSpec((None, tok, D_MODEL), lambda b, t: (b, t, 0))
    pos_spec = pl.BlockSpec((2, tok, HD), lambda b, t: (0, t, 0))
    tile_spec = pl.BlockSpec((SUBLANES, heads * HD), tile)
    qkv_spec = pl.BlockSpec((3, SUBLANES, heads * HD), lambda b, t: (0, *tile(b, t)))
    state_spec = pl.BlockSpec((None, SUBLANES, heads, HD, HD), lambda b, t: (layer, *tile(b, t), 0, 0))
    all_layers_spec = pl.BlockSpec((depth, SUBLANES, heads, HD, HD), lambda b, t: (0, *tile(b, t), 0, 0))
    if aliased:
        pstate_spec = pl.BlockSpec((None, None, HEADS, HD, HD), lambda b, t: (layer, b, 0, 0, 0))
    else:
        pstate_spec = pl.BlockSpec((depth, None, HEADS, HD, HD), lambda b, t: (0, b, 0, 0, 0))
    gamma = jnp.asarray(np.broadcast_to(np.asarray(GAMMA)[:, None, None], (HEADS, 1, HD)), F32)
    bf16_rows = lambda width: pltpu.VMEM((tok, width), BF16)
    per_chunk_head = lambda dtype: pltpu.VMEM((chunks_per_step, HEADS, CHUNK, CHUNK), dtype)

    return pl.pallas_call(
        functools.partial(_prompt_kernel, layer=layer, final_norm=final_norm, aliased=aliased),
        grid=(batch, steps_per_row),
        in_specs=[
            row_spec,
            _resident_spec((D_MODEL // 2, N_SEG * WIDTH)),
            _resident_spec((MIX // 2, D_MODEL)),
            pos_spec,
            _const_spec((CHUNK, WIDTH)), _const_spec((CHUNK, WIDTH)),
            _const_spec((HEADS, CHUNK, CHUNK)),
            _layer_spec((HEADS, CHUNK, CHUNK), layer),
            _const_spec((CHUNK, WIDTH)),
            _const_spec((depth, WIDTH)), _const_spec((depth, WIDTH)),
            _const_spec((1, D_MODEL)),
            _const_spec((HEADS, 1, HD)),
            qkv_spec,
            state_spec,
            *[pl.BlockSpec(memory_space=pl.ANY)] * (N_SHARED * aliased),
        ],
        out_specs=[row_spec, pstate_spec, tile_spec, state_spec if aliased else all_layers_spec],
        out_shape=[jax.ShapeDtypeStruct(x.shape, F32),
                   jax.ShapeDtypeStruct((depth, batch, HEADS, HD, HD), F32),
                   jax.ShapeDtypeStruct((n_rows, WIDTH), F32),
                   jax.ShapeDtypeStruct(states.shape, F32)],
        input_output_aliases=({N_PLAIN_OPERANDS: OUT_PROMPT_STATES, N_PLAIN_OPERANDS + 1: OUT_SAMPLE_STATES}
                              if aliased else {}),
        scratch_shapes=[
            bf16_rows(D_MODEL),
            bf16_rows(WIDTH),
            per_chunk_head(BF16),
            bf16_rows(WIDTH),
            pltpu.VMEM((tok, WIDTH), F32),
            pltpu.VMEM((tok, WIDTH), F32),
            bf16_rows(WIDTH),
            bf16_rows(MIX),
            pltpu.VMEM((HEADS, CHUNK, CHUNK), BF16),
            per_chunk_head(BF16),
            per_chunk_head(F32),
            per_chunk_head(BF16),
        ],
        compiler_params=pltpu.CompilerParams(
            dimension_semantics=("arbitrary", "arbitrary"),
            vmem_limit_bytes=VMEM_LIMIT_BYTES,
        ),
        name="prompt_layer",
    )(x, win, wout, rope, qdec, kdec, mask, ws, gb, lng, lnb, fg, gamma, qkv, states,
      *(shared if aliased else ()))


N_WOUT_BLOCKS = 4
N_START_STEPS = max(N_SEG, N_WOUT_BLOCKS)
N_WIN_BUFFERS = 3
WIN_DMA_PRIORITY = 1


def _finish_layer(h, scv_ref, opart_ref, gr_ref, m_ref, wout_ref):
    parts = []
    for hd in range(HEADS):
        hs = _head(hd)
        o = scv_ref[:, hs] + GAMMA[hd] * opart_ref[:, hs]
        parts.append((_rms_scale(o) * gr_ref[:, hs]).astype(BF16))
    om = jnp.concatenate(parts + [m_ref[...]], axis=1)
    return h + _dot(om, _unpack(wout_ref[...]))


def _sample_start_kernel(*refs, post, layer):
    h_ref, refs = refs[0], refs[1:]
    if post:
        post_refs, vn_prev_ref, refs = refs[:5], refs[5], refs[6:]
    ng_ref, win_ref, wout_ref, cos_ref, sin_ref, ws_ref, gmb_ref, lng_ref, lnb_ref = refs[:9]
    refs = refs[9:]
    if post:
        hout_ref, refs = refs[0], refs[1:]
    qkv_ref, scv_ref, gr_ref, m_ref, vn_ref, gb_ref, winb_ref, woutb_ref, xb, u_keep, vn_keep, win_buf, win_sem = refs
    q_ref, k_ref, v_ref = (qkv_ref.at[i] for i in range(3))
    j = pl.program_id(0)

    def win_copy(segment, slot):
        cols = pl.ds(pl.multiple_of(segment * WIDTH, WIDTH), WIDTH)
        return pltpu.make_async_copy(win_ref.at[layer, :, cols], win_buf.at[slot], win_sem.at[slot])

    @pl.when(j == 0)
    def _prefill():
        for s in range(min(N_WIN_BUFFERS, N_SEG)):
            win_copy(s, s).start(priority=WIN_DMA_PRIORITY)

    @pl.when(j == 0)
    def _tokens():
        h = h_ref[...]
        if post:
            h = _finish_layer(h, *post_refs)
            hout_ref[...] = h
        xb[...] = _rms_scale(h).astype(BF16)

    @pl.when(j < N_SEG)
    def _segment():
        gain = ng_ref[layer:layer + 1, :]
        tiles = [gain[:, i * LANES:(i + 1) * LANES] for i in range(D_MODEL // LANES)]
        pad = jnp.zeros((LANES - len(tiles), LANES), F32)
        gain_cols = jnp.concatenate(tiles + [pad], axis=0).T
        slot = lax.rem(j, N_WIN_BUFFERS)
        win_copy(j, slot).wait()
        for i in range(D_MODEL // LANES):
            rows = slice(i * LANES, (i + 1) * LANES)
            packed_rows = slice(i * LANES // 2, (i + 1) * LANES // 2)
            winb_ref[packed_rows, :] = _pack((win_buf[slot, rows, :] * gain_cols[:, i:i + 1]).astype(BF16))

        @pl.when(j + N_WIN_BUFFERS < N_SEG)
        def _refill():
            win_copy(j + N_WIN_BUFFERS, slot).start(priority=WIN_DMA_PRIORITY)

        acc = _dot(xb[...], _unpack(winb_ref[...]))

        @pl.when(j == SEG_Q)
        def _():
            for hd in range(HEADS):
                q_ref[:, _head(hd)] = _rope(acc[:, _head(hd)], cos_ref[...], sin_ref[...])

        @pl.when(j == SEG_K)
        def _():
            for hd in range(HEADS):
                hs = _head(hd)
                k = _rope(acc[:, hs], cos_ref[...], sin_ref[...]) * K_SCALE
                k_ref[:, hs] = k
                qk = jnp.sum(q_ref[:, hs] * k, axis=-1, keepdims=True)
                scv_ref[:, hs] = jnp.broadcast_to(qk, k.shape)

        @pl.when(j == SEG_V)
        def _():
            v_ref[...] = acc
            scv_ref[...] = scv_ref[...] * acc

        @pl.when(j == SEG_GR)
        def _():
            gr_ref[...] = _silu(acc)

        @pl.when(j == SEG_U)
        def _():
            u_keep[...] = acc

        @pl.when(j == SEG_VG)
        def _():
            vn = _layernorm(acc, lng_ref[layer:layer + 1, :], lnb_ref[layer:layer + 1, :])
            vn_keep[...] = vn
            if post:
                vn_ref[0:layer] = vn_prev_ref[...]
            vn_ref[layer] = jnp.swapaxes(jnp.stack([vn[:, _head(hd)] for hd in range(HEADS)], axis=0), 0, 1)

        @pl.when(j == SEG_GG)
        def _():
            for hd in range(HEADS):
                hs = _head(hd)
                s = vn_keep[:, hs] * ws_ref[hd, 0:1, 0:1] + gmb_ref[layer, hd:hd + 1, 0:1]
                m_ref[:, hs] = (u_keep[:, hs] * _silu(acc[:, hs]) * s).astype(BF16)

    @pl.when(j < N_WOUT_BLOCKS)
    def _out_weight():
        woutb_ref[...] = _pack(wout_ref[...].astype(BF16))

    @pl.when(j == pl.num_programs(0) - 1)
    def _bias_table():
        b = gmb_ref[layer]
        cols = jnp.concatenate([b, jnp.zeros((LANES - HEADS, CHUNK), F32)], axis=0).T
        for hd in range(HEADS):
            gb_ref[:, _head(hd)] = jnp.broadcast_to(cols[:, hd:hd + 1], (CHUNK, HD))


def _sample_final_kernel(h_ref, scv_ref, opart_ref, gr_ref, m_ref, wout_ref, fg_ref, y_ref):
    y_ref[...] = _rms_scale(_finish_layer(h_ref[...], scv_ref, opart_ref, gr_ref, m_ref, wout_ref)) * fg_ref[...]


def _token_spec(n_rows):
    return pl.BlockSpec((n_rows, None, D_MODEL), lambda *_: (0, 0, 0))


def _sample_start(h, post_args, ng, w_in, w_out, rope, ws, gmb, lng, lnb, *, layer):
    n_rows, depth = h.shape[0], w_in.shape[0]
    post = post_args is not None
    assert post == (layer > 0)
    vn_stack = (layer + 1, n_rows, HEADS, HD)
    full = _const_spec((n_rows, WIDTH))
    vecs = _const_spec((depth, WIDTH))
    seg = lambda j: jnp.minimum(j, N_SEG - 1)
    out_block = lambda j: jnp.minimum(j, N_WOUT_BLOCKS - 1)
    f32_rows = jax.ShapeDtypeStruct((n_rows, WIDTH), F32)
    args = [h] + (list(post_args) if post else []) + [ng, w_in, w_out, *rope, ws, gmb, lng, lnb]
    in_specs = [_token_spec(n_rows) if h.ndim == 3 else full]
    if post:
        in_specs += [full, full, full, full, _resident_spec((MIX // 2, D_MODEL)), _const_spec((layer, *vn_stack[1:]))]
    in_specs += [vecs,
                 pl.BlockSpec(memory_space=pl.ANY),
                 pl.BlockSpec((None, MIX // N_WOUT_BLOCKS, D_MODEL), lambda j: (layer, out_block(j), 0)),
                 _const_spec((1, HD)), _const_spec((1, HD)),
                 _layer_spec((HEADS, CHUNK, CHUNK), layer), _const_spec((depth, HEADS, CHUNK)), vecs, vecs]
    out_shape = (([f32_rows] if post else []) + [jax.ShapeDtypeStruct((3, n_rows, WIDTH), F32)] + [f32_rows] * 2
                 + [jax.ShapeDtypeStruct((n_rows, WIDTH), BF16), jax.ShapeDtypeStruct(vn_stack, F32),
                    jax.ShapeDtypeStruct((CHUNK, WIDTH), F32),
                    jax.ShapeDtypeStruct((D_MODEL // 2, N_SEG * WIDTH), jnp.uint32),
                    jax.ShapeDtypeStruct((MIX // 2, D_MODEL), jnp.uint32)])
    out_specs = [full] * post + [_const_spec((3, n_rows, WIDTH))] + [full] * 3 + [_const_spec(vn_stack), _const_spec((CHUNK, WIDTH)),
                                                 pl.BlockSpec((D_MODEL // 2, WIDTH), lambda j: (0, seg(j))),
                                                 pl.BlockSpec((MIX // N_WOUT_BLOCKS // 2, D_MODEL), lambda j: (out_block(j), 0))]
    return pl.pallas_call(
        functools.partial(_sample_start_kernel, post=post, layer=layer),
        grid=(N_START_STEPS,),
        in_specs=in_specs,
        out_specs=out_specs,
        out_shape=out_shape,
        scratch_shapes=[pltpu.VMEM((n_rows, D_MODEL), BF16),
                        pltpu.VMEM((n_rows, WIDTH), F32),
                        pltpu.VMEM((n_rows, WIDTH), F32),
                        pltpu.VMEM((N_WIN_BUFFERS, D_MODEL, WIDTH), F32),
                        pltpu.SemaphoreType.DMA((N_WIN_BUFFERS,))],
        compiler_params=pltpu.CompilerParams(
            dimension_semantics=("arbitrary",),
            vmem_limit_bytes=VMEM_LIMIT_BYTES,
        ),
        name="sample_start",
    )(*args)


def _sample_final(h, post_args, fg):
    n_rows = h.shape[0]
    full = _const_spec((n_rows, WIDTH))
    return pl.pallas_call(
        _sample_final_kernel,
        grid=(1,),
        in_specs=[full, full, full, full, full, _const_spec((MIX // 2, D_MODEL)), _const_spec((1, D_MODEL))],
        out_specs=_token_spec(n_rows),
        out_shape=jax.ShapeDtypeStruct((n_rows, 1, D_MODEL), F32),
        compiler_params=pltpu.CompilerParams(
            dimension_semantics=("arbitrary",),
            vmem_limit_bytes=VMEM_LIMIT_BYTES,
        ),
        name="sample_final",
    )(h, *post_args, fg)


def _rope_tables(pos):
    inv = ROPE_BASE ** (-np.arange(0, HD, 2, dtype=np.float64) / HD)
    ang = np.asarray(pos, np.float64)[:, None] * inv[None, :]
    c, s = np.cos(ang), np.sin(ang)
    return (jnp.asarray(np.concatenate([c, c], axis=-1), F32),
            jnp.asarray(np.concatenate([-s, s], axis=-1), F32))


def _retention_tables():
    lg = np.log(np.asarray(GAMMA, np.float64))
    idx = np.arange(CHUNK, dtype=np.float64)
    causal = idx[:, None] >= idx[None, :]
    mask = np.where(causal[None], np.exp(-lg * CHUNK)[:, None, None], 0.0)
    q_dec = np.exp(lg[None, :] * (idx[:, None] + 1.0))
    k_dec = np.exp(lg[None, :] * (CHUNK - 1.0 - idx[:, None])) * K_SCALE
    per_lane = lambda a: np.repeat(a, HD, axis=1)
    return jnp.asarray(mask, F32), jnp.asarray(per_lane(q_dec), F32), jnp.asarray(per_lane(k_dec), F32)


def kernel(x_prompt, x_sample, state_ret, norm_g, w_in, w_out, gm_ws, gm_b, gm_ln_g, gm_ln_b, final_g):
    depth = w_in.shape[0]
    batch, seq, _ = x_prompt.shape
    n_rows = x_sample.shape[0]
    assert x_sample.shape[1] == 1

    rope_p = jnp.stack(_rope_tables(np.arange(seq)))
    rope_s = _rope_tables(PAST_LEN + np.arange(1))
    mask, q_dec, k_dec = _retention_tables()
    fg = final_g.reshape(1, D_MODEL)
    start = functools.partial(_sample_start, ng=norm_g, w_in=w_in, w_out=w_out, rope=rope_s, ws=gm_ws, gmb=gm_b,
                              lng=gm_ln_g, lnb=gm_ln_b)

    h_p, h_s, shared = x_prompt, x_sample, None
    qkv, scv, gr, m, vn, gb, win_b, wout_b = start(h_s, None, layer=0)
    for l in range(depth):
        h_p, states_p, opart, states_s = _prompt_layer(
            h_p, win_b, wout_b, rope_p, q_dec, k_dec, mask, gm_ws, gb, gm_ln_g, gm_ln_b, fg,
            qkv, state_ret, shared, layer=l, chunks_per_step=CHUNKS_PER_STEP, final_norm=l == depth - 1)
        shared = (states_p, states_s)
        post_args = (scv, opart, gr, m, wout_b)
        if l == depth - 1:
            y_s = _sample_final(h_s, post_args, fg)
        else:
            h_s, qkv, scv, gr, m, vn, gb, win_b, wout_b = start(h_s, (*post_args, vn), layer=l + 1)

    return h_p, y_s, states_p, states_s, vn.reshape(depth, n_rows, 1, HEADS, HD)
```

```python
import functools

import jax
import jax.numpy as jnp
import numpy as np
from jax import lax
from jax.experimental import pallas as pl
from jax.experimental.pallas import tpu as pltpu

F32 = jnp.float32
BF16 = jnp.bfloat16

D_MODEL = 1024
HEADS = 8
HD = 128
LANES, SUBLANES = 128, 8
WIDTH = HEADS * HD
N_SEG = 7
MIX = 2 * WIDTH
CHUNK = 128
PAST_LEN = 16384
ROPE_BASE = 10000.0
EPS = 1e-6
SEG_Q, SEG_K, SEG_V, SEG_GR, SEG_U, SEG_VG, SEG_GG = range(N_SEG)

GAMMA = tuple(1.0 - 2.0 ** (-5.0 - h) for h in range(HEADS))
CHUNK_DECAY = tuple(g ** CHUNK for g in GAMMA)
K_SCALE = HD ** -0.5

CHUNKS_PER_STEP = 4
V7X_VMEM_BYTES = 64 * 1024 * 1024
VMEM_LIMIT_BYTES = V7X_VMEM_BYTES - 2 * 1024 * 1024


def _silu(x):
    return x * (1.0 / (1.0 + jnp.exp(-x)))


def _rms_scale(x):
    return x * lax.rsqrt(jnp.mean(x * x, axis=-1, keepdims=True) + EPS)


def _layernorm(x, g, b):
    mu = jnp.mean(x, axis=-1, keepdims=True)
    xc = x - mu
    var = jnp.mean(xc * xc, axis=-1, keepdims=True)
    return xc * lax.rsqrt(var + EPS) * g + b


def _rope(x, cos, sin):
    return x * cos + pltpu.roll(x, HD // 2, 1) * sin


def _head(h):
    return slice(h * HD, (h + 1) * HD)


def _dot(a, b):
    return jnp.dot(a, b, preferred_element_type=F32)


def _pack(w):
    return pltpu.bitcast(w, jnp.uint32)


def _unpack(words):
    return pltpu.bitcast(words, BF16)


def _const_spec(shape):
    zeros = (0,) * len(shape)
    return pl.BlockSpec(shape, lambda *_: zeros)


def _layer_spec(shape, layer, **kwargs):
    index = (layer,) + (0,) * len(shape)
    return pl.BlockSpec((None,) + shape, lambda *_: index, **kwargs)


def _resident_spec(shape):
    zeros = (0,) * len(shape)
    return pl.BlockSpec(shape, lambda *_: zeros, pipeline_mode=pl.Buffered(1))


def _chunk_rows(c):
    return slice(c * CHUNK, (c + 1) * CHUNK)


def _sample_state_step(q_ref, k_ref, v_ref, gamma_ref, s0_ref, write_new, opart_ref, head0):
    heads = opart_ref.shape[1] // HD
    tiles = [r[:, _head(hh)] for r in (k_ref, q_ref) for hh in range(heads)]
    pad = jnp.zeros((LANES - len(tiles) * SUBLANES, HD), F32)
    cols = jnp.concatenate(tiles + [pad], axis=0).T
    for hh in range(heads):
        for j in range(SUBLANES):
            s_old = s0_ref[j, hh]
            k_lane = hh * SUBLANES + j
            q_lane = heads * SUBLANES + k_lane
            k_col = jnp.broadcast_to(cols[:, k_lane:k_lane + 1], (HD, HD))
            q_col = jnp.broadcast_to(cols[:, q_lane:q_lane + 1], (HD, HD))
            write_new(j, hh, s_old * gamma_ref[head0 + hh] + k_col * v_ref[j:j + 1, _head(hh)])
            opart_ref[j:j + 1, _head(hh)] = jnp.sum(q_col * s_old, axis=0, keepdims=True)


def _prompt_kernel(x_ref, win_ref, wout_ref, rope_ref, qdec_ref, kdec_ref, mask_ref,
                   ws_ref, gb_ref, lng_ref, lnb_ref, fg_ref, gamma_ref, qkv_ref, s0_ref, *refs,
                   layer, final_norm, aliased):
    refs = refs[N_SHARED * aliased:]
    y_ref, sall_ref, opart_ref, snew_ref = refs[:4]
    xb, qdb, kdt, vb, gr, ug, vnb, om, wtril, sb, kvs, scb = refs[4:]
    s_ref = sall_ref if aliased else sall_ref.at[layer]
    lng, lnb = (r[layer:layer + 1, :] for r in (lng_ref, lnb_ref))

    t = pl.program_id(1)
    n_chunks = x_ref.shape[0] // CHUNK

    @pl.when(t == 0)
    def _start_of_row():
        sall_ref[...] = jnp.zeros(sall_ref.shape, F32)
        row = lax.broadcasted_iota(jnp.int32, (CHUNK, CHUNK), 0)
        col = lax.broadcasted_iota(jnp.int32, (CHUNK, CHUNK), 1)
        for h in range(HEADS):
            wtril[h] = jnp.where(row >= col, ws_ref[h], 0.0).astype(BF16)

    def proj(seg, pair=None):
        lo, width = (seg * WIDTH, WIDTH) if pair is None else (seg * WIDTH + pair * 2 * HD, 2 * HD)
        return _dot(xb[...], _unpack(win_ref[:, lo:lo + width]))

    pairs = range(HEADS // 2)
    pair_cols = lambda pair: slice(pair * 2 * HD, (pair + 1) * 2 * HD)

    xb[...] = _rms_scale(x_ref[...]).astype(BF16)

    head_blocks = WIDTH // opart_ref.shape[1]
    head0 = lax.rem(pl.program_id(0) * pl.num_programs(1) + t, head_blocks) * (HEADS // head_blocks)
    if aliased:
        def write_new(j, hh, s_new):
            snew_ref[j, hh] = s_new
    else:
        def write_new(j, hh, s_new):
            for l in range(snew_ref.shape[0]):
                snew_ref[l, j, hh] = s_new if l == layer else jnp.zeros_like(s_new)
    _sample_state_step(*(qkv_ref.at[i] for i in range(3)), gamma_ref, s0_ref, write_new, opart_ref, head0)

    cos, sin = rope_ref[0], rope_ref[1]
    for pair in pairs:
        acc = proj(SEG_Q, pair)
        for i in range(2):
            h = 2 * pair + i
            r = _rope(acc[:, _head(i)], cos, sin)
            for c in range(n_chunks):
                rows = _chunk_rows(c)
                qdb[rows, _head(h)] = (r[rows] * qdec_ref[:, _head(h)]).astype(BF16)

    for pair in pairs:
        acc = proj(SEG_K, pair)
        for i in range(2):
            h = 2 * pair + i
            r = _rope(acc[:, _head(i)], cos, sin)
            for c in range(n_chunks):
                kd = r[_chunk_rows(c)] * kdec_ref[:, _head(h)]
                kdt[c, h] = kd.T.astype(BF16)

    for pair in pairs:
        vb[:, pair_cols(pair)] = proj(SEG_V, pair).astype(BF16)

    for c in range(n_chunks):
        rows = _chunk_rows(c)
        for h in range(HEADS):
            hs = _head(h)
            scb[c, h] = (_dot(qdb[rows, hs], kdt[c, h]) * mask_ref[h]).astype(BF16)
            kvs[c, h] = _dot(kdt[c, h], vb[rows, hs])

    for pair in pairs:
        gr[:, pair_cols(pair)] = _silu(proj(SEG_GR, pair))
    vnb[...] = _layernorm(proj(SEG_VG), lng, lnb).astype(BF16)

    for h in range(HEADS):
        s = s_ref[h]
        for c in range(n_chunks):
            sb[c, h] = s.astype(BF16)
            s = s * CHUNK_DECAY[h] + kvs[c, h]
        s_ref[h] = s

    for c in range(n_chunks):
        rows = _chunk_rows(c)
        for h in range(HEADS):
            hs = _head(h)
            lhs = jnp.concatenate([scb[c, h], qdb[rows, hs]], axis=1)
            rhs = jnp.concatenate([vb[rows, hs], sb[c, h]], axis=0)
            om[rows, hs] = (_rms_scale(_dot(lhs, rhs)) * gr[rows, hs]).astype(BF16)

    for pair in pairs:
        ug[:, pair_cols(pair)] = proj(SEG_U, pair) * _silu(proj(SEG_GG, pair))

    for c in range(n_chunks):
        rows = _chunk_rows(c)
        for h in range(HEADS):
            hs = _head(h)
            s = _dot(wtril[h], vnb[rows, hs]) + gb_ref[:, hs]
            om[rows, WIDTH + h * HD:WIDTH + (h + 1) * HD] = (ug[rows, hs] * s).astype(BF16)

    half = x_ref.shape[0] // 2
    for rows in (slice(0, half), slice(half, 2 * half)):
        y = x_ref[rows, :] + _dot(om[rows, :], _unpack(wout_ref[...]))
        if final_norm:
            y = _rms_scale(y) * fg_ref[...]
        y_ref[rows, :] = y


N_PLAIN_OPERANDS = 15
N_SHARED = 2
OUT_PROMPT_STATES, OUT_SAMPLE_STATES = 1, 3


def _prompt_layer(x, win, wout, rope, qdec, kdec, mask, ws, gb, lng, lnb, fg, qkv, states, shared, *,
                  layer, chunks_per_step, final_norm):
    batch, seq, _ = x.shape
    depth, n_rows = states.shape[:2]
    aliased = shared is not None
    tok = chunks_per_step * CHUNK
    steps_per_row = seq // tok
    heads = n_rows * HEADS // (batch * steps_per_row * SUBLANES)
    head_blocks = HEADS // heads
    assert heads * head_blocks == HEADS and batch * steps_per_row == (n_rows // SUBLANES) * head_blocks
    tile = lambda b, t: ((b * steps_per_row + t) // head_blocks, (b * steps_per_row + t) % head_blocks)

    row_spec = pl.BlockSpec((None, tok, D_MODEL), lambda b, t: (b, t, 0))
    pos_spec = pl.BlockSpec((2, tok, HD), lambda b, t: (0, t, 0))
    tile_spec = pl.BlockSpec((SUBLANES, heads * HD), tile)
    qkv_spec = pl.BlockSpec((3, SUBLANES, heads * HD), lambda b, t: (0, *tile(b, t)))
    state_spec = pl.BlockSpec((None, SUBLANES, heads, HD, HD), lambda b, t: (layer, *tile(b, t), 0, 0))
    all_layers_spec = pl.BlockSpec((depth, SUBLANES, heads, HD, HD), lambda b, t: (0, *tile(b, t), 0, 0))
    if aliased:
        pstate_spec = pl.BlockSpec((None, None, HEADS, HD, HD), lambda b, t: (layer, b, 0, 0, 0))
    else:
        pstate_spec = pl.BlockSpec((depth, None, HEADS, HD, HD), lambda b, t: (0, b, 0, 0, 0))
    gamma = jnp.asarray(np.broadcast_to(np.asarray(GAMMA)[:, None, None], (HEADS, 1, HD)), F32)
    bf16_rows = lambda width: pltpu.VMEM((tok, width), BF16)
    per_chunk_head = lambda dtype: pltpu.VMEM((chunks_per_step, HEADS, CHUNK, CHUNK), dtype)

    return pl.pallas_call(
        functools.partial(_prompt_kernel, layer=layer, final_norm=final_norm, aliased=aliased),
        grid=(batch, steps_per_row),
        in_specs=[
            row_spec,
            _resident_spec((D_MODEL // 2, N_SEG * WIDTH)),
            _resident_spec((MIX // 2, D_MODEL)),
            pos_spec,
            _const_spec((CHUNK, WIDTH)), _const_spec((CHUNK, WIDTH)),
            _const_spec((HEADS, CHUNK, CHUNK)),
            _layer_spec((HEADS, CHUNK, CHUNK), layer),
            _const_spec((CHUNK, WIDTH)),
            _const_spec((depth, WIDTH)), _const_spec((depth, WIDTH)),
            _const_spec((1, D_MODEL)),
            _const_spec((HEADS, 1, HD)),
            qkv_spec,
            state_spec,
            *[pl.BlockSpec(memory_space=pl.ANY)] * (N_SHARED * aliased),
        ],
        out_specs=[row_spec, pstate_spec, tile_spec, state_spec if aliased else all_layers_spec],
        out_shape=[jax.ShapeDtypeStruct(x.shape, F32),
                   jax.ShapeDtypeStruct((depth, batch, HEADS, HD, HD), F32),
                   jax.ShapeDtypeStruct((n_rows, WIDTH), F32),
                   jax.ShapeDtypeStruct(states.shape, F32)],
        input_output_aliases=({N_PLAIN_OPERANDS: OUT_PROMPT_STATES, N_PLAIN_OPERANDS + 1: OUT_SAMPLE_STATES}
                              if aliased else {}),
        scratch_shapes=[
            bf16_rows(D_MODEL),
            bf16_rows(WIDTH),
            per_chunk_head(BF16),
            bf16_rows(WIDTH),
            pltpu.VMEM((tok, WIDTH), F32),
            pltpu.VMEM((tok, WIDTH), F32),
            bf16_rows(WIDTH),
            bf16_rows(MIX),
            pltpu.VMEM((HEADS, CHUNK, CHUNK), BF16),
            per_chunk_head(BF16),
            per_chunk_head(F32),
            per_chunk_head(BF16),
        ],
        compiler_params=pltpu.CompilerParams(
            dimension_semantics=("arbitrary", "arbitrary"),
            vmem_limit_bytes=VMEM_LIMIT_BYTES,
        ),
        name="prompt_layer",
    )(x, win, wout, rope, qdec, kdec, mask, ws, gb, lng, lnb, fg, gamma, qkv, states,
      *(shared if aliased else ()))


N_WOUT_BLOCKS = 4
N_START_STEPS = max(N_SEG, N_WOUT_BLOCKS)
N_WIN_BUFFERS = 3
WIN_DMA_PRIORITY = 1


def _finish_layer(h, scv_ref, opart_ref, gr_ref, m_ref, wout_ref):
    parts = []
    for hd in range(HEADS):
        hs = _head(hd)
        o = scv_ref[:, hs] + GAMMA[hd] * opart_ref[:, hs]
        parts.append((_rms_scale(o) * gr_ref[:, hs]).astype(BF16))
    om = jnp.concatenate(parts + [m_ref[...]], axis=1)
    return h + _dot(om, _unpack(wout_ref[...]))


def _sample_start_kernel(*refs, post, layer):
    h_ref, refs = refs[0], refs[1:]
    if post:
        post_refs, vn_prev_ref, refs = refs[:5], refs[5], refs[6:]
    ng_ref, win_ref, wout_ref, cos_ref, sin_ref, ws_ref, gmb_ref, lng_ref, lnb_ref = refs[:9]
    refs = refs[9:]
    if post:
        hout_ref, refs = refs[0], refs[1:]
    qkv_ref, scv_ref, gr_ref, m_ref, vn_ref, gb_ref, winb_ref, woutb_ref, xb, u_keep, vn_keep, win_buf, win_sem = refs
    q_ref, k_ref, v_ref = (qkv_ref.at[i] for i in range(3))
    j = pl.program_id(0)

    def win_copy(segment, slot):
        cols = pl.ds(pl.multiple_of(segment * WIDTH, WIDTH), WIDTH)
        return pltpu.make_async_copy(win_ref.at[layer, :, cols], win_buf.at[slot], win_sem.at[slot])

    @pl.when(j == 0)
    def _prefill():
        for s in range(min(N_WIN_BUFFERS, N_SEG)):
            win_copy(s, s).start(priority=(WIN_DMA_PRIORITY + s) % 2)

    @pl.when(j == 0)
    def _tokens():
        h = h_ref[...]
        if post:
            h = _finish_layer(h, *post_refs)
            hout_ref[...] = h
        xb[...] = _rms_scale(h).astype(BF16)

    @pl.when(j < N_SEG)
    def _segment():
        gain = ng_ref[layer:layer + 1, :]
        tiles = [gain[:, i * LANES:(i + 1) * LANES] for i in range(D_MODEL // LANES)]
        pad = jnp.zeros((LANES - len(tiles), LANES), F32)
        gain_cols = jnp.concatenate(tiles + [pad], axis=0).T
        slot = lax.rem(j, N_WIN_BUFFERS)
        win_copy(j, slot).wait()
        for i in range(D_MODEL // LANES):
            rows = slice(i * LANES, (i + 1) * LANES)
            packed_rows = slice(i * LANES // 2, (i + 1) * LANES // 2)
            winb_ref[packed_rows, :] = _pack((win_buf[slot, rows, :] * gain_cols[:, i:i + 1]).astype(BF16))

        @pl.when(j + N_WIN_BUFFERS < N_SEG)
        def _refill():
            win_copy(j + N_WIN_BUFFERS, slot).start(priority=WIN_DMA_PRIORITY)

        acc = _dot(xb[...], _unpack(winb_ref[...]))

        @pl.when(j == SEG_Q)
        def _():
            for hd in range(HEADS):
                q_ref[:, _head(hd)] = _rope(acc[:, _head(hd)], cos_ref[...], sin_ref[...])

        @pl.when(j == SEG_K)
        def _():
            for hd in range(HEADS):
                hs = _head(hd)
                k = _rope(acc[:, hs], cos_ref[...], sin_ref[...]) * K_SCALE
                k_ref[:, hs] = k
                qk = jnp.sum(q_ref[:, hs] * k, axis=-1, keepdims=True)
                scv_ref[:, hs] = jnp.broadcast_to(qk, k.shape)

        @pl.when(j == SEG_V)
        def _():
            v_ref[...] = acc
            scv_ref[...] = scv_ref[...] * acc

        @pl.when(j == SEG_GR)
        def _():
            gr_ref[...] = _silu(acc)

        @pl.when(j == SEG_U)
        def _():
            u_keep[...] = acc

        @pl.when(j == SEG_VG)
        def _():
            vn = _layernorm(acc, lng_ref[layer:layer + 1, :], lnb_ref[layer:layer + 1, :])
            vn_keep[...] = vn
            if post:
                vn_ref[0:layer] = vn_prev_ref[...]
            vn_ref[layer] = jnp.swapaxes(jnp.stack([vn[:, _head(hd)] for hd in range(HEADS)], axis=0), 0, 1)

        @pl.when(j == SEG_GG)
        def _():
            for hd in range(HEADS):
                hs = _head(hd)
                s = vn_keep[:, hs] * ws_ref[hd, 0:1, 0:1] + gmb_ref[layer, hd:hd + 1, 0:1]
                m_ref[:, hs] = (u_keep[:, hs] * _silu(acc[:, hs]) * s).astype(BF16)

    @pl.when(j < N_WOUT_BLOCKS)
    def _out_weight():
        woutb_ref[...] = _pack(wout_ref[...].astype(BF16))

    @pl.when(j == pl.num_programs(0) - 1)
    def _bias_table():
        b = gmb_ref[layer]
        cols = jnp.concatenate([b, jnp.zeros((LANES - HEADS, CHUNK), F32)], axis=0).T
        for hd in range(HEADS):
            gb_ref[:, _head(hd)] = jnp.broadcast_to(cols[:, hd:hd + 1], (CHUNK, HD))


def _sample_final_kernel(h_ref, scv_ref, opart_ref, gr_ref, m_ref, wout_ref, fg_ref, y_ref):
    y_ref[...] = _rms_scale(_finish_layer(h_ref[...], scv_ref, opart_ref, gr_ref, m_ref, wout_ref)) * fg_ref[...]


def _token_spec(n_rows):
    return pl.BlockSpec((n_rows, None, D_MODEL), lambda *_: (0, 0, 0))


def _sample_start(h, post_args, ng, w_in, w_out, rope, ws, gmb, lng, lnb, *, layer):
    n_rows, depth = h.shape[0], w_in.shape[0]
    post = post_args is not None
    assert post == (layer > 0)
    vn_stack = (layer + 1, n_rows, HEADS, HD)
    full = _const_spec((n_rows, WIDTH))
    vecs = _const_spec((depth, WIDTH))
    seg = lambda j: jnp.minimum(j, N_SEG - 1)
    out_block = lambda j: jnp.minimum(j, N_WOUT_BLOCKS - 1)
    f32_rows = jax.ShapeDtypeStruct((n_rows, WIDTH), F32)
    args = [h] + (list(post_args) if post else []) + [ng, w_in, w_out, *rope, ws, gmb, lng, lnb]
    in_specs = [_token_spec(n_rows) if h.ndim == 3 else full]
    if post:
        in_specs += [full, full, full, full, _resident_spec((MIX // 2, D_MODEL)), _const_spec((layer, *vn_stack[1:]))]
    in_specs += [vecs,
                 pl.BlockSpec(memory_space=pl.ANY),
                 pl.BlockSpec((None, MIX // N_WOUT_BLOCKS, D_MODEL), lambda j: (layer, out_block(j), 0)),
                 _const_spec((1, HD)), _const_spec((1, HD)),
                 _layer_spec((HEADS, CHUNK, CHUNK), layer), _const_spec((depth, HEADS, CHUNK)), vecs, vecs]
    out_shape = (([f32_rows] if post else []) + [jax.ShapeDtypeStruct((3, n_rows, WIDTH), F32)] + [f32_rows] * 2
                 + [jax.ShapeDtypeStruct((n_rows, WIDTH), BF16), jax.ShapeDtypeStruct(vn_stack, F32),
                    jax.ShapeDtypeStruct((CHUNK, WIDTH), F32),
                    jax.ShapeDtypeStruct((D_MODEL // 2, N_SEG * WIDTH), jnp.uint32),
                    jax.ShapeDtypeStruct((MIX // 2, D_MODEL), jnp.uint32)])
    out_specs = [full] * post + [_const_spec((3, n_rows, WIDTH))] + [full] * 3 + [_const_spec(vn_stack), _const_spec((CHUNK, WIDTH)),
                                                 pl.BlockSpec((D_MODEL // 2, WIDTH), lambda j: (0, seg(j))),
                                                 pl.BlockSpec((MIX // N_WOUT_BLOCKS // 2, D_MODEL), lambda j: (out_block(j), 0))]
    return pl.pallas_call(
        functools.partial(_sample_start_kernel, post=post, layer=layer),
        grid=(N_START_STEPS,),
        in_specs=in_specs,
        out_specs=out_specs,
        out_shape=out_shape,
        scratch_shapes=[pltpu.VMEM((n_rows, D_MODEL), BF16),
                        pltpu.VMEM((n_rows, WIDTH), F32),
                        pltpu.VMEM((n_rows, WIDTH), F32),
                        pltpu.VMEM((N_WIN_BUFFERS, D_MODEL, WIDTH), F32),
                        pltpu.SemaphoreType.DMA((N_WIN_BUFFERS,))],
        compiler_params=pltpu.CompilerParams(
            dimension_semantics=("arbitrary",),
            vmem_limit_bytes=VMEM_LIMIT_BYTES,
        ),
        name="sample_start",
    )(*args)


def _sample_final(h, post_args, fg):
    n_rows = h.shape[0]
    full = _const_spec((n_rows, WIDTH))
    return pl.pallas_call(
        _sample_final_kernel,
        grid=(1,),
        in_specs=[full, full, full, full, full, _const_spec((MIX // 2, D_MODEL)), _const_spec((1, D_MODEL))],
        out_specs=_token_spec(n_rows),
        out_shape=jax.ShapeDtypeStruct((n_rows, 1, D_MODEL), F32),
        compiler_params=pltpu.CompilerParams(
            dimension_semantics=("arbitrary",),
            vmem_limit_bytes=VMEM_LIMIT_BYTES,
        ),
        name="sample_final",
    )(h, *post_args, fg)


def _rope_tables(pos):
    inv = ROPE_BASE ** (-np.arange(0, HD, 2, dtype=np.float64) / HD)
    ang = np.asarray(pos, np.float64)[:, None] * inv[None, :]
    c, s = np.cos(ang), np.sin(ang)
    return (jnp.asarray(np.concatenate([c, c], axis=-1), F32),
            jnp.asarray(np.concatenate([-s, s], axis=-1), F32))


def _retention_tables():
    lg = np.log(np.asarray(GAMMA, np.float64))
    idx = np.arange(CHUNK, dtype=np.float64)
    causal = idx[:, None] >= idx[None, :]
    mask = np.where(causal[None], np.exp(-lg * CHUNK)[:, None, None], 0.0)
    q_dec = np.exp(lg[None, :] * (idx[:, None] + 1.0))
    k_dec = np.exp(lg[None, :] * (CHUNK - 1.0 - idx[:, None])) * K_SCALE
    per_lane = lambda a: np.repeat(a, HD, axis=1)
    return jnp.asarray(mask, F32), jnp.asarray(per_lane(q_dec), F32), jnp.asarray(per_lane(k_dec), F32)


def kernel(x_prompt, x_sample, state_ret, norm_g, w_in, w_out, gm_ws, gm_b, gm_ln_g, gm_ln_b, final_g):
    depth = w_in.shape[0]
    batch, seq, _ = x_prompt.shape
    n_rows = x_sample.shape[0]
    assert x_sample.shape[1] == 1

    rope_p = jnp.stack(_rope_tables(np.arange(seq)))
    rope_s = _rope_tables(PAST_LEN + np.arange(1))
    mask, q_dec, k_dec = _retention_tables()
    fg = final_g.reshape(1, D_MODEL)
    start = functools.partial(_sample_start, ng=norm_g, w_in=w_in, w_out=w_out, rope=rope_s, ws=gm_ws, gmb=gm_b,
                              lng=gm_ln_g, lnb=gm_ln_b)

    h_p, h_s, shared = x_prompt, x_sample, None
    qkv, scv, gr, m, vn, gb, win_b, wout_b = start(h_s, None, layer=0)
    for l in range(depth):
        h_p, states_p, opart, states_s = _prompt_layer(
            h_p, win_b, wout_b, rope_p, q_dec, k_dec, mask, gm_ws, gb, gm_ln_g, gm_ln_b, fg,
            qkv, state_ret, shared, layer=l, chunks_per_step=CHUNKS_PER_STEP, final_norm=l == depth - 1)
        shared = (states_p, states_s)
        post_args = (scv, opart, gr, m, wout_b)
        if l == depth - 1:
            y_s = _sample_final(h_s, post_args, fg)
        else:
            h_s, qkv, scv, gr, m, vn, gb, win_b, wout_b = start(h_s, (*post_args, vn), layer=l + 1)

    return h_p, y_s, states_p, states_s, vn.reshape(depth, n_rows, 1, HEADS, HD)
```
